```python
import math
import jax, jax.numpy as jnp
from jax import lax
import numpy as np

D_MODEL = 1024
BATCH = 16
SEQ = 256
DEPTH = 4
DEC_BATCH = 8
DEC_SEQ = 4096
PAST_LEN = 256

GRID_W = 64
FOU_GROUPS = 4
FOU_GROUP_W = 128
FOU_W = FOU_GROUPS * FOU_GROUP_W
GLA_HEADS = 4
GLA_DK = 64
GLA_DV = 128
GLA_K = GLA_HEADS * GLA_DK
GLA_V = GLA_HEADS * GLA_DV
GATE_RANK = 16
GATE_TEMP = 16.0
GLA_CHUNK = 64
DIFF_HEADS = 4
DIFF_HEAD_DIM = 64
DIFF_V_HEAD = 2 * DIFF_HEAD_DIM
DIFF_QK = DIFF_HEADS * 2 * DIFF_HEAD_DIM
DIFF_V = DIFF_HEADS * DIFF_V_HEAD
Q_BLOCK = 128
ROPE_AXIS_DIM = DIFF_HEAD_DIM // 2
ROPE_FREQS = ROPE_AXIS_DIM // 2
ROPE_BASE = 10000.0
N_BRANCH = 3
D_FF = ((8 * D_MODEL // 3 + 255) // 256) * 256
PROJ_SIZES = (FOU_W, GLA_K, GLA_K, GLA_V, GLA_V, 2 * GATE_RANK, DIFF_QK, DIFF_QK, DIFF_V, N_BRANCH * D_MODEL)
IN_COLS = FOU_W + 2 * GLA_K + 2 * GLA_V + 2 * GATE_RANK + 2 * DIFF_QK + DIFF_V + N_BRANCH * D_MODEL

kernel_name = 'hybrid_fnet_gla_diffattn_prefix_dit_step'


def rms_norm(x, gain, eps=1e-6):
    xf = x.astype(jnp.float32)
    y = xf * lax.rsqrt(jnp.mean(xf * xf, axis=-1, keepdims=True) + eps)
    return (y * gain.astype(jnp.float32)).astype(x.dtype)


def split_projection(proj):
    outs, start = [], 0
    for size in PROJ_SIZES:
        outs.append(proj[..., start:start + size])
        start += size
    return outs


def fourier_mix(u):
    b, t, _ = u.shape
    ug = u.astype(jnp.float32).reshape(b, t, FOU_GROUPS, FOU_GROUP_W)
    f = jnp.fft.fft2(ug, axes=(1, 3), norm='ortho').real
    return f.reshape(b, t, FOU_W).astype(u.dtype)


def gla_log_gates(a_lr, w_a2, b_a):
    b, t = a_lr.shape[:2]
    z = jnp.einsum('btdr,drk->btdk', a_lr.reshape(b, t, 2, GATE_RANK), w_a2) + b_a
    return (jax.nn.log_sigmoid(z.astype(jnp.float32)) / GATE_TEMP).reshape(b, t, 2, GLA_HEADS, GLA_DK)


def gla_scan(q, k, v, log_a, s0):
    b, t, h, dk = q.shape
    dv = v.shape[-1]
    n = t // GLA_CHUNK
    def chunks(a):
        return jnp.moveaxis(a.astype(jnp.float32).reshape(b, n, GLA_CHUNK, *a.shape[2:]), 1, 0)
    causal = jnp.tril(jnp.ones((GLA_CHUNK, GLA_CHUNK), dtype=bool))
    def step(s, inp):
        qc, kc, vc, gc = inp
        cum = jnp.cumsum(gc, axis=1)
        o_inter = jnp.einsum('bihk,bhkv->bihv', qc * jnp.exp(cum), s)
        rel = cum[:, :, None] - cum[:, None, :]
        decay = jnp.exp(jnp.where(causal[None, :, :, None, None], rel, -jnp.inf))
        att = jnp.einsum('bihk,bjhk,bijhk->bhij', qc, kc, decay)
        o = o_inter + jnp.einsum('bhij,bjhv->bihv', att, vc)
        last = cum[:, -1]
        k_dec = kc * jnp.exp(last[:, None] - cum)
        s_new = jnp.exp(last)[..., None] * s + jnp.einsum('bjhk,bjhv->bhkv', k_dec, vc)
        return s_new, o
    s_fin, o = lax.scan(step, s0.astype(jnp.float32), (chunks(q), chunks(k), chunks(v), chunks(log_a)))
    o = jnp.moveaxis(o, 0, 1).reshape(b, t, h, dv)
    return o, s_fin


def gla_bidirectional(q, k, v, log_a, s0):
    o_f, s_f = gla_scan(q, k, v, log_a[:, :, 0], s0[:, 0])
    flip = lambda a: jnp.flip(a, axis=1)
    o_b, s_b = gla_scan(flip(q), flip(k), flip(v), flip(log_a[:, :, 1]), s0[:, 1])
    return o_f + flip(o_b), jnp.stack([s_f, s_b], axis=1)


def axial_rope(n_tokens):
    rows = n_tokens // GRID_W
    row = jnp.repeat(jnp.arange(rows), GRID_W).astype(jnp.float32)
    col = jnp.tile(jnp.arange(GRID_W), rows).astype(jnp.float32)
    inv = ROPE_BASE ** (-jnp.arange(ROPE_FREQS, dtype=jnp.float32) * 2.0 / ROPE_AXIS_DIM)
    ang_r = row[:, None] * inv
    ang_c = col[:, None] * inv
    ang = jnp.concatenate([ang_r, ang_r, ang_c, ang_c], axis=-1)
    return jnp.cos(ang), jnp.sin(ang)


def rotate_half_axial(x):
    xs = x.reshape(*x.shape[:-1], 2, 2, ROPE_FREQS)
    return jnp.stack([-xs[..., 1, :], xs[..., 0, :]], axis=-2).reshape(x.shape)


def apply_rope(x, cos, sin):
    xf = x.astype(jnp.float32)
    cs = cos[None, :, None, None, :]
    sn = sin[None, :, None, None, :]
    return (xf * cs + rotate_half_axial(xf) * sn).astype(x.dtype)


def diff_attend(q, k, v, lam):
    b, tq, h, _, d = q.shape
    dv = v.shape[-1]
    nb = tq // Q_BLOCK
    qb = jnp.moveaxis(q.reshape(b, nb, Q_BLOCK, h, 2, d), 1, 0)
    kf = k.astype(jnp.float32)
    vf = v.astype(jnp.float32)
    scale = d ** -0.5
    def block(qblk):
        s = jnp.einsum('bqhmd,bkhmd->bhmqk', qblk.astype(jnp.float32), kf) * scale
        p = jax.nn.softmax(s, axis=-1)
        w = p[:, :, 0] - lam * p[:, :, 1]
        return jnp.einsum('bhqk,bkhv->bqhv', w, vf)
    o = lax.map(block, qb)
    return jnp.moveaxis(o, 0, 1).reshape(b, tq, h, dv).astype(v.dtype)


def trunk_layer(x, cond, layer_idx, p, rope, ctx):
    b, t, _ = x.shape
    mod = jnp.einsum('bd,de->be', jax.nn.silu(cond), p['w_mod']) + p['b_mod']
    sh1, sc1, g1, sh2, sc2, g2 = jnp.split(mod[:, None, :], 6, axis=-1)
    h = rms_norm(x, p['norm1']) * (1 + sc1) + sh1
    u_f, q_g, k_g, v_g, r_g, a_g, q_d, k_d, v_d, gates = split_projection(h @ p['w_in'])

    y_f = fourier_mix(u_f) @ p['w_fou']

    q_g = q_g.reshape(b, t, GLA_HEADS, GLA_DK) * (GLA_DK ** -0.5)
    k_g = k_g.reshape(b, t, GLA_HEADS, GLA_DK)
    v_g = v_g.reshape(b, t, GLA_HEADS, GLA_DV)
    log_a = gla_log_gates(a_g, p['w_gla_a2'], p['b_gla_a'])
    s0 = jnp.zeros((b, 2, GLA_HEADS, GLA_DK, GLA_DV), jnp.float32) if ctx is None else ctx[0]
    o_g, s_fin = gla_bidirectional(q_g, k_g, v_g, log_a, s0)
    o_g = rms_norm(o_g.astype(x.dtype), p['gla_norm']) * jax.nn.silu(r_g.reshape(b, t, GLA_HEADS, GLA_DV))
    y_g = o_g.reshape(b, t, GLA_V) @ p['w_gla_o']

    q_d = rms_norm(q_d.reshape(b, t, DIFF_HEADS, 2, DIFF_HEAD_DIM), p['diff_qk_norm'][0])
    k_d = rms_norm(k_d.reshape(b, t, DIFF_HEADS, 2, DIFF_HEAD_DIM), p['diff_qk_norm'][1])
    v_d = v_d.reshape(b, t, DIFF_HEADS, DIFF_V_HEAD)
    lp = p['diff_lambda'].astype(jnp.float32)
    lam_init = 0.8 - 0.6 * math.exp(-0.3 * layer_idx)
    lam = jnp.exp(jnp.sum(lp[0] * lp[1])) - jnp.exp(jnp.sum(lp[2] * lp[3])) + lam_init
    if ctx is None:
        o_d = diff_attend(q_d, k_d, v_d, lam)
    else:
        cos, sin = rope
        keys = jnp.concatenate([apply_rope(k_d, cos, sin), ctx[1].astype(k_d.dtype)], axis=1)
        vals = jnp.concatenate([v_d, ctx[2].astype(v_d.dtype)], axis=1)
        o_d = diff_attend(apply_rope(q_d, cos, sin), keys, vals, lam)
    o_d = rms_norm(o_d, p['diff_norm']) * (1.0 - lam_init)
    y_d = o_d.reshape(b, t, DIFF_V) @ p['w_diff_o']

    gf, gg, gd = jnp.split(jax.nn.sigmoid(gates), N_BRANCH, axis=-1)
    merged = gf * y_f + gg * y_g + gd * y_d
    x = x + g1 * (merged @ p['w_out'])

    h2 = rms_norm(x, p['norm2']) * (1 + sc2) + sh2
    ff = (jax.nn.silu(h2 @ p['w_ff_gate']) * (h2 @ p['w_ff_up'])) @ p['w_ff_down']
    x = x + g2 * ff
    return x, s_fin, k_d, v_d


def setup_inputs(seed: int = 0) -> dict:
    key = jax.random.key(seed)
    ks = jax.random.split(key, 25)
    def nrm(k, shape, s=1.0):
        return jax.random.normal(k, shape, jnp.float32) * s
    D = D_MODEL
    return {
        'x_prompt': nrm(ks[0], (BATCH, SEQ, D)),
        'x_sample': nrm(ks[1], (DEC_BATCH, DEC_SEQ, D)),
        'c': nrm(ks[2], (DEC_BATCH, D)),
        'cache_diff_k': nrm(ks[3], (DEC_BATCH, DEPTH, PAST_LEN, DIFF_HEADS, 2, DIFF_HEAD_DIM)),
        'cache_diff_v': nrm(ks[4], (DEC_BATCH, DEPTH, PAST_LEN, DIFF_HEADS, DIFF_V_HEAD)),
        'state_gla': nrm(ks[5], (DEC_BATCH, DEPTH, 2, GLA_HEADS, GLA_DK, GLA_DV), 0.5),
        'c_ctx': nrm(ks[6], (D,)),
        'w_mod': nrm(ks[7], (DEPTH, D, 6 * D), 0.5 * D ** -0.5),
        'b_mod': nrm(ks[8], (DEPTH, 6 * D), 0.02),
        'norm1': 1.0 + nrm(ks[9], (DEPTH, D), 0.02),
        'norm2': 1.0 + nrm(ks[10], (DEPTH, D), 0.02),
        'w_in': nrm(ks[11], (DEPTH, D, IN_COLS), D ** -0.5),
        'w_gla_a2': nrm(ks[12], (DEPTH, 2, GATE_RANK, GLA_K), GATE_RANK ** -0.5),
        'b_gla_a': nrm(ks[13], (DEPTH, 2, GLA_K), 0.1),
        'gla_norm': 1.0 + nrm(ks[14], (DEPTH, GLA_DV), 0.02),
        'diff_qk_norm': 1.0 + nrm(ks[15], (DEPTH, 2, DIFF_HEAD_DIM), 0.02),
        'diff_lambda': nrm(ks[16], (DEPTH, 4, DIFF_HEAD_DIM), 0.1),
        'diff_norm': 1.0 + nrm(ks[17], (DEPTH, DIFF_V_HEAD), 0.02),
        'w_fou': nrm(ks[18], (DEPTH, FOU_W, D), FOU_W ** -0.5),
        'w_gla_o': nrm(ks[19], (DEPTH, GLA_V, D), GLA_V ** -0.5),
        'w_diff_o': nrm(ks[20], (DEPTH, DIFF_V, D), DIFF_V ** -0.5),
        'w_out': nrm(ks[21], (DEPTH, D, D), D ** -0.5),
        'w_ff_gate': nrm(ks[22], (DEPTH, D, D_FF), D ** -0.5),
        'w_ff_up': nrm(ks[23], (DEPTH, D, D_FF), D ** -0.5),
        'w_ff_down': nrm(ks[24], (DEPTH, D_FF, D), D_FF ** -0.5),
    }


def reference(x_prompt, x_sample, c, cache_diff_k, cache_diff_v, state_gla, c_ctx,
              w_mod, b_mod, norm1, norm2, w_in, w_gla_a2, b_gla_a, gla_norm,
              diff_qk_norm, diff_lambda, diff_norm, w_fou, w_gla_o, w_diff_o, w_out,
              w_ff_gate, w_ff_up, w_ff_down):
    def layer_params(l):
        return {
            'w_mod': w_mod[l], 'b_mod': b_mod[l], 'norm1': norm1[l], 'norm2': norm2[l],
            'w_in': w_in[l], 'w_gla_a2': w_gla_a2[l], 'b_gla_a': b_gla_a[l],
            'gla_norm': gla_norm[l], 'diff_qk_norm': diff_qk_norm[l],
            'diff_lambda': diff_lambda[l], 'diff_norm': diff_norm[l], 'w_fou': w_fou[l],
            'w_gla_o': w_gla_o[l], 'w_diff_o': w_diff_o[l], 'w_out': w_out[l],
            'w_ff_gate': w_ff_gate[l], 'w_ff_up': w_ff_up[l], 'w_ff_down': w_ff_down[l],
        }

    cond_ctx = c_ctx[None, :]
    y_prompt = x_prompt
    k_list, v_list, s_list = [], [], []
    for l in range(DEPTH):
        y_prompt, s_l, k_l, v_l = trunk_layer(y_prompt, cond_ctx, l, layer_params(l), None, None)
        s_list.append(s_l)
        k_list.append(k_l)
        v_list.append(v_l)

    rope = axial_rope(x_sample.shape[1])
    y_sample = x_sample
    for l in range(DEPTH):
        ctx = (state_gla[:, l], cache_diff_k[:, l], cache_diff_v[:, l])
        y_sample, _, _, _ = trunk_layer(y_sample, c, l, layer_params(l), rope, ctx)

    new_cache_diff_k = jnp.stack(k_list, axis=1)
    new_cache_diff_v = jnp.stack(v_list, axis=1)
    new_state_gla = jnp.stack(s_list, axis=1)
    return (y_prompt, y_sample, new_cache_diff_k, new_cache_diff_v, new_state_gla)
```

```python
import functools
import math

import numpy as np
import jax
import jax.numpy as jnp
from jax import lax
from jax.experimental import pallas as pl
from jax.experimental.pallas import tpu as pltpu

F32 = jnp.float32
BF16 = jnp.bfloat16

D_MODEL = 1024
DEPTH = 4
GRID_W = 64
FOU_GROUPS = 4
FOU_GROUP_W = 128
FOU_W = 512
GLA_HEADS = 4
GLA_DK = 64
GLA_DV = 128
GLA_K = 256
GLA_V = 512
GATE_RANK = 16
GATE_TEMP = 16.0
DIFF_HEADS = 4
DIFF_HEAD_DIM = 64
DIFF_V_HEAD = 128
DIFF_QK = 512
DIFF_V = 512
ROPE_AXIS_DIM = 32
ROPE_FREQS = 16
ROPE_BASE = 10000.0
N_BRANCH = 3
D_FF = 2816
EPS = 1e-6

COL_GATES = 0
COL_UF = 3072
COL_VG = 3584
COL_RG = 4096
COL_QD = 4608
COL_KD = 5120
COL_VD = 5632
COL_QG = 6144
COL_KG = 6400
PROJ_W = 6656
A_PAD = 128

GLA_C = 64
GLA_LEVELS = (32, 16, 8, 4, 2, 1)
GLA_MROWS = 64 * (1 + len(GLA_LEVELS)) + 8

VMEM_LIMIT = 48 * 1024 * 1024


def _params(sem, vmem=VMEM_LIMIT):
    return pltpu.CompilerParams(dimension_semantics=sem, vmem_limit_bytes=vmem)


def _dot(a, b):
    return jnp.dot(a, b, preferred_element_type=F32)


def _dot_nt(a, b):
    return lax.dot_general(a, b, (((1,), (1,)), ((), ())), preferred_element_type=F32)


def _dot_tn(a, b):
    return lax.dot_general(a, b, (((0,), (0,)), ((), ())), preferred_element_type=F32)


def _split3(x):
    x1 = x.astype(BF16)
    r1 = x - x1.astype(F32)
    x2 = r1.astype(BF16)
    x3 = (r1 - x2.astype(F32)).astype(BF16)
    return x1, x2, x3


def _mod_kernel(c_ref, w_ref, b_ref, o_ref):
    c = c_ref[...]
    s = c * jax.nn.sigmoid(c)
    s1, s2, s3 = _split3(s)
    w = w_ref[0]
    w1, w2, w3 = _split3(w)
    acc = _dot(s1, w1) + (_dot(s1, w2) + _dot(s2, w1)) + (_dot(s2, w2) + _dot(s1, w3) + _dot(s3, w1))
    o_ref[0] = acc + b_ref[0]


def _modulation(cond, w_mod, b_mod):
    r = cond.shape[0]
    tn = 768
    n = 6 * D_MODEL
    return pl.pallas_call(
        _mod_kernel,
        grid=(DEPTH, n // tn),
        in_specs=[
            pl.BlockSpec((r, D_MODEL), lambda l, j: (0, 0)),
            pl.BlockSpec((1, D_MODEL, tn), lambda l, j: (l, 0, j)),
            pl.BlockSpec((1, 1, tn), lambda l, j: (l, 0, j)),
        ],
        out_specs=pl.BlockSpec((1, r, tn), lambda l, j: (l, 0, j)),
        out_shape=jax.ShapeDtypeStruct((DEPTH, r, n), F32),
        compiler_params=_params(("arbitrary", "arbitrary")),
        name="modulation",
    )(cond, w_mod, b_mod.reshape(DEPTH, 1, n))


def _inproj_kernel(x_ref, sc_ref, sh_ref, gain_ref, w_ref, wa_ref, w2_ref, b2_ref,
                   proj_ref, g_ref, h_scr):
    j = pl.program_id(1)

    @pl.when(j == 0)
    def _():
        x = x_ref[...]
        ms = jnp.mean(x * x, axis=-1, keepdims=True)
        h = x * lax.rsqrt(ms + EPS) * gain_ref[...]
        h = h * (1.0 + sc_ref[0]) + sh_ref[0]
        hb = h.astype(BF16)
        h_scr[...] = hb
        a = _dot(hb, wa_ref[...])
        a1, a2, a3 = _split3(a)
        w2 = w2_ref[...]
        v1, v2, v3 = _split3(w2)
        z = (_dot(a1, v1) + (_dot(a1, v2) + _dot(a2, v1))
             + (_dot(a2, v2) + _dot(a1, v3) + _dot(a3, v1))) + b2_ref[...]
        logsig = jnp.minimum(z, 0.0) - jnp.log1p(jnp.exp(-jnp.abs(z)))
        g_ref[...] = logsig * (1.0 / GATE_TEMP)

    proj_ref[...] = _dot(h_scr[...], w_ref[...]).astype(BF16)


def _in_projection(x, sc, sh, gain, w_main, w_a, w2, b2, tokens_per_cond, tm):
    m = x.shape[0]
    tn = 512
    per = tokens_per_cond // tm
    ncond = sc.shape[0]
    cidx = (lambda i, j: (i // per, 0, 0)) if ncond > 1 else (lambda i, j: (0, 0, 0))
    return pl.pallas_call(
        _inproj_kernel,
        grid=(m // tm, PROJ_W // tn),
        in_specs=[
            pl.BlockSpec((tm, D_MODEL), lambda i, j: (i, 0)),
            pl.BlockSpec((1, 1, D_MODEL), cidx),
            pl.BlockSpec((1, 1, D_MODEL), cidx),
            pl.BlockSpec((1, D_MODEL), lambda i, j: (0, 0)),
            pl.BlockSpec((D_MODEL, tn), lambda i, j: (0, j)),
            pl.BlockSpec((D_MODEL, A_PAD), lambda i, j: (0, 0)),
            pl.BlockSpec((A_PAD, 2 * GLA_K), lambda i, j: (0, 0)),
            pl.BlockSpec((1, 2 * GLA_K), lambda i, j: (0, 0)),
        ],
        out_specs=[
            pl.BlockSpec((tm, tn), lambda i, j: (i, j)),
            pl.BlockSpec((tm, 2 * GLA_K), lambda i, j: (i, 0)),
        ],
        out_shape=[
            jax.ShapeDtypeStruct((m, PROJ_W), BF16),
            jax.ShapeDtypeStruct((m, 2 * GLA_K), F32),
        ],
        scratch_shapes=[pltpu.VMEM((tm, D_MODEL), BF16)],
        compiler_params=_params(("arbitrary", "arbitrary")),
        name="in_projection",
    )(x, sc, sh, gain, w_main, w_a, w2, b2)


def _qkprep_kernel(use_rope, q_ref, k_ref, gq_ref, gk_ref, gqp_ref, gkp_ref, grp_ref, rot_ref,
                   cos_ref, sin_ref, qn_ref, kn_ref, kf_ref):
    def prep(xb, g_ref, gp_ref, scale):
        x = xb.astype(F32)
        ms = _dot((x * x).astype(BF16), grp_ref[...])
        r = lax.rsqrt(ms + EPS)
        y = x * r * g_ref[...]
        if use_rope:
            yr = _dot(xb, rot_ref[...]) * r * gp_ref[...]
            y = y * cos_ref[...] + yr * sin_ref[...]
        return y * scale

    qn_ref[...] = prep(q_ref[...], gq_ref, gqp_ref, DIFF_HEAD_DIM ** -0.5).astype(BF16)
    kn = prep(k_ref[...], gk_ref, gkp_ref, 1.0)
    kn_ref[0] = kn.astype(BF16)
    if kf_ref is not None:
        kf_ref[...] = kn


def _qk_prepare(proj, gains, grp, rot, cos, sin, nb, t, t_keys, use_rope, want_f32, tm):
    m = proj.shape[0]
    per = t // tm
    gq, gk, gqp, gkp = gains
    vec = pl.BlockSpec((1, DIFF_QK), lambda i: (0, 0))
    mat = pl.BlockSpec((DIFF_QK, DIFF_QK), lambda i: (0, 0))
    tab = pl.BlockSpec((tm, DIFF_QK), lambda i: (i % per, 0))
    out_specs = [
        pl.BlockSpec((tm, DIFF_QK), lambda i: (i, 0)),
        pl.BlockSpec((1, tm, DIFF_QK), lambda i: (i // per, i % per, 0)),
    ]
    out_shape = [
        jax.ShapeDtypeStruct((m, DIFF_QK), BF16),
        jax.ShapeDtypeStruct((nb, t_keys, DIFF_QK), BF16),
    ]
    if want_f32:
        out_specs.append(pl.BlockSpec((tm, DIFF_QK), lambda i: (i, 0)))
        out_shape.append(jax.ShapeDtypeStruct((m, DIFF_QK), F32))
        kern = functools.partial(_qkprep_kernel, use_rope)
    else:
        kern = lambda *refs: _qkprep_kernel(use_rope, *refs, None)
    return pl.pallas_call(
        kern,
        grid=(m // tm,),
        in_specs=[
            pl.BlockSpec((tm, DIFF_QK), lambda i: (i, COL_QD // DIFF_QK)),
            pl.BlockSpec((tm, DIFF_QK), lambda i: (i, COL_KD // DIFF_QK)),
            vec, vec, vec, vec, mat, mat, tab, tab,
        ],
        out_specs=out_specs,
        out_shape=out_shape,
        compiler_params=_params(("arbitrary",)),
        name="qk_prepare",
    )(proj, proj, gq, gk, gqp, gkp, grp, rot, cos, sin)


def _diffattn_kernel(n_seg, lam_init, *refs):
    q_ref = refs[0]
    kv_refs = refs[1:1 + 2 * n_seg]
    lam_ref, gain_ref, o_ref = refs[1 + 2 * n_seg:4 + 2 * n_seg]
    kz_refs = refs[4 + 2 * n_seg:]
    i = pl.program_id(2)

    @pl.when(i == 0)
    def _():
        for s in range(n_seg):
            k = kv_refs[2 * s][0]
            lane = lax.broadcasted_iota(jnp.int32, k.shape, 1)
            zero = jnp.zeros_like(k)
            kz_refs[2 * s][...] = jnp.where(lane < DIFF_HEAD_DIM, k, zero)
            kz_refs[2 * s + 1][...] = jnp.where(lane >= DIFF_HEAD_DIM, k, zero)

    q = q_ref[0]

    def attend(which):
        scores = [_dot_nt(q, kz_refs[2 * s + which][...]) for s in range(n_seg)]
        mx = scores[0].max(axis=-1, keepdims=True)
        for sc in scores[1:]:
            mx = jnp.maximum(mx, sc.max(axis=-1, keepdims=True))
        den = None
        num = None
        for s, sc in enumerate(scores):
            p = jnp.exp(sc - mx)
            d = p.sum(axis=-1, keepdims=True)
            o = _dot(p.astype(BF16), kv_refs[2 * s + 1][0])
            den = d if den is None else den + d
            num = o if num is None else num + o
        return num / den

    lp = lam_ref[...]
    lam = (jnp.exp(jnp.sum(lp[0:1] * lp[1:2], axis=-1, keepdims=True))
           - jnp.exp(jnp.sum(lp[2:3] * lp[3:4], axis=-1, keepdims=True)) + lam_init)
    o = attend(0) - lam * attend(1)
    ms = jnp.mean(o * o, axis=-1, keepdims=True)
    o = o * lax.rsqrt(ms + EPS) * gain_ref[...] * (1.0 - lam_init)
    o_ref[0] = o.astype(BF16)


def _diff_attention(qn, segments, lam_p, gain, lam_init, nb, tq_total, tq):
    n_seg = len(segments)
    hd = 2 * DIFF_HEAD_DIM
    in_specs = [pl.BlockSpec((1, tq, hd), lambda b, h, i: (b, i, h))]
    args = [qn]
    scratch = []
    for (ka, kc, va, vc, tk) in segments:
        in_specs.append(pl.BlockSpec((1, tk, hd), lambda b, h, i, kc=kc: (b, 0, kc + h)))
        in_specs.append(pl.BlockSpec((1, tk, DIFF_V_HEAD), lambda b, h, i, vc=vc: (b, 0, vc + h)))
        args += [ka, va]
        scratch += [pltpu.VMEM((tk, hd), BF16), pltpu.VMEM((tk, hd), BF16)]
    in_specs.append(pl.BlockSpec((4, DIFF_HEAD_DIM), lambda b, h, i: (0, 0)))
    in_specs.append(pl.BlockSpec((1, DIFF_V_HEAD), lambda b, h, i: (0, 0)))
    args += [lam_p, gain]
    return pl.pallas_call(
        functools.partial(_diffattn_kernel, n_seg, lam_init),
        grid=(nb, DIFF_HEADS, tq_total // tq),
        in_specs=in_specs,
        out_specs=pl.BlockSpec((1, tq, DIFF_V_HEAD), lambda b, h, i: (b, i, h)),
        out_shape=jax.ShapeDtypeStruct((nb, tq_total, DIFF_V), BF16),
        scratch_shapes=scratch,
        compiler_params=_params(("arbitrary", "arbitrary", "arbitrary")),
        name="diff_attention",
    )(*args)


def _gla_tables():
    c = GLA_C
    nl = len(GLA_LEVELS)
    mst = np.zeros((2, GLA_MROWS, c), np.float32)
    xm = np.zeros((2, nl, c, 1), np.float32)
    pm = np.zeros((2, nl + 1, c, c), np.float32)
    idx = np.arange(c)
    for d in range(2):
        ip = idx if d == 0 else c - 1 - idx
        mst[d, 0:c] = (ip[None, :] <= ip[:, None])
        mst[d, c * (nl + 1)] = 1.0
        pm[d, 0] = np.eye(c)
        for li, s in enumerate(GLA_LEVELS):
            ref = (ip // (2 * s)) * (2 * s) + s - 1
            mst[d, c * (li + 1):c * (li + 2)] = (ip[None, :] <= ref[:, None])
            odd = (ip // s) % 2 == 1
            xm[d, li, :, 0] = odd
            same = (ip[:, None] // (2 * s)) == (ip[None, :] // (2 * s))
            pm[d, li + 1] = odd[:, None] & (~odd)[None, :] & same
    xm = np.broadcast_to(xm, (2, nl, c, GLA_K)).copy()
    pm = np.tile(pm, (1, 1, GLA_HEADS, 1))
    rows = np.arange(GLA_HEADS * c)[:, None] // c
    hm = (rows == (np.arange(GLA_K)[None, :] // GLA_DK)).astype(np.float32)
    bd = ((np.arange(GLA_V)[:, None] // GLA_DV) == (np.arange(GLA_K)[None, :] // GLA_DK)).astype(np.float32)
    return mst, xm, pm, hm, bd


def _gla_kernel(q_ref, k_ref, v_ref, g_ref, mst_ref, xm_ref, pm_ref, hm_ref, bd_ref, s0_ref,
                o_ref, sfin_ref, s_scr):
    c_idx = pl.program_id(2)

    @pl.when(c_idx == 0)
    def _():
        s_scr[...] = s0_ref[0, 0]

    c = GLA_C
    nl = len(GLA_LEVELS)
    g = g_ref[0]
    g1, g2, g3 = _split3(g)
    m = mst_ref[0]
    r_all = _dot(m, g1) + _dot(m, g2) + _dot(m, g3)
    cum = r_all[0:c]
    tot = r_all[c * (nl + 1):c * (nl + 1) + 1]
    q = q_ref[0].astype(F32) * (GLA_DK ** -0.5)
    k = k_ref[0].astype(F32)
    v = v_ref[0]
    hm = hm_ref[...]

    def stack(x):
        return (jnp.concatenate([x] * GLA_HEADS, axis=0) * hm).astype(BF16)

    state = s_scr[...]
    o_inter = _dot_nt((q * jnp.exp(cum)).astype(BF16), state.astype(BF16))
    k_dec = (k * jnp.exp(tot - cum)).astype(BF16)
    s_new = state * jnp.exp(tot) + _dot_tn(v, k_dec) * bd_ref[...]
    s_scr[...] = s_new
    sfin_ref[0, 0] = s_new

    att = pm_ref[0, 0] * _dot_nt(stack(q), k.astype(BF16))
    for li in range(nl):
        ref = r_all[c * (li + 1):c * (li + 2)]
        e = jnp.exp(-jnp.abs(cum - ref))
        xmask = xm_ref[0, li]
        xs = q * e * xmask
        ys = k * e * (1.0 - xmask)
        att = att + pm_ref[0, li + 1] * _dot_nt(stack(xs), ys.astype(BF16))
    outs = []
    for h in range(GLA_HEADS):
        a_h = att[h * c:(h + 1) * c].astype(BF16)
        outs.append(_dot(a_h, v[:, h * GLA_DV:(h + 1) * GLA_DV]))
    o_ref[0, 0] = o_inter + jnp.concatenate(outs, axis=1)


def _gla(proj3, g3, s0, tables, nb, t):
    mst, xm, pm, hm, bd = tables
    nc = t // GLA_C
    nl = len(GLA_LEVELS)

    def chunk(d, c):
        return jnp.where(d == 0, c, nc - 1 - c)

    return pl.pallas_call(
        _gla_kernel,
        grid=(nb, 2, nc),
        in_specs=[
            pl.BlockSpec((1, GLA_C, GLA_K), lambda b, d, c: (b, chunk(d, c), COL_QG // GLA_K)),
            pl.BlockSpec((1, GLA_C, GLA_K), lambda b, d, c: (b, chunk(d, c), COL_KG // GLA_K)),
            pl.BlockSpec((1, GLA_C, GLA_V), lambda b, d, c: (b, chunk(d, c), COL_VG // GLA_V)),
            pl.BlockSpec((1, GLA_C, GLA_K), lambda b, d, c: (b, chunk(d, c), d)),
            pl.BlockSpec((1, GLA_MROWS, GLA_C), lambda b, d, c: (d, 0, 0)),
            pl.BlockSpec((1, nl, GLA_C, GLA_K), lambda b, d, c: (d, 0, 0, 0)),
            pl.BlockSpec((1, nl + 1, GLA_HEADS * GLA_C, GLA_C), lambda b, d, c: (d, 0, 0, 0)),
            pl.BlockSpec((GLA_HEADS * GLA_C, GLA_K), lambda b, d, c: (0, 0)),
            pl.BlockSpec((GLA_V, GLA_K), lambda b, d, c: (0, 0)),
            pl.BlockSpec((1, 1, GLA_V, GLA_K), lambda b, d, c: (b, d, 0, 0)),
        ],
        out_specs=[
            pl.BlockSpec((1, 1, GLA_C, GLA_V), lambda b, d, c: (d, b, chunk(d, c), 0)),
            pl.BlockSpec((1, 1, GLA_V, GLA_K), lambda b, d, c: (b, d, 0, 0)),
        ],
        out_shape=[
            jax.ShapeDtypeStruct((2, nb, t, GLA_V), F32),
            jax.ShapeDtypeStruct((nb, 2, GLA_V, GLA_K), F32),
        ],
        scratch_shapes=[pltpu.VMEM((GLA_V, GLA_K), F32)],
        compiler_params=_params(("arbitrary", "arbitrary", "arbitrary")),
        name="gla",
    )(proj3, proj3, proj3, g3, mst, xm, pm, hm, bd, s0)


def _fourier_kernel(t, u_ref, bcs_ref, ct_ref, o_ref, y_scr):
    i = pl.program_id(1)

    @pl.when(i == 0)
    def _():
        rows = min(t, 512)
        for r in range(t // rows):
            y = _dot(u_ref[0, r * rows:(r + 1) * rows, :], bcs_ref[...])
            y_scr[r * rows:(r + 1) * rows, :] = y[:, :FOU_W].astype(BF16)
            y_scr[t + r * rows:t + (r + 1) * rows, :] = y[:, FOU_W:].astype(BF16)

    o_ref[0] = _dot(ct_ref[...], y_scr[...]).astype(BF16)


def _fourier(proj3, bcs, ct, nb, t, tm):
    return pl.pallas_call(
        functools.partial(_fourier_kernel, t),
        grid=(nb, t // tm),
        in_specs=[
            pl.BlockSpec((1, t, FOU_W), lambda b, i: (b, 0, COL_UF // FOU_W)),
            pl.BlockSpec((FOU_W, 2 * FOU_W), lambda b, i: (0, 0)),
            pl.BlockSpec((tm, 2 * t), lambda b, i: (i, 0)),
        ],
        out_specs=pl.BlockSpec((1, tm, FOU_W), lambda b, i: (b, i, 0)),
        out_shape=jax.ShapeDtypeStruct((nb, t, FOU_W), BF16),
        scratch_shapes=[pltpu.VMEM((2 * t, FOU_W), BF16)],
        compiler_params=_params(("arbitrary", "arbitrary")),
        name="fourier_mix",
    )(proj3, bcs, ct)


def _dft_tables(t):
    n = jnp.arange(t, dtype=jnp.int32)
    ang = ((n[:, None] * n[None, :]) % t).astype(F32) * (2.0 * math.pi / t)
    scale = t ** -0.5
    ct = jnp.concatenate([jnp.cos(ang) * scale, -jnp.sin(ang) * scale], axis=1).astype(BF16)
    return ct


def _channel_dft_table():
    n = np.arange(FOU_GROUP_W)
    ang = 2.0 * np.pi * ((n[:, None] * n[None, :]) % FOU_GROUP_W) / FOU_GROUP_W
    eye = np.eye(FOU_GROUPS)
    bc = np.kron(eye, np.cos(ang)) * FOU_GROUP_W ** -0.5
    bs = np.kron(eye, np.sin(ang)) * FOU_GROUP_W ** -0.5
    return np.concatenate([bc, bs], axis=1).astype(np.float32)


def _merge_kernel(x_ref, g1_ref, fm_ref, of_ref, ob_ref, r_ref, od_ref, gt_ref,
                  wf_ref, wg_ref, wd_ref, wo_ref, gn_ref, grp_ref, o_ref):
    y_f = _dot(fm_ref[...], wf_ref[...])
    og = of_ref[0] + ob_ref[0]
    ms = _dot((og * og).astype(BF16), grp_ref[...])
    r = r_ref[...].astype(F32)
    og = og * lax.rsqrt(ms + EPS) * gn_ref[...] * (r * jax.nn.sigmoid(r))
    y_g = _dot(og.astype(BF16), wg_ref[...])
    y_d = _dot(od_ref[...], wd_ref[...])
    gates = jax.nn.sigmoid(gt_ref[...].astype(F32))
    merged = (gates[:, 0:D_MODEL] * y_f + gates[:, D_MODEL:2 * D_MODEL] * y_g
              + gates[:, 2 * D_MODEL:3 * D_MODEL] * y_d)
    o_ref[...] = x_ref[...] + g1_ref[0] * _dot(merged.astype(BF16), wo_ref[...])


def _merge(x, g1, proj, fm, o_gla, od, w_fou, w_gla_o, w_diff_o, w_out, gn, grp, tokens_per_cond, tm):
    m = x.shape[0]
    per = tokens_per_cond // tm
    ncond = g1.shape[0]
    cidx = (lambda i: (i // per, 0, 0)) if ncond > 1 else (lambda i: (0, 0, 0))
    half = pl.BlockSpec((FOU_W, D_MODEL), lambda i: (0, 0))
    return pl.pallas_call(
        _merge_kernel,
        grid=(m // tm,),
        in_specs=[
            pl.BlockSpec((tm, D_MODEL), lambda i: (i, 0)),
            pl.BlockSpec((1, 1, D_MODEL), cidx),
            pl.BlockSpec((tm, FOU_W), lambda i: (i, 0)),
            pl.BlockSpec((1, tm, GLA_V), lambda i: (0, i, 0)),
            pl.BlockSpec((1, tm, GLA_V), lambda i: (1, i, 0)),
            pl.BlockSpec((tm, GLA_V), lambda i: (i, COL_RG // GLA_V)),
            pl.BlockSpec((tm, DIFF_V), lambda i: (i, 0)),
            pl.BlockSpec((tm, N_BRANCH * D_MODEL), lambda i: (i, COL_GATES)),
            half, half, half,
            pl.BlockSpec((D_MODEL, D_MODEL), lambda i: (0, 0)),
            pl.BlockSpec((1, GLA_V), lambda i: (0, 0)),
            pl.BlockSpec((GLA_V, GLA_V), lambda i: (0, 0)),
        ],
        out_specs=pl.BlockSpec((tm, D_MODEL), lambda i: (i, 0)),
        out_shape=jax.ShapeDtypeStruct((m, D_MODEL), F32),
        compiler_params=_params(("arbitrary",)),
        name="merge",
    )(x, g1, fm, o_gla, o_gla, proj, od, proj, w_fou, w_gla_o, w_diff_o, w_out, gn, grp)


def _ffn_kernel(x_ref, sc_ref, sh_ref, g2_ref, gain_ref, wg_ref, wu_ref, wd_ref, o_ref, h_scr, acc_scr):
    f = pl.program_id(1)

    @pl.when(f == 0)
    def _():
        x = x_ref[...]
        ms = jnp.mean(x * x, axis=-1, keepdims=True)
        h = x * lax.rsqrt(ms + EPS) * gain_ref[...]
        h_scr[...] = (h * (1.0 + sc_ref[0]) + sh_ref[0]).astype(BF16)
        acc_scr[...] = jnp.zeros_like(acc_scr)

    h = h_scr[...]
    gate = _dot(h, wg_ref[...])
    up = _dot(h, wu_ref[...])
    a = (gate * jax.nn.sigmoid(gate) * up).astype(BF16)
    acc_scr[...] += _dot(a, wd_ref[...])

    @pl.when(f == pl.num_programs(1) - 1)
    def _():
        o_ref[...] = x_ref[...] + g2_ref[0] * acc_scr[...]


def _ffn(x, sc, sh, g2, gain, w_gate, w_up, w_down, tokens_per_cond, tm):
    m = x.shape[0]
    tf = 256
    per = tokens_per_cond // tm
    ncond = sc.shape[0]
    cidx = (lambda i, f: (i // per, 0, 0)) if ncond > 1 else (lambda i, f: (0, 0, 0))
    return pl.pallas_call(
        _ffn_kernel,
        grid=(m // tm, D_FF // tf),
        in_specs=[
            pl.BlockSpec((tm, D_MODEL), lambda i, f: (i, 0)),
            pl.BlockSpec((1, 1, D_MODEL), cidx),
            pl.BlockSpec((1, 1, D_MODEL), cidx),
            pl.BlockSpec((1, 1, D_MODEL), cidx),
            pl.BlockSpec((1, D_MODEL), lambda i, f: (0, 0)),
            pl.BlockSpec((D_MODEL, tf), lambda i, f: (0, f)),
            pl.BlockSpec((D_MODEL, tf), lambda i, f: (0, f)),
            pl.BlockSpec((tf, D_MODEL), lambda i, f: (f, 0)),
        ],
        out_specs=pl.BlockSpec((tm, D_MODEL), lambda i, f: (i, 0)),
        out_shape=jax.ShapeDtypeStruct((m, D_MODEL), F32),
        scratch_shapes=[pltpu.VMEM((tm, D_MODEL), BF16), pltpu.VMEM((tm, D_MODEL), F32)],
        compiler_params=_params(("arbitrary", "arbitrary")),
        name="ffn",
    )(x, sc, sh, g2, gain, w_gate, w_up, w_down)


def _group_mean_matrix(width, group):
    idx = np.arange(width) // group
    return (idx[:, None] == idx[None, :]).astype(np.float32) / group


def _rotate_half_matrix():
    r = np.zeros((DIFF_HEAD_DIM, DIFF_HEAD_DIM), np.float32)
    for axis in range(2):
        base = axis * ROPE_AXIS_DIM
        for f in range(ROPE_FREQS):
            r[base + ROPE_FREQS + f, base + f] = -1.0
            r[base + f, base + ROPE_FREQS + f] = 1.0
    return np.kron(np.eye(DIFF_QK // DIFF_HEAD_DIM), r).astype(np.float32)


def _rotate_half_perm():
    p = np.zeros((DIFF_HEAD_DIM,), np.int32)
    for axis in range(2):
        base = axis * ROPE_AXIS_DIM
        for f in range(ROPE_FREQS):
            p[base + f] = base + ROPE_FREQS + f
            p[base + ROPE_FREQS + f] = base + f
    return p


def _rope_tables(n_tokens):
    rows = n_tokens // GRID_W
    row = jnp.repeat(jnp.arange(rows), GRID_W).astype(F32)
    col = jnp.tile(jnp.arange(GRID_W), rows).astype(F32)
    inv = ROPE_BASE ** (-jnp.arange(ROPE_FREQS, dtype=F32) * 2.0 / ROPE_AXIS_DIM)
    ang_r = row[:, None] * inv
    ang_c = col[:, None] * inv
    ang = jnp.concatenate([ang_r, ang_r, ang_c, ang_c], axis=-1)
    reps = DIFF_QK // DIFF_HEAD_DIM
    return jnp.tile(jnp.cos(ang), (1, reps)), jnp.tile(jnp.sin(ang), (1, reps))


def _block_diag_state(s):
    eye = jnp.eye(GLA_HEADS, dtype=s.dtype)
    full = jnp.einsum('bdhkv,hg->bdhvgk', s, eye)
    return full.reshape(s.shape[0], 2, GLA_V, GLA_K)


def _unblock_state(sb):
    nb = sb.shape[0]
    s = sb.reshape(nb, 2, GLA_HEADS, GLA_DV, GLA_HEADS, GLA_DK)
    diag = jnp.stack([s[:, :, h, :, h, :] for h in range(GLA_HEADS)], axis=2)
    return jnp.swapaxes(diag, -1, -2)


def _run_path(x, mods, layers, consts, nb, t, ctx):
    m = nb * t
    tm = 512 if t >= 512 else t
    x = x.reshape(m, D_MODEL)
    is_latent = ctx is not None
    cos, sin = consts['rope'] if is_latent else (consts['ones'], consts['ones'])
    ct = consts['ct_lat'] if is_latent else consts['ct_ctx']
    k_out, v_out, s_out = [], [], []
    for l, p in enumerate(layers):
        sh1, sc1, g1, sh2, sc2, g2 = [mods[l][:, None, j * D_MODEL:(j + 1) * D_MODEL] for j in range(6)]
        proj, g = _in_projection(x, sc1, sh1, p['norm1'], p['w_main'], p['w_a'], p['w2'], p['b2'], t, tm)
        proj3 = proj.reshape(nb, t, PROJ_W)

        t_keys = t + (ctx[1].shape[2] if is_latent else 0)
        prep = _qk_prepare(proj, p['qk_gains'], consts['grp64'], consts['rot'], cos, sin,
                           nb, t, t, is_latent, not is_latent, tm)
        qn, kn = prep[0], prep[1]
        lam_init = 0.8 - 0.6 * math.exp(-0.3 * l)
        segments = [(kn, 0, proj3, COL_VD // DIFF_V_HEAD, t)]
        if is_latent:
            ck = ctx[1][:, l].reshape(nb, -1, DIFF_QK).astype(BF16)
            cv = ctx[2][:, l].reshape(nb, -1, DIFF_V).astype(BF16)
            segments.append((ck, 0, cv, 0, ck.shape[1]))
        od = _diff_attention(qn.reshape(nb, t, DIFF_QK), segments, p['diff_lambda'], p['diff_norm'],
                             lam_init, nb, t, min(t, 256))

        if is_latent:
            s0 = _block_diag_state(ctx[0][:, l])
        else:
            s0 = jnp.zeros((nb, 2, GLA_V, GLA_K), F32)
        o_gla, s_fin = _gla(proj3, g.reshape(nb, t, 2 * GLA_K), s0, consts['gla'], nb, t)

        fm = _fourier(proj3, consts['bcs'], ct, nb, t, min(t, 256))

        x = _merge(x, g1, proj, fm.reshape(m, FOU_W), o_gla.reshape(2, m, GLA_V), od.reshape(m, DIFF_V),
                   p['w_fou'], p['w_gla_o'], p['w_diff_o'], p['w_out'], p['gla_norm'], consts['grp128'],
                   t, min(tm, 256))
        x = _ffn(x, sc2, sh2, g2, p['norm2'], p['w_ff_gate'], p['w_ff_up'], p['w_ff_down'], t, tm)

        if not is_latent:
            k_out.append(prep[2].reshape(nb, t, DIFF_HEADS, 2, DIFF_HEAD_DIM))
            v_out.append(proj3[:, :, COL_VD:COL_VD + DIFF_V].astype(F32).reshape(nb, t, DIFF_HEADS, DIFF_V_HEAD))
            s_out.append(_unblock_state(s_fin))
    return x.reshape(nb, t, D_MODEL), k_out, v_out, s_out


def kernel(x_prompt, x_sample, c, cache_diff_k, cache_diff_v, state_gla, c_ctx, w_mod, b_mod, norm1, norm2, w_in, w_gla_a2, b_gla_a, gla_norm, diff_qk_norm, diff_lambda, diff_norm, w_fou, w_gla_o, w_diff_o, w_out, w_ff_gate, w_ff_up, w_ff_down):
    nb_ctx, t_ctx, _ = x_prompt.shape
    nb_lat, t_lat, _ = x_sample.shape

    n_cond = 1 + nb_lat
    r_pad = -(-n_cond // 8) * 8
    cond = jnp.zeros((r_pad, D_MODEL), F32).at[0].set(c_ctx).at[1:n_cond].set(c)
    mods = _modulation(cond, w_mod, b_mod)
    mods_ctx = [mods[l, 0:1] for l in range(DEPTH)]
    mods_lat = [mods[l, 1:n_cond] for l in range(DEPTH)]

    a0 = 2048
    a1 = a0 + 2 * GATE_RANK
    pieces = {
        'u_f': (0, 512), 'q_g': (512, 768), 'k_g': (768, 1024), 'v_g': (1024, 1536), 'r_g': (1536, 2048),
        'q_d': (a1, a1 + 512), 'k_d': (a1 + 512, a1 + 1024), 'v_d': (a1 + 1024, a1 + 1536),
        'gates': (a1 + 1536, a1 + 1536 + 3072),
    }
    order = ['gates', 'u_f', 'v_g', 'r_g', 'q_d', 'k_d', 'v_d', 'q_g', 'k_g']
    w_main = jnp.concatenate([w_in[:, :, pieces[n][0]:pieces[n][1]] for n in order], axis=-1).astype(BF16)
    w_a = jnp.pad(w_in[:, :, a0:a1], ((0, 0), (0, 0), (0, A_PAD - 2 * GATE_RANK))).astype(BF16)
    w2 = jnp.zeros((DEPTH, A_PAD, 2 * GLA_K), F32)
    w2 = w2.at[:, 0:GATE_RANK, 0:GLA_K].set(w_gla_a2[:, 0])
    w2 = w2.at[:, GATE_RANK:2 * GATE_RANK, GLA_K:].set(w_gla_a2[:, 1])
    b2 = b_gla_a.reshape(DEPTH, 1, 2 * GLA_K)

    perm = _rotate_half_perm()
    reps = DIFF_QK // DIFF_HEAD_DIM
    layers = []
    for l in range(DEPTH):
        gq = diff_qk_norm[l, 0]
        gk = diff_qk_norm[l, 1]
        layers.append({
            'norm1': norm1[l][None, :], 'norm2': norm2[l][None, :],
            'w_main': w_main[l], 'w_a': w_a[l], 'w2': w2[l], 'b2': b2[l],
            'qk_gains': tuple(jnp.tile(v, reps)[None, :] for v in (gq, gk, gq[perm], gk[perm])),
            'diff_lambda': diff_lambda[l], 'diff_norm': diff_norm[l][None, :],
            'gla_norm': jnp.tile(gla_norm[l], GLA_HEADS)[None, :],
            'w_fou': w_fou[l].astype(BF16), 'w_gla_o': w_gla_o[l].astype(BF16),
            'w_diff_o': w_diff_o[l].astype(BF16), 'w_out': w_out[l].astype(BF16),
            'w_ff_gate': w_ff_gate[l].astype(BF16), 'w_ff_up': w_ff_up[l].astype(BF16),
            'w_ff_down': w_ff_down[l].astype(BF16),
        })

    mst, xm, pm, hm, bd = _gla_tables()
    consts = {
        'grp64': jnp.asarray(_group_mean_matrix(DIFF_QK, DIFF_HEAD_DIM), BF16),
        'grp128': jnp.asarray(_group_mean_matrix(GLA_V, GLA_DV), BF16),
        'rot': jnp.asarray(_rotate_half_matrix(), BF16),
        'rope': _rope_tables(t_lat),
        'ones': jnp.ones((t_ctx, DIFF_QK), F32),
        'bcs': jnp.asarray(_channel_dft_table(), BF16),
        'ct_ctx': _dft_tables(t_ctx),
        'ct_lat': _dft_tables(t_lat),
        'gla': (jnp.asarray(mst, BF16), jnp.asarray(xm), jnp.asarray(pm), jnp.asarray(hm), jnp.asarray(bd)),
    }

    y_prompt, k_list, v_list, s_list = _run_path(x_prompt, mods_ctx, layers, consts, nb_ctx, t_ctx, None)
    y_sample, _, _, _ = _run_path(x_sample, mods_lat, layers, consts, nb_lat, t_lat,
                                  (state_gla, cache_diff_k, cache_diff_v))
    return (y_prompt, y_sample, jnp.stack(k_list, axis=1), jnp.stack(v_list, axis=1),
            jnp.stack(s_list, axis=1))
```

```python
import functools
import math

import numpy as np
import jax
import jax.numpy as jnp
from jax import lax
from jax.experimental import pallas as pl
from jax.experimental.pallas import tpu as pltpu

F32 = jnp.float32
BF16 = jnp.bfloat16

D_MODEL = 1024
DEPTH = 4
GRID_W = 64
FOU_GROUPS = 4
FOU_GROUP_W = 128
FOU_W = 512
GLA_HEADS = 4
GLA_DK = 64
GLA_DV = 128
GLA_K = 256
GLA_V = 512
GATE_RANK = 16
GATE_TEMP = 16.0
DIFF_HEADS = 4
DIFF_HEAD_DIM = 64
DIFF_V_HEAD = 128
DIFF_QK = 512
DIFF_V = 512
ROPE_AXIS_DIM = 32
ROPE_FREQS = 16
ROPE_BASE = 10000.0
N_BRANCH = 3
D_FF = 2816
EPS = 1e-6

COL_GATES = 0
COL_UF = 3072
COL_VG = 3584
COL_RG = 4096
COL_QD = 4608
COL_KD = 5120
COL_VD = 5632
COL_QG = 6144
COL_KG = 6400
PROJ_W = 6656
A_PAD = 128

GLA_C = 64
GLA_LEVELS = (32, 16, 8, 4, 2, 1)
GLA_MROWS = 64 * (1 + len(GLA_LEVELS)) + 8
GLA_BB = 4

VMEM_LIMIT = 48 * 1024 * 1024
VMEM_LIMIT_BIG = 56 * 1024 * 1024
LOG2_E = math.log2(math.e)


def _params(sem, vmem=VMEM_LIMIT):
    return pltpu.CompilerParams(dimension_semantics=sem, vmem_limit_bytes=vmem)


def _dot(a, b):
    return jnp.dot(a, b, preferred_element_type=F32)


def _dot_nt(a, b):
    return lax.dot_general(a, b, (((1,), (1,)), ((), ())), preferred_element_type=F32)


def _dot_tn(a, b):
    return lax.dot_general(a, b, (((0,), (0,)), ((), ())), preferred_element_type=F32)


def _split3(x):
    x1 = x.astype(BF16)
    r1 = x - x1.astype(F32)
    x2 = r1.astype(BF16)
    x3 = (r1 - x2.astype(F32)).astype(BF16)
    return x1, x2, x3


def _mod_kernel(c_ref, w_ref, b_ref, o_ref):
    c = c_ref[...]
    s = c * jax.nn.sigmoid(c)
    s1, s2, s3 = _split3(s)
    w = w_ref[0]
    w1, w2, w3 = _split3(w)
    acc = _dot(s1, w1) + (_dot(s1, w2) + _dot(s2, w1)) + (_dot(s2, w2) + _dot(s1, w3) + _dot(s3, w1))
    o_ref[0] = acc + b_ref[0]


def _modulation(cond, w_mod, b_mod):
    r = cond.shape[0]
    tn = 768
    n = 6 * D_MODEL
    return pl.pallas_call(
        _mod_kernel,
        grid=(DEPTH, n // tn),
        in_specs=[
            pl.BlockSpec((r, D_MODEL), lambda l, j: (0, 0)),
            pl.BlockSpec((1, D_MODEL, tn), lambda l, j: (l, 0, j)),
            pl.BlockSpec((1, 1, tn), lambda l, j: (l, 0, j)),
        ],
        out_specs=pl.BlockSpec((1, r, tn), lambda l, j: (l, 0, j)),
        out_shape=jax.ShapeDtypeStruct((DEPTH, r, n), F32),
        compiler_params=_params(("arbitrary", "arbitrary")),
        name="modulation",
    )(cond, w_mod, b_mod.reshape(DEPTH, 1, n))


def _inproj_kernel(tn, x_ref, sc_ref, sh_ref, gain_ref, w_ref, wa_ref, w2_ref, b2_ref,
                   proj_ref, g_ref, h_scr):
    x = x_ref[...]
    ms = jnp.mean(x * x, axis=-1, keepdims=True)
    scale = gain_ref[...] * (1.0 + sc_ref[0])
    hb = (x * lax.rsqrt(ms + EPS) * scale + sh_ref[0]).astype(BF16)
    h_scr[...] = hb
    a = _dot(hb, wa_ref[...])
    a1, a2, a3 = _split3(a)
    v1, v2, v3 = _split3(w2_ref[...])
    z = (_dot(a1, v1) + (_dot(a1, v2) + _dot(a2, v1))
         + (_dot(a2, v2) + _dot(a1, v3) + _dot(a3, v1))) + b2_ref[...]
    logsig = jnp.minimum(z, 0.0) - jnp.log1p(jnp.exp(-jnp.abs(z)))
    g_ref[...] = logsig * (1.0 / GATE_TEMP)
    for j in range(PROJ_W // tn):
        proj_ref[:, j * tn:(j + 1) * tn] = _dot(h_scr[...], w_ref[:, j * tn:(j + 1) * tn]).astype(BF16)


def _in_projection(x, sc, sh, gain, w_main, w_a, w2, b2, tokens_per_cond, tm):
    m = x.shape[0]
    tn = 512
    per = tokens_per_cond // tm
    ncond = sc.shape[0]
    cidx = (lambda i: (i // per, 0, 0)) if ncond > 1 else (lambda i: (0, 0, 0))
    const = lambda i: (0, 0)
    once = pl.Buffered(1)
    return pl.pallas_call(
        functools.partial(_inproj_kernel, tn),
        grid=(m // tm,),
        in_specs=[
            pl.BlockSpec((tm, D_MODEL), lambda i: (i, 0)),
            pl.BlockSpec((1, 1, D_MODEL), cidx),
            pl.BlockSpec((1, 1, D_MODEL), cidx),
            pl.BlockSpec((1, D_MODEL), const),
            pl.BlockSpec((D_MODEL, PROJ_W), const, pipeline_mode=once),
            pl.BlockSpec((D_MODEL, A_PAD), const, pipeline_mode=once),
            pl.BlockSpec((A_PAD, 2 * GLA_K), const, pipeline_mode=once),
            pl.BlockSpec((1, 2 * GLA_K), const),
        ],
        out_specs=[
            pl.BlockSpec((tm, PROJ_W), lambda i: (i, 0)),
            pl.BlockSpec((tm, 2 * GLA_K), lambda i: (i, 0)),
        ],
        out_shape=[
            jax.ShapeDtypeStruct((m, PROJ_W), BF16),
            jax.ShapeDtypeStruct((m, 2 * GLA_K), F32),
        ],
        scratch_shapes=[pltpu.VMEM((tm, D_MODEL), BF16)],
        compiler_params=_params(("arbitrary",), VMEM_LIMIT_BIG),
        name="in_projection",
    )(x, sc, sh, gain, w_main, w_a, w2, b2)


def _qkprep_kernel(use_rope, q_ref, k_ref, gq_ref, gk_ref, gqp_ref, gkp_ref, grp_ref, rot_ref,
                   cos_ref, sin_ref, qn_ref, kn_ref, kf_ref):
    def prep(xb, g_ref, gp_ref, scale):
        x = xb.astype(F32)
        ms = _dot((x * x).astype(BF16), grp_ref[...])
        r = lax.rsqrt(ms + EPS)
        y = x * r * g_ref[...]
        if use_rope:
            yr = _dot(xb, rot_ref[...]) * r * gp_ref[...]
            y = y * cos_ref[...] + yr * sin_ref[...]
        return y * scale

    qn_ref[...] = prep(q_ref[...], gq_ref, gqp_ref, DIFF_HEAD_DIM ** -0.5 * LOG2_E).astype(BF16)
    kn = prep(k_ref[...], gk_ref, gkp_ref, 1.0)
    kn_ref[0] = kn.astype(BF16)
    if kf_ref is not None:
        kf_ref[...] = kn


def _qk_prepare(proj, gains, grp, rot, cos, sin, nb, t, t_keys, use_rope, want_f32, tm):
    m = proj.shape[0]
    per = t // tm
    gq, gk, gqp, gkp = gains
    vec = pl.BlockSpec((1, DIFF_QK), lambda i: (0, 0))
    mat = pl.BlockSpec((DIFF_QK, DIFF_QK), lambda i: (0, 0))
    tab = pl.BlockSpec((tm, DIFF_QK), lambda i: (i % per, 0))
    out_specs = [
        pl.BlockSpec((tm, DIFF_QK), lambda i: (i, 0)),
        pl.BlockSpec((1, tm, DIFF_QK), lambda i: (i // per, i % per, 0)),
    ]
    out_shape = [
        jax.ShapeDtypeStruct((m, DIFF_QK), BF16),
        jax.ShapeDtypeStruct((nb, t_keys, DIFF_QK), BF16),
    ]
    if want_f32:
        out_specs.append(pl.BlockSpec((tm, DIFF_QK), lambda i: (i, 0)))
        out_shape.append(jax.ShapeDtypeStruct((m, DIFF_QK), F32))
        kern = functools.partial(_qkprep_kernel, use_rope)
    else:
        kern = lambda *refs: _qkprep_kernel(use_rope, *refs, None)
    return pl.pallas_call(
        kern,
        grid=(m // tm,),
        in_specs=[
            pl.BlockSpec((tm, DIFF_QK), lambda i: (i, COL_QD // DIFF_QK)),
            pl.BlockSpec((tm, DIFF_QK), lambda i: (i, COL_KD // DIFF_QK)),
            vec, vec, vec, vec, mat, mat, tab, tab,
        ],
        out_specs=out_specs,
        out_shape=out_shape,
        compiler_params=_params(("arbitrary",)),
        name="qk_prepare",
    )(proj, proj, gq, gk, gqp, gkp, grp, rot, cos, sin)


def _diffattn_kernel(n_seg, lam_init, *refs):
    q_ref = refs[0]
    kv_refs = refs[1:1 + 2 * n_seg]
    lam_ref, gain_ref, o_ref = refs[1 + 2 * n_seg:4 + 2 * n_seg]
    scr = refs[4 + 2 * n_seg:]
    i = pl.program_id(2)

    @pl.when(i == 0)
    def _():
        for s in range(n_seg):
            k = kv_refs[2 * s][0]
            lane = lax.broadcasted_iota(jnp.int32, k.shape, 1)
            zero = jnp.zeros_like(k)
            scr[3 * s][...] = jnp.where(lane < DIFF_HEAD_DIM, k, zero)
            scr[3 * s + 1][...] = jnp.where(lane >= DIFF_HEAD_DIM, k, zero)
            v = kv_refs[2 * s + 1][0]
            scr[3 * s + 2][:, 0:DIFF_V_HEAD] = v
            scr[3 * s + 2][:, DIFF_V_HEAD:] = jnp.ones_like(v)

    q = q_ref[0]
    scores = [[_dot_nt(q, scr[3 * s + w][...]) for s in range(n_seg)] for w in range(2)]

    def finish(sc_list):
        mx = sc_list[0].max(axis=-1, keepdims=True)
        for sc in sc_list[1:]:
            mx = jnp.maximum(mx, sc.max(axis=-1, keepdims=True))
        acc = None
        for s, sc in enumerate(sc_list):
            p = jnp.exp2(sc - mx).astype(BF16)
            part = _dot(p, scr[3 * s + 2][...])
            acc = part if acc is None else acc + part
        return acc[:, 0:DIFF_V_HEAD] / acc[:, DIFF_V_HEAD:]

    lp = lam_ref[...]
    lam = (jnp.exp(jnp.sum(lp[0:1] * lp[1:2], axis=-1, keepdims=True))
           - jnp.exp(jnp.sum(lp[2:3] * lp[3:4], axis=-1, keepdims=True)) + lam_init)
    o = finish(scores[0]) - lam * finish(scores[1])
    ms = jnp.mean(o * o, axis=-1, keepdims=True)
    o = o * lax.rsqrt(ms + EPS) * gain_ref[...] * (1.0 - lam_init)
    o_ref[0] = o.astype(BF16)


def _diff_attention(qn, segments, lam_p, gain, lam_init, nb, tq_total, tq):
    n_seg = len(segments)
    hd = 2 * DIFF_HEAD_DIM
    in_specs = [pl.BlockSpec((1, tq, hd), lambda b, h, i: (b, i, h))]
    args = [qn]
    scratch = []
    for (ka, kc, va, vc, tk) in segments:
        in_specs.append(pl.BlockSpec((1, tk, hd), lambda b, h, i, kc=kc: (b, 0, kc + h)))
        in_specs.append(pl.BlockSpec((1, tk, DIFF_V_HEAD), lambda b, h, i, vc=vc: (b, 0, vc + h)))
        args += [ka, va]
        scratch += [pltpu.VMEM((tk, hd), BF16), pltpu.VMEM((tk, hd), BF16),
                    pltpu.VMEM((tk, 2 * DIFF_V_HEAD), BF16)]
    in_specs.append(pl.BlockSpec((4, DIFF_HEAD_DIM), lambda b, h, i: (0, 0)))
    in_specs.append(pl.BlockSpec((1, DIFF_V_HEAD), lambda b, h, i: (0, 0)))
    args += [lam_p, gain]
    return pl.pallas_call(
        functools.partial(_diffattn_kernel, n_seg, lam_init),
        grid=(nb, DIFF_HEADS, tq_total // tq),
        in_specs=in_specs,
        out_specs=pl.BlockSpec((1, tq, DIFF_V_HEAD), lambda b, h, i: (b, i, h)),
        out_shape=jax.ShapeDtypeStruct((nb, tq_total, DIFF_V), BF16),
        scratch_shapes=scratch,
        compiler_params=_params(("arbitrary", "arbitrary", "arbitrary")),
        name="diff_attention",
    )(*args)


def _gla_tables():
    c = GLA_C
    nl = len(GLA_LEVELS)
    mst = np.zeros((2, GLA_MROWS, c), np.float32)
    xm = np.zeros((2, nl + 1, c), np.float32)
    pm = np.zeros((2, nl + 1, c, c), np.float32)
    idx = np.arange(c)
    for d in range(2):
        ip = idx if d == 0 else c - 1 - idx
        mst[d, 0:c] = (ip[None, :] <= ip[:, None])
        mst[d, c * (nl + 1)] = 1.0
        pm[d, 0] = np.eye(c)
        xm[d, 0] = 1.0
        for li, s in enumerate(GLA_LEVELS):
            ref = (ip // (2 * s)) * (2 * s) + s - 1
            mst[d, c * (li + 1):c * (li + 2)] = (ip[None, :] <= ref[:, None])
            odd = (ip // s) % 2 == 1
            xm[d, li + 1] = odd
            same = (ip[:, None] // (2 * s)) == (ip[None, :] // (2 * s))
            pm[d, li + 1] = odd[:, None] & (~odd)[None, :] & same
    pm = np.tile(pm, (1, 1, GLA_HEADS, 1))
    rows = np.arange(GLA_HEADS * c)[:, None] // c
    hm = (rows == (np.arange(GLA_K)[None, :] // GLA_DK)).astype(np.float32)
    hmx = hm[None, None] * np.tile(xm, (1, 1, GLA_HEADS))[:, :, :, None]
    ym = np.broadcast_to((1.0 - xm[:, 1:])[:, :, :, None], (2, nl, c, GLA_K)).copy()
    bd = ((np.arange(GLA_V)[:, None] // GLA_DV) == (np.arange(GLA_K)[None, :] // GLA_DK)).astype(np.float32)
    return mst, hmx, ym, pm, bd


def _gla_kernel(bb, q_ref, k_ref, v_ref, g_ref, mst_ref, hmx_ref, ym_ref, pm_ref, bd_ref, s0_ref,
                o_ref, sfin_ref, s_scr):
    c_idx = pl.program_id(2)

    @pl.when(c_idx == 0)
    def _():
        s_scr[...] = s0_ref[:, 0]

    c = GLA_C
    nl = len(GLA_LEVELS)
    m = mst_ref[0]
    seqs = range(bb)

    def stack(xb, li):
        return jnp.concatenate([xb] * GLA_HEADS, axis=0) * hmx_ref[0, li]

    r_all = []
    for b in seqs:
        g = g_ref[b]
        g1 = g.astype(BF16)
        g2 = (g - g1.astype(F32)).astype(BF16)
        r_all.append(_dot(m, g1) + _dot(m, g2))
    cum = [r[0:c] for r in r_all]
    tot = [r[c * (nl + 1):c * (nl + 1) + 1] for r in r_all]
    q = [q_ref[b].astype(F32) * (GLA_DK ** -0.5) for b in seqs]
    k = [k_ref[b].astype(F32) for b in seqs]

    o_inter = []
    for b in seqs:
        state = s_scr[b]
        o_inter.append(_dot_nt((q[b] * jnp.exp(cum[b])).astype(BF16), state.astype(BF16)))
        k_dec = (k[b] * jnp.exp(tot[b] - cum[b])).astype(BF16)
        s_new = state * jnp.exp(tot[b]) + _dot_tn(v_ref[b], k_dec) * bd_ref[...]
        s_scr[b] = s_new
        sfin_ref[b, 0] = s_new

    att = [pm_ref[0, 0] * _dot_nt(stack(q[b].astype(BF16), 0), k_ref[b]) for b in seqs]
    for li in range(nl):
        for b in seqs:
            ref = r_all[b][c * (li + 1):c * (li + 2)]
            e = jnp.exp(-jnp.abs(cum[b] - ref))
            xs = stack((q[b] * e).astype(BF16), li + 1)
            ys = (k[b] * e).astype(BF16) * ym_ref[0, li]
            att[b] = att[b] + pm_ref[0, li + 1] * _dot_nt(xs, ys)
    for b in seqs:
        v = v_ref[b]
        outs = []
        for h in range(GLA_HEADS):
            a_h = att[b][h * c:(h + 1) * c].astype(BF16)
            outs.append(_dot(a_h, v[:, h * GLA_DV:(h + 1) * GLA_DV]))
        o_ref[0, b] = o_inter[b] + jnp.concatenate(outs, axis=1)


def _gla(proj3, g3, s0, tables, nb, t, bb):
    mst, hmx, ym, pm, bd = tables
    nc = t // GLA_C
    nl = len(GLA_LEVELS)

    def chunk(d, c):
        return jnp.where(d == 0, c, nc - 1 - c)

    return pl.pallas_call(
        functools.partial(_gla_kernel, bb),
        grid=(nb // bb, 2, nc),
        in_specs=[
            pl.BlockSpec((bb, GLA_C, GLA_K), lambda b, d, c: (b, chunk(d, c), COL_QG // GLA_K)),
            pl.BlockSpec((bb, GLA_C, GLA_K), lambda b, d, c: (b, chunk(d, c), COL_KG // GLA_K)),
            pl.BlockSpec((bb, GLA_C, GLA_V), lambda b, d, c: (b, chunk(d, c), COL_VG // GLA_V)),
            pl.BlockSpec((bb, GLA_C, GLA_K), lambda b, d, c: (b, chunk(d, c), d)),
            pl.BlockSpec((1, GLA_MROWS, GLA_C), lambda b, d, c: (d, 0, 0)),
            pl.BlockSpec((1, nl + 1, GLA_HEADS * GLA_C, GLA_K), lambda b, d, c: (d, 0, 0, 0)),
            pl.BlockSpec((1, nl, GLA_C, GLA_K), lambda b, d, c: (d, 0, 0, 0)),
            pl.BlockSpec((1, nl + 1, GLA_HEADS * GLA_C, GLA_C), lambda b, d, c: (d, 0, 0, 0)),
            pl.BlockSpec((GLA_V, GLA_K), lambda b, d, c: (0, 0)),
            pl.BlockSpec((bb, 1, GLA_V, GLA_K), lambda b, d, c: (b, d, 0, 0)),
        ],
        out_specs=[
            pl.BlockSpec((1, bb, GLA_C, GLA_V), lambda b, d, c: (d, b, chunk(d, c), 0)),
            pl.BlockSpec((bb, 1, GLA_V, GLA_K), lambda b, d, c: (b, d, 0, 0)),
        ],
        out_shape=[
            jax.ShapeDtypeStruct((2, nb, t, GLA_V), F32),
            jax.ShapeDtypeStruct((nb, 2, GLA_V, GLA_K), F32),
        ],
        scratch_shapes=[pltpu.VMEM((bb, GLA_V, GLA_K), F32)],
        compiler_params=_params(("arbitrary", "arbitrary", "arbitrary")),
        name="gla",
    )(proj3, proj3, proj3, g3, mst, hmx, ym, pm, bd, s0)


def _fourier_kernel(t, u_ref, bcs_ref, ct_ref, o_ref, y_scr):
    i = pl.program_id(1)

    @pl.when(i == 0)
    def _():
        rows = min(t, 512)
        for r in range(t // rows):
            y = _dot(u_ref[0, r * rows:(r + 1) * rows, :], bcs_ref[...])
            y_scr[r * rows:(r + 1) * rows, :] = y[:, :FOU_W].astype(BF16)
            y_scr[t + r * rows:t + (r + 1) * rows, :] = y[:, FOU_W:].astype(BF16)

    o_ref[0] = _dot(ct_ref[...], y_scr[...]).astype(BF16)


def _fourier(proj3, bcs, ct, nb, t, tm):
    return pl.pallas_call(
        functools.partial(_fourier_kernel, t),
        grid=(nb, t // tm),
        in_specs=[
            pl.BlockSpec((1, t, FOU_W), lambda b, i: (b, 0, COL_UF // FOU_W)),
            pl.BlockSpec((FOU_W, 2 * FOU_W), lambda b, i: (0, 0)),
            pl.BlockSpec((tm, 2 * t), lambda b, i: (i, 0)),
        ],
        out_specs=pl.BlockSpec((1, tm, FOU_W), lambda b, i: (b, i, 0)),
        out_shape=jax.ShapeDtypeStruct((nb, t, FOU_W), BF16),
        scratch_shapes=[pltpu.VMEM((2 * t, FOU_W), BF16)],
        compiler_params=_params(("arbitrary", "arbitrary")),
        name="fourier_mix",
    )(proj3, bcs, ct)


def _dft_tables(t):
    lo = min(t, 64)
    hi = t // lo
    f = jnp.arange(t, dtype=jnp.int32)[None, :]
    unit = 2.0 * math.pi / t
    a = ((lo * jnp.arange(hi, dtype=jnp.int32)[:, None] * f) % t).astype(F32) * unit
    b = ((jnp.arange(lo, dtype=jnp.int32)[:, None] * f) % t).astype(F32) * unit
    ca, sa = jnp.cos(a)[:, None, :], jnp.sin(a)[:, None, :]
    cb, sb = jnp.cos(b)[None, :, :], jnp.sin(b)[None, :, :]
    scale = t ** -0.5
    cos = ((ca * cb - sa * sb) * scale).reshape(t, t)
    msin = ((sa * cb + ca * sb) * -scale).reshape(t, t)
    return jnp.concatenate([cos, msin], axis=1).astype(BF16)


def _channel_dft_table():
    n = np.arange(FOU_GROUP_W)
    ang = 2.0 * np.pi * ((n[:, None] * n[None, :]) % FOU_GROUP_W) / FOU_GROUP_W
    eye = np.eye(FOU_GROUPS)
    bc = np.kron(eye, np.cos(ang)) * FOU_GROUP_W ** -0.5
    bs = np.kron(eye, np.sin(ang)) * FOU_GROUP_W ** -0.5
    return np.concatenate([bc, bs], axis=1).astype(np.float32)


def _merge_kernel(x_ref, g1_ref, fm_ref, of_ref, ob_ref, r_ref, od_ref, gt_ref,
                  wf_ref, wg_ref, wd_ref, wo_ref, gn_ref, grp_ref, o_ref):
    y_f = _dot(fm_ref[...], wf_ref[...])
    og = of_ref[0] + ob_ref[0]
    ms = _dot((og * og).astype(BF16), grp_ref[...])
    r = r_ref[...].astype(F32)
    og = og * lax.rsqrt(ms + EPS) * gn_ref[...] * (r * jax.nn.sigmoid(r))
    y_g = _dot(og.astype(BF16), wg_ref[...])
    y_d = _dot(od_ref[...], wd_ref[...])
    gates = jax.nn.sigmoid(gt_ref[...].astype(F32))
    merged = (gates[:, 0:D_MODEL] * y_f + gates[:, D_MODEL:2 * D_MODEL] * y_g
              + gates[:, 2 * D_MODEL:3 * D_MODEL] * y_d)
    o_ref[...] = x_ref[...] + g1_ref[0] * _dot(merged.astype(BF16), wo_ref[...])


def _merge(x, g1, proj, fm, o_gla, od, w_fou, w_gla_o, w_diff_o, w_out, gn, grp, tokens_per_cond, tm):
    m = x.shape[0]
    per = tokens_per_cond // tm
    ncond = g1.shape[0]
    cidx = (lambda i: (i // per, 0, 0)) if ncond > 1 else (lambda i: (0, 0, 0))
    half = pl.BlockSpec((FOU_W, D_MODEL), lambda i: (0, 0))
    return pl.pallas_call(
        _merge_kernel,
        grid=(m // tm,),
        in_specs=[
            pl.BlockSpec((tm, D_MODEL), lambda i: (i, 0)),
            pl.BlockSpec((1, 1, D_MODEL), cidx),
            pl.BlockSpec((tm, FOU_W), lambda i: (i, 0)),
            pl.BlockSpec((1, tm, GLA_V), lambda i: (0, i, 0)),
            pl.BlockSpec((1, tm, GLA_V), lambda i: (1, i, 0)),
            pl.BlockSpec((tm, GLA_V), lambda i: (i, COL_RG // GLA_V)),
            pl.BlockSpec((tm, DIFF_V), lambda i: (i, 0)),
            pl.BlockSpec((tm, N_BRANCH * D_MODEL), lambda i: (i, COL_GATES)),
            half, half, half,
            pl.BlockSpec((D_MODEL, D_MODEL), lambda i: (0, 0)),
            pl.BlockSpec((1, GLA_V), lambda i: (0, 0)),
            pl.BlockSpec((GLA_V, GLA_V), lambda i: (0, 0)),
        ],
        out_specs=pl.BlockSpec((tm, D_MODEL), lambda i: (i, 0)),
        out_shape=jax.ShapeDtypeStruct((m, D_MODEL), F32),
        compiler_params=_params(("arbitrary",)),
        name="merge",
    )(x, g1, fm, o_gla, o_gla, proj, od, proj, w_fou, w_gla_o, w_diff_o, w_out, gn, grp)


def _ffn_kernel(tf, x_ref, sc_ref, sh_ref, g2_ref, gain_ref, wg_ref, wu_ref, wd_ref, o_ref, h_scr):
    x = x_ref[...]
    ms = jnp.mean(x * x, axis=-1, keepdims=True)
    scale = gain_ref[...] * (1.0 + sc_ref[0])
    h_scr[...] = (x * lax.rsqrt(ms + EPS) * scale + sh_ref[0]).astype(BF16)
    acc = None
    for f in range(D_FF // tf):
        cols = slice(f * tf, (f + 1) * tf)
        gate = _dot(h_scr[...], wg_ref[:, cols])
        up = _dot(h_scr[...], wu_ref[:, cols])
        a = (gate * jax.nn.sigmoid(gate) * up).astype(BF16)
        part = _dot(a, wd_ref[cols, :])
        acc = part if acc is None else acc + part
    o_ref[...] = x_ref[...] + g2_ref[0] * acc


def _ffn(x, sc, sh, g2, gain, w_gate, w_up, w_down, tokens_per_cond, tm):
    m = x.shape[0]
    tf = 256
    per = tokens_per_cond // tm
    ncond = sc.shape[0]
    cidx = (lambda i: (i // per, 0, 0)) if ncond > 1 else (lambda i: (0, 0, 0))
    const = lambda i: (0, 0)
    once = pl.Buffered(1)
    return pl.pallas_call(
        functools.partial(_ffn_kernel, tf),
        grid=(m // tm,),
        in_specs=[
            pl.BlockSpec((tm, D_MODEL), lambda i: (i, 0)),
            pl.BlockSpec((1, 1, D_MODEL), cidx),
            pl.BlockSpec((1, 1, D_MODEL), cidx),
            pl.BlockSpec((1, 1, D_MODEL), cidx),
            pl.BlockSpec((1, D_MODEL), const),
            pl.BlockSpec((D_MODEL, D_FF), const, pipeline_mode=once),
            pl.BlockSpec((D_MODEL, D_FF), const, pipeline_mode=once),
            pl.BlockSpec((D_FF, D_MODEL), const, pipeline_mode=once),
        ],
        out_specs=pl.BlockSpec((tm, D_MODEL), lambda i: (i, 0)),
        out_shape=jax.ShapeDtypeStruct((m, D_MODEL), F32),
        scratch_shapes=[pltpu.VMEM((tm, D_MODEL), BF16)],
        compiler_params=_params(("arbitrary",)),
        name="ffn",
    )(x, sc, sh, g2, gain, w_gate, w_up, w_down)


def _group_mean_matrix(width, group):
    idx = np.arange(width) // group
    return (idx[:, None] == idx[None, :]).astype(np.float32) / group


def _rotate_half_matrix():
    r = np.zeros((DIFF_HEAD_DIM, DIFF_HEAD_DIM), np.float32)
    for axis in range(2):
        base = axis * ROPE_AXIS_DIM
        for f in range(ROPE_FREQS):
            r[base + ROPE_FREQS + f, base + f] = -1.0
            r[base + f, base + ROPE_FREQS + f] = 1.0
    return np.kron(np.eye(DIFF_QK // DIFF_HEAD_DIM), r).astype(np.float32)


def _rotate_half_perm():
    p = np.zeros((DIFF_HEAD_DIM,), np.int32)
    for axis in range(2):
        base = axis * ROPE_AXIS_DIM
        for f in range(ROPE_FREQS):
            p[base + f] = base + ROPE_FREQS + f
            p[base + ROPE_FREQS + f] = base + f
    return p


def _rope_tables(n_tokens):
    rows = n_tokens // GRID_W
    row = jnp.repeat(jnp.arange(rows), GRID_W).astype(F32)
    col = jnp.tile(jnp.arange(GRID_W), rows).astype(F32)
    inv = ROPE_BASE ** (-jnp.arange(ROPE_FREQS, dtype=F32) * 2.0 / ROPE_AXIS_DIM)
    ang_r = row[:, None] * inv
    ang_c = col[:, None] * inv
    ang = jnp.concatenate([ang_r, ang_r, ang_c, ang_c], axis=-1)
    reps = DIFF_QK // DIFF_HEAD_DIM
    return jnp.tile(jnp.cos(ang), (1, reps)), jnp.tile(jnp.sin(ang), (1, reps))


def _block_diag_state(s):
    eye = jnp.eye(GLA_HEADS, dtype=s.dtype)
    full = jnp.einsum('bdhkv,hg->bdhvgk', s, eye)
    return full.reshape(s.shape[0], 2, GLA_V, GLA_K)


def _unblock_state(sb):
    nb = sb.shape[0]
    s = sb.reshape(nb, 2, GLA_HEADS, GLA_DV, GLA_HEADS, GLA_DK)
    diag = jnp.stack([s[:, :, h, :, h, :] for h in range(GLA_HEADS)], axis=2)
    return jnp.swapaxes(diag, -1, -2)


def _run_path(x, mods, layers, consts, nb, t, ctx):
    m = nb * t
    tm = 512 if t >= 512 else t
    x = x.reshape(m, D_MODEL)
    is_latent = ctx is not None
    cos, sin = consts['rope'] if is_latent else (consts['ones'], consts['ones'])
    ct = consts['ct_lat'] if is_latent else consts['ct_ctx']
    k_out, v_out, s_out = [], [], []
    for l, p in enumerate(layers):
        sh1, sc1, g1, sh2, sc2, g2 = [mods[l][:, None, j * D_MODEL:(j + 1) * D_MODEL] for j in range(6)]
        proj, g = _in_projection(x, sc1, sh1, p['norm1'], p['w_main'], p['w_a'], p['w2'], p['b2'], t, tm)
        proj3 = proj.reshape(nb, t, PROJ_W)

        prep = _qk_prepare(proj, p['qk_gains'], consts['grp64'], consts['rot'], cos, sin,
                           nb, t, t, is_latent, not is_latent, tm)
        qn, kn = prep[0], prep[1]
        lam_init = 0.8 - 0.6 * math.exp(-0.3 * l)
        segments = [(kn, 0, proj3, COL_VD // DIFF_V_HEAD, t)]
        if is_latent:
            ck = ctx[1][:, l].reshape(nb, -1, DIFF_QK).astype(BF16)
            cv = ctx[2][:, l].reshape(nb, -1, DIFF_V).astype(BF16)
            segments.append((ck, 0, cv, 0, ck.shape[1]))
        od = _diff_attention(qn.reshape(nb, t, DIFF_QK), segments, p['diff_lambda'], p['diff_norm'],
                             lam_init, nb, t, min(t, 256))

        if is_latent:
            s0 = _block_diag_state(ctx[0][:, l])
        else:
            s0 = jnp.zeros((nb, 2, GLA_V, GLA_K), F32)
        o_gla, s_fin = _gla(proj3, g.reshape(nb, t, 2 * GLA_K), s0, consts['gla'], nb, t, math.gcd(nb, GLA_BB))

        fm = _fourier(proj3, consts['bcs'], ct, nb, t, min(t, 256))

        x = _merge(x, g1, proj, fm.reshape(m, FOU_W), o_gla.reshape(2, m, GLA_V), od.reshape(m, DIFF_V),
                   p['w_fou'], p['w_gla_o'], p['w_diff_o'], p['w_out'], p['gla_norm'], consts['grp128'],
                   t, min(tm, 256))
        x = _ffn(x, sc2, sh2, g2, p['norm2'], p['w_ff_gate'], p['w_ff_up'], p['w_ff_down'], t, tm)

        if not is_latent:
            k_out.append(prep[2].reshape(nb, t, DIFF_HEADS, 2, DIFF_HEAD_DIM))
            v_out.append(proj3[:, :, COL_VD:COL_VD + DIFF_V].astype(F32).reshape(nb, t, DIFF_HEADS, DIFF_V_HEAD))
            s_out.append(_unblock_state(s_fin))
    return x.reshape(nb, t, D_MODEL), k_out, v_out, s_out


def kernel(x_prompt, x_sample, c, cache_diff_k, cache_diff_v, state_gla, c_ctx, w_mod, b_mod, norm1, norm2, w_in, w_gla_a2, b_gla_a, gla_norm, diff_qk_norm, diff_lambda, diff_norm, w_fou, w_gla_o, w_diff_o, w_out, w_ff_gate, w_ff_up, w_ff_down):
    nb_ctx, t_ctx, _ = x_prompt.shape
    nb_lat, t_lat, _ = x_sample.shape

    n_cond = 1 + nb_lat
    r_pad = -(-n_cond // 8) * 8
    cond = jnp.zeros((r_pad, D_MODEL), F32).at[0].set(c_ctx).at[1:n_cond].set(c)
    mods = _modulation(cond, w_mod, b_mod)
    mods_ctx = [mods[l, 0:1] for l in range(DEPTH)]
    mods_lat = [mods[l, 1:n_cond] for l in range(DEPTH)]

    a0 = 2048
    a1 = a0 + 2 * GATE_RANK
    pieces = {
        'u_f': (0, 512), 'q_g': (512, 768), 'k_g': (768, 1024), 'v_g': (1024, 1536), 'r_g': (1536, 2048),
        'q_d': (a1, a1 + 512), 'k_d': (a1 + 512, a1 + 1024), 'v_d': (a1 + 1024, a1 + 1536),
        'gates': (a1 + 1536, a1 + 1536 + 3072),
    }
    order = ['gates', 'u_f', 'v_g', 'r_g', 'q_d', 'k_d', 'v_d', 'q_g', 'k_g']
    w_main = jnp.concatenate([w_in[:, :, pieces[n][0]:pieces[n][1]] for n in order], axis=-1).astype(BF16)
    w_a = jnp.pad(w_in[:, :, a0:a1], ((0, 0), (0, 0), (0, A_PAD - 2 * GATE_RANK))).astype(BF16)
    w2 = jnp.zeros((DEPTH, A_PAD, 2 * GLA_K), F32)
    w2 = w2.at[:, 0:GATE_RANK, 0:GLA_K].set(w_gla_a2[:, 0])
    w2 = w2.at[:, GATE_RANK:2 * GATE_RANK, GLA_K:].set(w_gla_a2[:, 1])
    b2 = b_gla_a.reshape(DEPTH, 1, 2 * GLA_K)

    perm = _rotate_half_perm()
    reps = DIFF_QK // DIFF_HEAD_DIM
    layers = []
    for l in range(DEPTH):
        gq = diff_qk_norm[l, 0]
        gk = diff_qk_norm[l, 1]
        layers.append({
            'norm1': norm1[l][None, :], 'norm2': norm2[l][None, :],
            'w_main': w_main[l], 'w_a': w_a[l], 'w2': w2[l], 'b2': b2[l],
            'qk_gains': tuple(jnp.tile(v, reps)[None, :] for v in (gq, gk, gq[perm], gk[perm])),
            'diff_lambda': diff_lambda[l], 'diff_norm': diff_norm[l][None, :],
            'gla_norm': jnp.tile(gla_norm[l], GLA_HEADS)[None, :],
            'w_fou': w_fou[l].astype(BF16), 'w_gla_o': w_gla_o[l].astype(BF16),
            'w_diff_o': w_diff_o[l].astype(BF16), 'w_out': w_out[l].astype(BF16),
            'w_ff_gate': w_ff_gate[l].astype(BF16), 'w_ff_up': w_ff_up[l].astype(BF16),
            'w_ff_down': w_ff_down[l].astype(BF16),
        })

    mst, hmx, ym, pm, bd = _gla_tables()
    consts = {
        'grp64': jnp.asarray(_group_mean_matrix(DIFF_QK, DIFF_HEAD_DIM), BF16),
        'grp128': jnp.asarray(_group_mean_matrix(GLA_V, GLA_DV), BF16),
        'rot': jnp.asarray(_rotate_half_matrix(), BF16),
        'rope': _rope_tables(t_lat),
        'ones': jnp.ones((t_ctx, DIFF_QK), F32),
        'bcs': jnp.asarray(_channel_dft_table(), BF16),
        'ct_ctx': _dft_tables(t_ctx),
        'ct_lat': _dft_tables(t_lat),
        'gla': (jnp.asarray(mst, BF16), jnp.asarray(hmx, BF16), jnp.asarray(ym, BF16), jnp.asarray(pm),
                jnp.asarray(bd)),
    }

    y_prompt, k_list, v_list, s_list = _run_path(x_prompt, mods_ctx, layers, consts, nb_ctx, t_ctx, None)
    y_sample, _, _, _ = _run_path(x_sample, mods_lat, layers, consts, nb_lat, t_lat,
                                  (state_gla, cache_diff_k, cache_diff_v))
    return (y_prompt, y_sample, jnp.stack(k_list, axis=1), jnp.stack(v_list, axis=1),
            jnp.stack(s_list, axis=1))
```

```python
import functools
import math

import numpy as np
import jax
import jax.numpy as jnp
from jax import lax
from jax.experimental import pallas as pl
from jax.experimental.pallas import tpu as pltpu

F32 = jnp.float32
BF16 = jnp.bfloat16

D_MODEL = 1024
DEPTH = 4
GRID_W = 64
FOU_GROUPS = 4
FOU_GROUP_W = 128
FOU_W = 512
GLA_HEADS = 4
GLA_DK = 64
GLA_DV = 128
GLA_K = 256
GLA_V = 512
GATE_RANK = 16
GATE_TEMP = 16.0
DIFF_HEADS = 4
DIFF_HEAD_DIM = 64
DIFF_V_HEAD = 128
DIFF_QK = 512
DIFF_V = 512
ROPE_AXIS_DIM = 32
ROPE_FREQS = 16
ROPE_BASE = 10000.0
N_BRANCH = 3
D_FF = 2816
EPS = 1e-6

COL_GATES = 0
COL_UF = 3072
COL_VG = 3584
COL_RG = 4096
COL_QD = 4608
COL_KD = 5120
COL_VD = 5632
COL_QG = 6144
COL_KG = 6400
PROJ_W = 6656
A_PAD = 128

GLA_C = 64
GLA_LEVELS = (32, 16, 8, 4, 2, 1)
GLA_MROWS = 64 * (1 + len(GLA_LEVELS)) + 8
GLA_BB = 4
VMEM_LIMIT = 48 * 1024 * 1024
VMEM_LIMIT_BIG = 56 * 1024 * 1024
LOG2_E = math.log2(math.e)


def _params(sem, vmem=VMEM_LIMIT):
    return pltpu.CompilerParams(dimension_semantics=sem, vmem_limit_bytes=vmem)


def _dot(a, b):
    return jnp.dot(a, b, preferred_element_type=F32)


def _dot_nt(a, b):
    return lax.dot_general(a, b, (((1,), (1,)), ((), ())), preferred_element_type=F32)


def _dot_tn(a, b):
    return lax.dot_general(a, b, (((0,), (0,)), ((), ())), preferred_element_type=F32)


def _split3(x):
    x1 = x.astype(BF16)
    r1 = x - x1.astype(F32)
    x2 = r1.astype(BF16)
    x3 = (r1 - x2.astype(F32)).astype(BF16)
    return x1, x2, x3


def _mod_kernel(c_ref, w_ref, b_ref, o_ref):
    c = c_ref[...]
    s = c * jax.nn.sigmoid(c)
    s1, s2, s3 = _split3(s)
    w = w_ref[0]
    w1, w2, w3 = _split3(w)
    acc = _dot(s1, w1) + (_dot(s1, w2) + _dot(s2, w1)) + (_dot(s2, w2) + _dot(s1, w3) + _dot(s3, w1))
    o_ref[0] = acc + b_ref[0]


def _modulation(cond, w_mod, b_mod):
    r = cond.shape[0]
    tn = 768
    n = 6 * D_MODEL
    return pl.pallas_call(
        _mod_kernel,
        grid=(DEPTH, n // tn),
        in_specs=[
            pl.BlockSpec((r, D_MODEL), lambda l, j: (0, 0)),
            pl.BlockSpec((1, D_MODEL, tn), lambda l, j: (l, 0, j)),
            pl.BlockSpec((1, 1, tn), lambda l, j: (l, 0, j)),
        ],
        out_specs=pl.BlockSpec((1, r, tn), lambda l, j: (l, 0, j)),
        out_shape=jax.ShapeDtypeStruct((DEPTH, r, n), F32),
        compiler_params=_params(("arbitrary", "arbitrary")),
        name="modulation",
    )(cond, w_mod, b_mod.reshape(DEPTH, 1, n))


def _inproj_kernel(tn, x_ref, sc_ref, sh_ref, gain_ref, w_ref, wa_ref, w2_ref, b2_ref,
                   proj_ref, g_ref, h_scr):
    x = x_ref[...]
    ms = jnp.mean(x * x, axis=-1, keepdims=True)
    scale = gain_ref[...] * (1.0 + sc_ref[0])
    hb = (x * lax.rsqrt(ms + EPS) * scale + sh_ref[0]).astype(BF16)
    h_scr[...] = hb
    a = _dot(hb, wa_ref[...])
    a1, a2, a3 = _split3(a)
    v1, v2, v3 = _split3(w2_ref[...])
    z = (_dot(a1, v1) + (_dot(a1, v2) + _dot(a2, v1))
         + (_dot(a2, v2) + _dot(a1, v3) + _dot(a3, v1))) + b2_ref[...]
    logsig = jnp.minimum(z, 0.0) - jnp.log1p(jnp.exp(-jnp.abs(z)))
    g_ref[...] = logsig * (1.0 / GATE_TEMP)
    for j in range(PROJ_W // tn):
        proj_ref[:, j * tn:(j + 1) * tn] = _dot(h_scr[...], w_ref[:, j * tn:(j + 1) * tn]).astype(BF16)


def _in_projection(x, sc, sh, gain, w_main, w_a, w2, b2, tokens_per_cond, tm):
    m = x.shape[0]
    tn = 512
    per = tokens_per_cond // tm
    ncond = sc.shape[0]
    cidx = (lambda i: (i // per, 0, 0)) if ncond > 1 else (lambda i: (0, 0, 0))
    const = lambda i: (0, 0)
    once = pl.Buffered(1)
    return pl.pallas_call(
        functools.partial(_inproj_kernel, tn),
        grid=(m // tm,),
        in_specs=[
            pl.BlockSpec((tm, D_MODEL), lambda i: (i, 0)),
            pl.BlockSpec((1, 1, D_MODEL), cidx),
            pl.BlockSpec((1, 1, D_MODEL), cidx),
            pl.BlockSpec((1, D_MODEL), const),
            pl.BlockSpec((D_MODEL, PROJ_W), const, pipeline_mode=once),
            pl.BlockSpec((D_MODEL, A_PAD), const, pipeline_mode=once),
            pl.BlockSpec((A_PAD, 2 * GLA_K), const, pipeline_mode=once),
            pl.BlockSpec((1, 2 * GLA_K), const),
        ],
        out_specs=[
            pl.BlockSpec((tm, PROJ_W), lambda i: (i, 0)),
            pl.BlockSpec((tm, 2 * GLA_K), lambda i: (i, 0)),
        ],
        out_shape=[
            jax.ShapeDtypeStruct((m, PROJ_W), BF16),
            jax.ShapeDtypeStruct((m, 2 * GLA_K), F32),
        ],
        scratch_shapes=[pltpu.VMEM((tm, D_MODEL), BF16)],
        compiler_params=_params(("arbitrary",), VMEM_LIMIT_BIG),
        name="in_projection",
    )(x, sc, sh, gain, w_main, w_a, w2, b2)


def _qkprep_kernel(use_rope, q_ref, k_ref, gq_ref, gk_ref, gqp_ref, gkp_ref, grp_ref, rot_ref,
                   cos_ref, sin_ref, qn_ref, kn_ref, kf_ref):
    def prep(xb, g_ref, gp_ref, scale):
        x = xb.astype(F32)
        ms = _dot((x * x).astype(BF16), grp_ref[...])
        r = lax.rsqrt(ms + EPS)
        y = x * r * g_ref[...]
        if use_rope:
            yr = _dot(xb, rot_ref[...]) * r * gp_ref[...]
            y = y * cos_ref[...] + yr * sin_ref[...]
        return y * scale

    qn_ref[...] = prep(q_ref[...], gq_ref, gqp_ref, DIFF_HEAD_DIM ** -0.5 * LOG2_E).astype(BF16)
    kn = prep(k_ref[...], gk_ref, gkp_ref, 1.0)
    kn_ref[0] = kn.astype(BF16)
    if kf_ref is not None:
        kf_ref[...] = kn


def _qk_prepare(proj, gains, grp, rot, cos, sin, nb, t, t_keys, use_rope, want_f32, tm):
    m = proj.shape[0]
    per = t // tm
    gq, gk, gqp, gkp = gains
    vec = pl.BlockSpec((1, DIFF_QK), lambda i: (0, 0))
    mat = pl.BlockSpec((DIFF_QK, DIFF_QK), lambda i: (0, 0))
    tab = pl.BlockSpec((tm, DIFF_QK), lambda i: (i % per, 0))
    out_specs = [
        pl.BlockSpec((tm, DIFF_QK), lambda i: (i, 0)),
        pl.BlockSpec((1, tm, DIFF_QK), lambda i: (i // per, i % per, 0)),
    ]
    out_shape = [
        jax.ShapeDtypeStruct((m, DIFF_QK), BF16),
        jax.ShapeDtypeStruct((nb, t_keys, DIFF_QK), BF16),
    ]
    if want_f32:
        out_specs.append(pl.BlockSpec((tm, DIFF_QK), lambda i: (i, 0)))
        out_shape.append(jax.ShapeDtypeStruct((m, DIFF_QK), F32))
        kern = functools.partial(_qkprep_kernel, use_rope)
    else:
        kern = lambda *refs: _qkprep_kernel(use_rope, *refs, None)
    return pl.pallas_call(
        kern,
        grid=(m // tm,),
        in_specs=[
            pl.BlockSpec((tm, DIFF_QK), lambda i: (i, COL_QD // DIFF_QK)),
            pl.BlockSpec((tm, DIFF_QK), lambda i: (i, COL_KD // DIFF_QK)),
            vec, vec, vec, vec, mat, mat, tab, tab,
        ],
        out_specs=out_specs,
        out_shape=out_shape,
        compiler_params=_params(("arbitrary",)),
        name="qk_prepare",
    )(proj, proj, gq, gk, gqp, gkp, grp, rot, cos, sin)


def _diffattn_kernel(seg_len, lam_init, *refs):
    n_seg = len(seg_len)
    q_ref = refs[0]
    kv_refs = refs[1:1 + 2 * n_seg]
    lam_ref, gain_ref, o_ref = refs[1 + 2 * n_seg:4 + 2 * n_seg]
    kz0, kz1, vaug, s_a, s_b, m_a, m_b = refs[4 + 2 * n_seg:]
    i = pl.program_id(2)

    @pl.when(i == 0)
    def _():
        off = 0
        for s, tk in enumerate(seg_len):
            k = kv_refs[2 * s][0]
            lane = lax.broadcasted_iota(jnp.int32, k.shape, 1)
            zero = jnp.zeros_like(k)
            kz0[off:off + tk, :] = jnp.where(lane < DIFF_HEAD_DIM, k, zero)
            kz1[off:off + tk, :] = jnp.where(lane >= DIFF_HEAD_DIM, k, zero)
            v = kv_refs[2 * s + 1][0]
            vaug[off:off + tk, 0:DIFF_V_HEAD] = v
            vaug[off:off + tk, DIFF_V_HEAD:] = jnp.ones_like(v)
            off += tk

    @pl.when((pl.program_id(0) == 0) & (pl.program_id(1) == 0) & (i == 0))
    def _():
        s_b[...] = jnp.zeros_like(s_b)
        m_b[...] = jnp.zeros_like(m_b)

    lp = lam_ref[...]
    lam = (jnp.exp(jnp.sum(lp[0:1] * lp[1:2], axis=-1, keepdims=True))
           - jnp.exp(jnp.sum(lp[2:3] * lp[3:4], axis=-1, keepdims=True)) + lam_init)

    def score(w, kz, s_write, m_write):
        sc = _dot_nt(q_ref[0], kz[...])
        s_write[w] = sc
        m_write[w] = jnp.broadcast_to(sc.max(axis=-1, keepdims=True), m_write.shape[1:])

    def finish(w, s_read, m_read):
        p = jnp.exp2(s_read[w] - m_read[w][:, 0:1]).astype(BF16)
        acc = _dot(p, vaug[...])
        return acc[:, 0:DIFF_V_HEAD] / acc[:, DIFF_V_HEAD:]

    def step(s_write, m_write, s_read, m_read):
        score(0, kz0, s_write, m_write)
        score(1, kz1, s_write, m_write)
        o = finish(0, s_read, m_read) - lam * finish(1, s_read, m_read)
        ms = jnp.mean(o * o, axis=-1, keepdims=True)
        o = o * lax.rsqrt(ms + EPS) * gain_ref[...] * (1.0 - lam_init)
        o_ref[0] = o.astype(BF16)

    @pl.when(i % 2 == 0)
    def _():
        step(s_a, m_a, s_b, m_b)

    @pl.when(i % 2 == 1)
    def _():
        step(s_b, m_b, s_a, m_a)


def _diff_attention(qn, segments, lam_p, gain, lam_init, nb, tq_total, tq):
    hd = 2 * DIFF_HEAD_DIM
    nq = tq_total // tq
    in_specs = [pl.BlockSpec((1, tq, hd), lambda b, h, i: (b, jnp.minimum(i, nq - 1), h))]
    args = [qn]
    seg_len = tuple(seg[4] for seg in segments)
    tk_all = sum(seg_len)
    for (ka, kc, va, vc, tk) in segments:
        in_specs.append(pl.BlockSpec((1, tk, hd), lambda b, h, i, kc=kc: (b, 0, kc + h)))
        in_specs.append(pl.BlockSpec((1, tk, DIFF_V_HEAD), lambda b, h, i, vc=vc: (b, 0, vc + h)))
        args += [ka, va]
    in_specs.append(pl.BlockSpec((4, DIFF_HEAD_DIM), lambda b, h, i: (0, 0)))
    in_specs.append(pl.BlockSpec((1, DIFF_V_HEAD), lambda b, h, i: (0, 0)))
    args += [lam_p, gain]
    scratch = [
        pltpu.VMEM((tk_all, hd), BF16), pltpu.VMEM((tk_all, hd), BF16),
        pltpu.VMEM((tk_all, 2 * DIFF_V_HEAD), BF16),
        pltpu.VMEM((2, tq, tk_all), F32), pltpu.VMEM((2, tq, tk_all), F32),
        pltpu.VMEM((2, tq, hd), F32), pltpu.VMEM((2, tq, hd), F32),
    ]
    return pl.pallas_call(
        functools.partial(_diffattn_kernel, seg_len, lam_init),
        grid=(nb, DIFF_HEADS, nq + 1),
        in_specs=in_specs,
        out_specs=pl.BlockSpec((1, tq, DIFF_V_HEAD), lambda b, h, i: (b, jnp.maximum(i - 1, 0), h)),
        out_shape=jax.ShapeDtypeStruct((nb, tq_total, DIFF_V), BF16),
        scratch_shapes=scratch,
        compiler_params=_params(("arbitrary", "arbitrary", "arbitrary")),
        name="diff_attention",
    )(*args)


def _gla_tables():
    c = GLA_C
    nl = len(GLA_LEVELS)
    mst = np.zeros((2, GLA_MROWS, c), np.float32)
    xm = np.zeros((2, nl + 1, c), np.float32)
    pm = np.zeros((2, nl + 1, c, c), np.float32)
    idx = np.arange(c)
    for d in range(2):
        ip = idx if d == 0 else c - 1 - idx
        mst[d, 0:c] = (ip[None, :] <= ip[:, None])
        mst[d, c * (nl + 1)] = 1.0
        pm[d, 0] = np.eye(c)
        xm[d, 0] = 1.0
        for li, s in enumerate(GLA_LEVELS):
            ref = (ip // (2 * s)) * (2 * s) + s - 1
            mst[d, c * (li + 1):c * (li + 2)] = (ip[None, :] <= ref[:, None])
            odd = (ip // s) % 2 == 1
            xm[d, li + 1] = odd
            same = (ip[:, None] // (2 * s)) == (ip[None, :] // (2 * s))
            pm[d, li + 1] = odd[:, None] & (~odd)[None, :] & same
    pm = np.tile(pm, (1, 1, GLA_HEADS, 1))
    rows = np.arange(GLA_HEADS * c)[:, None] // c
    hm = (rows == (np.arange(GLA_K)[None, :] // GLA_DK)).astype(np.float32)
    hmx = hm[None, None] * np.tile(xm, (1, 1, GLA_HEADS))[:, :, :, None]
    ym = np.broadcast_to((1.0 - xm[:, 1:])[:, :, :, None], (2, nl, c, GLA_K)).copy()
    bd = ((np.arange(GLA_V)[:, None] // GLA_DV) == (np.arange(GLA_K)[None, :] // GLA_DK)).astype(np.float32)
    return mst, hmx, ym, pm, bd


def _gla_kernel(bb, q_ref, k_ref, v_ref, g_ref, mst_ref, hmx_ref, ym_ref, pm_ref, bd_ref, s0_ref,
                o_ref, sfin_ref, s_scr):
    c_idx = pl.program_id(2)

    @pl.when(c_idx == 0)
    def _():
        s_scr[...] = s0_ref[:, 0]

    c = GLA_C
    nl = len(GLA_LEVELS)
    m = mst_ref[0]
    seqs = range(bb)

    def stack(xb, li):
        return jnp.concatenate([xb] * GLA_HEADS, axis=0) * hmx_ref[0, li]

    r_all = []
    for b in seqs:
        g = g_ref[b]
        g1 = g.astype(BF16)
        g2 = (g - g1.astype(F32)).astype(BF16)
        r_all.append(_dot(m, g1) + _dot(m, g2))
    cum = [r[0:c] for r in r_all]
    tot = [r[c * (nl + 1):c * (nl + 1) + 1] for r in r_all]
    q = [q_ref[b].astype(F32) * (GLA_DK ** -0.5) for b in seqs]
    k = [k_ref[b].astype(F32) for b in seqs]

    o_inter = []
    for b in seqs:
        state = s_scr[b]
        o_inter.append(_dot_nt((q[b] * jnp.exp(cum[b])).astype(BF16), state.astype(BF16)))
        k_dec = (k[b] * jnp.exp(tot[b] - cum[b])).astype(BF16)
        s_new = state * jnp.exp(tot[b]) + _dot_tn(v_ref[b], k_dec) * bd_ref[...]
        s_scr[b] = s_new
        sfin_ref[b, 0] = s_new

    att = [pm_ref[0, 0] * _dot_nt(stack(q[b].astype(BF16), 0), k_ref[b]) for b in seqs]
    for li in range(nl):
        for b in seqs:
            ref = r_all[b][c * (li + 1):c * (li + 2)]
            e = jnp.exp(-jnp.abs(cum[b] - ref))
            xs = stack((q[b] * e).astype(BF16), li + 1)
            ys = (k[b] * e).astype(BF16) * ym_ref[0, li]
            att[b] = att[b] + pm_ref[0, li + 1] * _dot_nt(xs, ys)
    for b in seqs:
        v = v_ref[b]
        outs = []
        for h in range(GLA_HEADS):
            a_h = att[b][h * c:(h + 1) * c].astype(BF16)
            outs.append(_dot(a_h, v[:, h * GLA_DV:(h + 1) * GLA_DV]))
        o_ref[0, b] = o_inter[b] + jnp.concatenate(outs, axis=1)


def _gla(proj3, g3, s0, tables, nb, t, bb):
    mst, hmx, ym, pm, bd = tables
    nc = t // GLA_C
    nl = len(GLA_LEVELS)

    def chunk(d, c):
        return jnp.where(d == 0, c, nc - 1 - c)

    return pl.pallas_call(
        functools.partial(_gla_kernel, bb),
        grid=(nb // bb, 2, nc),
        in_specs=[
            pl.BlockSpec((bb, GLA_C, GLA_K), lambda b, d, c: (b, chunk(d, c), COL_QG // GLA_K)),
            pl.BlockSpec((bb, GLA_C, GLA_K), lambda b, d, c: (b, chunk(d, c), COL_KG // GLA_K)),
            pl.BlockSpec((bb, GLA_C, GLA_V), lambda b, d, c: (b, chunk(d, c), COL_VG // GLA_V)),
            pl.BlockSpec((bb, GLA_C, GLA_K), lambda b, d, c: (b, chunk(d, c), d)),
            pl.BlockSpec((1, GLA_MROWS, GLA_C), lambda b, d, c: (d, 0, 0)),
            pl.BlockSpec((1, nl + 1, GLA_HEADS * GLA_C, GLA_K), lambda b, d, c: (d, 0, 0, 0)),
            pl.BlockSpec((1, nl, GLA_C, GLA_K), lambda b, d, c: (d, 0, 0, 0)),
            pl.BlockSpec((1, nl + 1, GLA_HEADS * GLA_C, GLA_C), lambda b, d, c: (d, 0, 0, 0)),
            pl.BlockSpec((GLA_V, GLA_K), lambda b, d, c: (0, 0)),
            pl.BlockSpec((bb, 1, GLA_V, GLA_K), lambda b, d, c: (b, d, 0, 0)),
        ],
        out_specs=[
            pl.BlockSpec((1, bb, GLA_C, GLA_V), lambda b, d, c: (d, b, chunk(d, c), 0)),
            pl.BlockSpec((bb, 1, GLA_V, GLA_K), lambda b, d, c: (b, d, 0, 0)),
        ],
        out_shape=[
            jax.ShapeDtypeStruct((2, nb, t, GLA_V), F32),
            jax.ShapeDtypeStruct((nb, 2, GLA_V, GLA_K), F32),
        ],
        scratch_shapes=[pltpu.VMEM((bb, GLA_V, GLA_K), F32)],
        compiler_params=_params(("arbitrary", "arbitrary", "arbitrary")),
        name="gla",
    )(proj3, proj3, proj3, g3, mst, hmx, ym, pm, bd, s0)


def _fourier_kernel(t, u_ref, bcs_ref, ct_ref, o_ref, y_scr):
    i = pl.program_id(1)

    @pl.when(i == 0)
    def _():
        rows = min(t, 512)
        for r in range(t // rows):
            y = _dot(u_ref[0, r * rows:(r + 1) * rows, :], bcs_ref[...])
            y_scr[r * rows:(r + 1) * rows, :] = y[:, :FOU_W].astype(BF16)
            y_scr[t + r * rows:t + (r + 1) * rows, :] = y[:, FOU_W:].astype(BF16)

    o_ref[0] = _dot(ct_ref[...], y_scr[...]).astype(BF16)


def _fourier(proj3, bcs, ct, nb, t, tm):
    return pl.pallas_call(
        functools.partial(_fourier_kernel, t),
        grid=(nb, t // tm),
        in_specs=[
            pl.BlockSpec((1, t, FOU_W), lambda b, i: (b, 0, COL_UF // FOU_W)),
            pl.BlockSpec((FOU_W, 2 * FOU_W), lambda b, i: (0, 0)),
            pl.BlockSpec((tm, 2 * t), lambda b, i: (i, 0)),
        ],
        out_specs=pl.BlockSpec((1, tm, FOU_W), lambda b, i: (b, i, 0)),
        out_shape=jax.ShapeDtypeStruct((nb, t, FOU_W), BF16),
        scratch_shapes=[pltpu.VMEM((2 * t, FOU_W), BF16)],
        compiler_params=_params(("arbitrary", "arbitrary")),
        name="fourier_mix",
    )(proj3, bcs, ct)


def _dft_tables(t):
    lo = min(t, 64)
    hi = t // lo
    f = jnp.arange(t, dtype=jnp.int32)[None, :]
    unit = 2.0 * math.pi / t
    a = ((lo * jnp.arange(hi, dtype=jnp.int32)[:, None] * f) % t).astype(F32) * unit
    b = ((jnp.arange(lo, dtype=jnp.int32)[:, None] * f) % t).astype(F32) * unit
    ca, sa = jnp.cos(a)[:, None, :], jnp.sin(a)[:, None, :]
    cb, sb = jnp.cos(b)[None, :, :], jnp.sin(b)[None, :, :]
    scale = t ** -0.5
    cos = ((ca * cb - sa * sb) * scale).reshape(t, t)
    msin = ((sa * cb + ca * sb) * -scale).reshape(t, t)
    return jnp.concatenate([cos, msin], axis=1).astype(BF16)


def _channel_dft_table():
    n = np.arange(FOU_GROUP_W)
    ang = 2.0 * np.pi * ((n[:, None] * n[None, :]) % FOU_GROUP_W) / FOU_GROUP_W
    eye = np.eye(FOU_GROUPS)
    bc = np.kron(eye, np.cos(ang)) * FOU_GROUP_W ** -0.5
    bs = np.kron(eye, np.sin(ang)) * FOU_GROUP_W ** -0.5
    return np.concatenate([bc, bs], axis=1).astype(np.float32)


def _merge_kernel(x_ref, g1_ref, fm_ref, of_ref, ob_ref, r_ref, od_ref, gt_ref,
                  wf_ref, wg_ref, wd_ref, wo_ref, gn_ref, grp_ref, o_ref):
    y_f = _dot(fm_ref[...], wf_ref[...])
    og = of_ref[0] + ob_ref[0]
    ms = _dot((og * og).astype(BF16), grp_ref[...])
    r = r_ref[...].astype(F32)
    og = og * lax.rsqrt(ms + EPS) * gn_ref[...] * (r * jax.nn.sigmoid(r))
    y_g = _dot(og.astype(BF16), wg_ref[...])
    y_d = _dot(od_ref[...], wd_ref[...])
    gates = jax.nn.sigmoid(gt_ref[...].astype(F32))
    merged = (gates[:, 0:D_MODEL] * y_f + gates[:, D_MODEL:2 * D_MODEL] * y_g
              + gates[:, 2 * D_MODEL:3 * D_MODEL] * y_d)
    o_ref[...] = x_ref[...] + g1_ref[0] * _dot(merged.astype(BF16), wo_ref[...])


def _merge(x, g1, proj, fm, o_gla, od, w_fou, w_gla_o, w_diff_o, w_out, gn, grp, tokens_per_cond, tm):
    m = x.shape[0]
    per = tokens_per_cond // tm
    ncond = g1.shape[0]
    cidx = (lambda i: (i // per, 0, 0)) if ncond > 1 else (lambda i: (0, 0, 0))
    half = pl.BlockSpec((FOU_W, D_MODEL), lambda i: (0, 0))
    return pl.pallas_call(
        _merge_kernel,
        grid=(m // tm,),
        in_specs=[
            pl.BlockSpec((tm, D_MODEL), lambda i: (i, 0)),
            pl.BlockSpec((1, 1, D_MODEL), cidx),
            pl.BlockSpec((tm, FOU_W), lambda i: (i, 0)),
            pl.BlockSpec((1, tm, GLA_V), lambda i: (0, i, 0)),
            pl.BlockSpec((1, tm, GLA_V), lambda i: (1, i, 0)),
            pl.BlockSpec((tm, GLA_V), lambda i: (i, COL_RG // GLA_V)),
            pl.BlockSpec((tm, DIFF_V), lambda i: (i, 0)),
            pl.BlockSpec((tm, N_BRANCH * D_MODEL), lambda i: (i, COL_GATES)),
            half, half, half,
            pl.BlockSpec((D_MODEL, D_MODEL), lambda i: (0, 0)),
            pl.BlockSpec((1, GLA_V), lambda i: (0, 0)),
            pl.BlockSpec((GLA_V, GLA_V), lambda i: (0, 0)),
        ],
        out_specs=pl.BlockSpec((tm, D_MODEL), lambda i: (i, 0)),
        out_shape=jax.ShapeDtypeStruct((m, D_MODEL), F32),
        compiler_params=_params(("arbitrary",)),
        name="merge",
    )(x, g1, fm, o_gla, o_gla, proj, od, proj, w_fou, w_gla_o, w_diff_o, w_out, gn, grp)


def _ffn_kernel(tf, x_ref, sc_ref, sh_ref, g2_ref, gain_ref, wg_ref, wu_ref, wd_ref, o_ref, h_scr):
    x = x_ref[...]
    ms = jnp.mean(x * x, axis=-1, keepdims=True)
    scale = gain_ref[...] * (1.0 + sc_ref[0])
    h_scr[...] = (x * lax.rsqrt(ms + EPS) * scale + sh_ref[0]).astype(BF16)
    acc = None
    for f in range(D_FF // tf):
        cols = slice(f * tf, (f + 1) * tf)
        gate = _dot(h_scr[...], wg_ref[:, cols])
        up = _dot(h_scr[...], wu_ref[:, cols])
        a = (gate * jax.nn.sigmoid(gate) * up).astype(BF16)
        part = _dot(a, wd_ref[cols, :])
        acc = part if acc is None else acc + part
    o_ref[...] = x_ref[...] + g2_ref[0] * acc


def _ffn(x, sc, sh, g2, gain, w_gate, w_up, w_down, tokens_per_cond, tm):
    m = x.shape[0]
    tf = 256
    per = tokens_per_cond // tm
    ncond = sc.shape[0]
    cidx = (lambda i: (i // per, 0, 0)) if ncond > 1 else (lambda i: (0, 0, 0))
    const = lambda i: (0, 0)
    once = pl.Buffered(1)
    return pl.pallas_call(
        functools.partial(_ffn_kernel, tf),
        grid=(m // tm,),
        in_specs=[
            pl.BlockSpec((tm, D_MODEL), lambda i: (i, 0)),
            pl.BlockSpec((1, 1, D_MODEL), cidx),
            pl.BlockSpec((1, 1, D_MODEL), cidx),
            pl.BlockSpec((1, 1, D_MODEL), cidx),
            pl.BlockSpec((1, D_MODEL), const),
            pl.BlockSpec((D_MODEL, D_FF), const, pipeline_mode=once),
            pl.BlockSpec((D_MODEL, D_FF), const, pipeline_mode=once),
            pl.BlockSpec((D_FF, D_MODEL), const, pipeline_mode=once),
        ],
        out_specs=pl.BlockSpec((tm, D_MODEL), lambda i: (i, 0)),
        out_shape=jax.ShapeDtypeStruct((m, D_MODEL), F32),
        scratch_shapes=[pltpu.VMEM((tm, D_MODEL), BF16)],
        compiler_params=_params(("arbitrary",)),
        name="ffn",
    )(x, sc, sh, g2, gain, w_gate, w_up, w_down)


def _group_mean_matrix(width, group):
    idx = np.arange(width) // group
    return (idx[:, None] == idx[None, :]).astype(np.float32) / group


def _rotate_half_matrix():
    r = np.zeros((DIFF_HEAD_DIM, DIFF_HEAD_DIM), np.float32)
    for axis in range(2):
        base = axis * ROPE_AXIS_DIM
        for f in range(ROPE_FREQS):
            r[base + ROPE_FREQS + f, base + f] = -1.0
            r[base + f, base + ROPE_FREQS + f] = 1.0
    return np.kron(np.eye(DIFF_QK // DIFF_HEAD_DIM), r).astype(np.float32)


def _rotate_half_perm():
    p = np.zeros((DIFF_HEAD_DIM,), np.int32)
    for axis in range(2):
        base = axis * ROPE_AXIS_DIM
        for f in range(ROPE_FREQS):
            p[base + f] = base + ROPE_FREQS + f
            p[base + ROPE_FREQS + f] = base + f
    return p


def _rope_tables(n_tokens):
    rows = n_tokens // GRID_W
    row = jnp.repeat(jnp.arange(rows), GRID_W).astype(F32)
    col = jnp.tile(jnp.arange(GRID_W), rows).astype(F32)
    inv = ROPE_BASE ** (-jnp.arange(ROPE_FREQS, dtype=F32) * 2.0 / ROPE_AXIS_DIM)
    ang_r = row[:, None] * inv
    ang_c = col[:, None] * inv
    ang = jnp.concatenate([ang_r, ang_r, ang_c, ang_c], axis=-1)
    reps = DIFF_QK // DIFF_HEAD_DIM
    return jnp.tile(jnp.cos(ang), (1, reps)), jnp.tile(jnp.sin(ang), (1, reps))


def _block_diag_state(s):
    eye = jnp.eye(GLA_HEADS, dtype=s.dtype)
    full = jnp.einsum('bdhkv,hg->bdhvgk', s, eye)
    return full.reshape(s.shape[0], 2, GLA_V, GLA_K)


def _unblock_state(sb):
    nb = sb.shape[0]
    s = sb.reshape(nb, 2, GLA_HEADS, GLA_DV, GLA_HEADS, GLA_DK)
    diag = jnp.stack([s[:, :, h, :, h, :] for h in range(GLA_HEADS)], axis=2)
    return jnp.swapaxes(diag, -1, -2)


def _run_path(x, mods, layers, consts, nb, t, ctx):
    m = nb * t
    tm = 512 if t >= 512 else t
    x = x.reshape(m, D_MODEL)
    is_latent = ctx is not None
    cos, sin = consts['rope'] if is_latent else (consts['ones'], consts['ones'])
    ct = consts['ct_lat'] if is_latent else consts['ct_ctx']
    k_out, v_out, s_out = [], [], []
    for l, p in enumerate(layers):
        sh1, sc1, g1, sh2, sc2, g2 = [mods[l][:, None, j * D_MODEL:(j + 1) * D_MODEL] for j in range(6)]
        proj, g = _in_projection(x, sc1, sh1, p['norm1'], p['w_main'], p['w_a'], p['w2'], p['b2'], t, tm)
        proj3 = proj.reshape(nb, t, PROJ_W)

        prep = _qk_prepare(proj, p['qk_gains'], consts['grp64'], consts['rot'], cos, sin,
                           nb, t, t, is_latent, not is_latent, tm)
        qn, kn = prep[0], prep[1]
        lam_init = 0.8 - 0.6 * math.exp(-0.3 * l)
        segments = [(kn, 0, proj3, COL_VD // DIFF_V_HEAD, t)]
        if is_latent:
            ck = ctx[1][:, l].reshape(nb, -1, DIFF_QK).astype(BF16)
            cv = ctx[2][:, l].reshape(nb, -1, DIFF_V).astype(BF16)
            segments.append((ck, 0, cv, 0, ck.shape[1]))
        od = _diff_attention(qn.reshape(nb, t, DIFF_QK), segments, p['diff_lambda'], p['diff_norm'],
                             lam_init, nb, t, min(t, 256))

        if is_latent:
            s0 = _block_diag_state(ctx[0][:, l])
        else:
            s0 = jnp.zeros((nb, 2, GLA_V, GLA_K), F32)
        o_gla, s_fin = _gla(proj3, g.reshape(nb, t, 2 * GLA_K), s0, consts['gla'], nb, t, math.gcd(nb, GLA_BB))

        fm = _fourier(proj3, consts['bcs'], ct, nb, t, min(t, 256))

        x = _merge(x, g1, proj, fm.reshape(m, FOU_W), o_gla.reshape(2, m, GLA_V), od.reshape(m, DIFF_V),
                   p['w_fou'], p['w_gla_o'], p['w_diff_o'], p['w_out'], p['gla_norm'], consts['grp128'],
                   t, min(tm, 256))
        x = _ffn(x, sc2, sh2, g2, p['norm2'], p['w_ff_gate'], p['w_ff_up'], p['w_ff_down'], t, tm)

        if not is_latent:
            k_out.append(prep[2].reshape(nb, t, DIFF_HEADS, 2, DIFF_HEAD_DIM))
            v_out.append(proj3[:, :, COL_VD:COL_VD + DIFF_V].astype(F32).reshape(nb, t, DIFF_HEADS, DIFF_V_HEAD))
            s_out.append(_unblock_state(s_fin))
    return x.reshape(nb, t, D_MODEL), k_out, v_out, s_out


def kernel(x_prompt, x_sample, c, cache_diff_k, cache_diff_v, state_gla, c_ctx, w_mod, b_mod, norm1, norm2, w_in, w_gla_a2, b_gla_a, gla_norm, diff_qk_norm, diff_lambda, diff_norm, w_fou, w_gla_o, w_diff_o, w_out, w_ff_gate, w_ff_up, w_ff_down):
    nb_ctx, t_ctx, _ = x_prompt.shape
    nb_lat, t_lat, _ = x_sample.shape

    n_cond = 1 + nb_lat
    r_pad = -(-n_cond // 8) * 8
    cond = jnp.zeros((r_pad, D_MODEL), F32).at[0].set(c_ctx).at[1:n_cond].set(c)
    mods = _modulation(cond, w_mod, b_mod)
    mods_ctx = [mods[l, 0:1] for l in range(DEPTH)]
    mods_lat = [mods[l, 1:n_cond] for l in range(DEPTH)]

    a0 = 2048
    a1 = a0 + 2 * GATE_RANK
    pieces = {
        'u_f': (0, 512), 'q_g': (512, 768), 'k_g': (768, 1024), 'v_g': (1024, 1536), 'r_g': (1536, 2048),
        'q_d': (a1, a1 + 512), 'k_d': (a1 + 512, a1 + 1024), 'v_d': (a1 + 1024, a1 + 1536),
        'gates': (a1 + 1536, a1 + 1536 + 3072),
    }
    order = ['gates', 'u_f', 'v_g', 'r_g', 'q_d', 'k_d', 'v_d', 'q_g', 'k_g']
    w_main = jnp.concatenate([w_in[:, :, pieces[n][0]:pieces[n][1]] for n in order], axis=-1).astype(BF16)
    w_a = jnp.pad(w_in[:, :, a0:a1], ((0, 0), (0, 0), (0, A_PAD - 2 * GATE_RANK))).astype(BF16)
    w2 = jnp.zeros((DEPTH, A_PAD, 2 * GLA_K), F32)
    w2 = w2.at[:, 0:GATE_RANK, 0:GLA_K].set(w_gla_a2[:, 0])
    w2 = w2.at[:, GATE_RANK:2 * GATE_RANK, GLA_K:].set(w_gla_a2[:, 1])
    b2 = b_gla_a.reshape(DEPTH, 1, 2 * GLA_K)

    perm = _rotate_half_perm()
    reps = DIFF_QK // DIFF_HEAD_DIM
    layers = []
    for l in range(DEPTH):
        gq = diff_qk_norm[l, 0]
        gk = diff_qk_norm[l, 1]
        layers.append({
            'norm1': norm1[l][None, :], 'norm2': norm2[l][None, :],
            'w_main': w_main[l], 'w_a': w_a[l], 'w2': w2[l], 'b2': b2[l],
            'qk_gains': tuple(jnp.tile(v, reps)[None, :] for v in (gq, gk, gq[perm], gk[perm])),
            'diff_lambda': diff_lambda[l], 'diff_norm': diff_norm[l][None, :],
            'gla_norm': jnp.tile(gla_norm[l], GLA_HEADS)[None, :],
            'w_fou': w_fou[l].astype(BF16), 'w_gla_o': w_gla_o[l].astype(BF16),
            'w_diff_o': w_diff_o[l].astype(BF16), 'w_out': w_out[l].astype(BF16),
            'w_ff_gate': w_ff_gate[l].astype(BF16), 'w_ff_up': w_ff_up[l].astype(BF16),
            'w_ff_down': w_ff_down[l].astype(BF16),
        })

    mst, hmx, ym, pm, bd = _gla_tables()
    consts = {
        'grp64': jnp.asarray(_group_mean_matrix(DIFF_QK, DIFF_HEAD_DIM), BF16),
        'grp128': jnp.asarray(_group_mean_matrix(GLA_V, GLA_DV), BF16),
        'rot': jnp.asarray(_rotate_half_matrix(), BF16),
        'rope': _rope_tables(t_lat),
        'ones': jnp.ones((t_ctx, DIFF_QK), F32),
        'bcs': jnp.asarray(_channel_dft_table(), BF16),
        'ct_ctx': _dft_tables(t_ctx),
        'ct_lat': _dft_tables(t_lat),
        'gla': (jnp.asarray(mst, BF16), jnp.asarray(hmx, BF16), jnp.asarray(ym, BF16), jnp.asarray(pm),
                jnp.asarray(bd)),
    }

    y_prompt, k_list, v_list, s_list = _run_path(x_prompt, mods_ctx, layers, consts, nb_ctx, t_ctx, None)
    y_sample, _, _, _ = _run_path(x_sample, mods_lat, layers, consts, nb_lat, t_lat,
                                  (state_gla, cache_diff_k, cache_diff_v))
    return (y_prompt, y_sample, jnp.stack(k_list, axis=1), jnp.stack(v_list, axis=1),
            jnp.stack(s_list, axis=1))
```

```python
import functools
import math

import numpy as np
import jax
import jax.numpy as jnp
from jax import lax
from jax.experimental import pallas as pl
from jax.experimental.pallas import tpu as pltpu

F32 = jnp.float32
BF16 = jnp.bfloat16

D_MODEL = 1024
DEPTH = 4
GRID_W = 64
FOU_GROUPS = 4
FOU_GROUP_W = 128
FOU_W = 512
GLA_HEADS = 4
GLA_DK = 64
GLA_DV = 128
GLA_K = 256
GLA_V = 512
GATE_RANK = 16
GATE_TEMP = 16.0
DIFF_HEADS = 4
DIFF_HEAD_DIM = 64
DIFF_V_HEAD = 128
DIFF_QK = 512
DIFF_V = 512
ROPE_AXIS_DIM = 32
ROPE_FREQS = 16
ROPE_BASE = 10000.0
N_BRANCH = 3
D_FF = 2816
EPS = 1e-6

COL_GATES = 0
COL_UF = 3072
COL_VG = 3584
COL_RG = 4096
COL_VD = 4608
COL_QG = 5120
COL_KG = 5376
PROJ_W = 5632
W_COLS = PROJ_W + 2 * 512
A_PAD = 128

GLA_C = 64
GLA_LEVELS = (32, 16, 8, 4, 2, 1)
GLA_MROWS = 64 * (1 + len(GLA_LEVELS)) + 8
GLA_BB = 8
VMEM_LIMIT = 48 * 1024 * 1024
VMEM_LIMIT_BIG = 56 * 1024 * 1024
LOG2_E = math.log2(math.e)


def _params(sem, vmem=VMEM_LIMIT):
    return pltpu.CompilerParams(dimension_semantics=sem, vmem_limit_bytes=vmem)


def _dot(a, b):
    return jnp.dot(a, b, preferred_element_type=F32)


def _dot_nt(a, b):
    return lax.dot_general(a, b, (((1,), (1,)), ((), ())), preferred_element_type=F32)


def _dot_tn(a, b):
    return lax.dot_general(a, b, (((0,), (0,)), ((), ())), preferred_element_type=F32)


def _split3(x):
    x1 = x.astype(BF16)
    r1 = x - x1.astype(F32)
    x2 = r1.astype(BF16)
    x3 = (r1 - x2.astype(F32)).astype(BF16)
    return x1, x2, x3


def _mod_kernel(c_ref, w_ref, b_ref, o_ref):
    c = c_ref[...]
    s = c * jax.nn.sigmoid(c)
    s1, s2, s3 = _split3(s)
    w = w_ref[0]
    w1, w2, w3 = _split3(w)
    acc = _dot(s1, w1) + (_dot(s1, w2) + _dot(s2, w1)) + (_dot(s2, w2) + _dot(s1, w3) + _dot(s3, w1))
    o_ref[0] = acc + b_ref[0]


def _modulation(cond, w_mod, b_mod):
    r = cond.shape[0]
    tn = 768
    n = 6 * D_MODEL
    return pl.pallas_call(
        _mod_kernel,
        grid=(DEPTH, n // tn),
        in_specs=[
            pl.BlockSpec((r, D_MODEL), lambda l, j: (0, 0)),
            pl.BlockSpec((1, D_MODEL, tn), lambda l, j: (l, 0, j)),
            pl.BlockSpec((1, 1, tn), lambda l, j: (l, 0, j)),
        ],
        out_specs=pl.BlockSpec((1, r, tn), lambda l, j: (l, 0, j)),
        out_shape=jax.ShapeDtypeStruct((DEPTH, r, n), F32),
        compiler_params=_params(("arbitrary", "arbitrary")),
        name="modulation",
    )(cond, w_mod, b_mod.reshape(DEPTH, 1, n))


def _inproj_kernel(tn, use_rope, want_f32, *refs):
    (x_ref, sc_ref, sh_ref, gain_ref, w_ref, wa_ref, w2_ref, b2_ref,
     gq_ref, gk_ref, gqp_ref, gkp_ref, grp_ref, rot_ref, cos_ref, sin_ref) = refs[:16]
    proj_ref, g_ref, qn_ref, kn_ref = refs[16:20]
    kf_ref = refs[20] if want_f32 else None
    h_scr = refs[-1]

    x = x_ref[...]
    ms = jnp.mean(x * x, axis=-1, keepdims=True)
    scale = gain_ref[...] * (1.0 + sc_ref[0])
    hb = (x * lax.rsqrt(ms + EPS) * scale + sh_ref[0]).astype(BF16)
    h_scr[...] = hb
    a = _dot(hb, wa_ref[...])
    a1, a2, a3 = _split3(a)
    v1, v2, v3 = _split3(w2_ref[...])
    z = (_dot(a1, v1) + (_dot(a1, v2) + _dot(a2, v1))
         + (_dot(a2, v2) + _dot(a1, v3) + _dot(a3, v1))) + b2_ref[...]
    logsig = jnp.minimum(z, 0.0) - jnp.log1p(jnp.exp(-jnp.abs(z)))
    g_ref[...] = logsig * (1.0 / GATE_TEMP)
    def prep(col, g_ref_, gp_ref_, out_scale):
        y = _dot(h_scr[...], w_ref[:, col:col + DIFF_QK])
        msq = _dot((y * y).astype(BF16), grp_ref[...])
        r = lax.rsqrt(msq + EPS)
        out = y * r * g_ref_[...]
        if use_rope:
            y1 = y.astype(BF16)
            y2 = (y - y1.astype(F32)).astype(BF16)
            rot = _dot(y1, rot_ref[...]) + _dot(y2, rot_ref[...])
            out = out * cos_ref[...] + rot * r * gp_ref_[...] * sin_ref[...]
        return out * out_scale

    n_tiles = PROJ_W // tn
    for j in range(n_tiles):
        proj_ref[:, j * tn:(j + 1) * tn] = _dot(h_scr[...], w_ref[:, j * tn:(j + 1) * tn]).astype(BF16)
        if j == n_tiles // 4:
            qn_ref[...] = prep(PROJ_W, gq_ref, gqp_ref, DIFF_HEAD_DIM ** -0.5 * LOG2_E).astype(BF16)
        if j == (3 * n_tiles) // 4:
            kn = prep(PROJ_W + DIFF_QK, gk_ref, gkp_ref, 1.0)
            kn_ref[0] = kn.astype(BF16)
            if want_f32:
                kf_ref[...] = kn


def _in_projection(x, sc, sh, gain, w_main, w_a, w2, b2, qk_gains, grp, rot, cos, sin,
                   nb, t, use_rope, want_f32, tm):
    m = x.shape[0]
    tn = 512
    per = t // tm
    ncond = sc.shape[0]
    cidx = (lambda i: (i // per, 0, 0)) if ncond > 1 else (lambda i: (0, 0, 0))
    const = lambda i: (0, 0)
    once = pl.Buffered(1)
    vec = pl.BlockSpec((1, DIFF_QK), const)
    mat = pl.BlockSpec((DIFF_QK, DIFF_QK), const, pipeline_mode=once)
    tab = pl.BlockSpec((tm, DIFF_QK), lambda i: (i % per, 0))
    out_specs = [
        pl.BlockSpec((tm, PROJ_W), lambda i: (i, 0)),
        pl.BlockSpec((tm, 2 * GLA_K), lambda i: (i, 0)),
        pl.BlockSpec((tm, DIFF_QK), lambda i: (i, 0)),
        pl.BlockSpec((1, tm, DIFF_QK), lambda i: (i // per, i % per, 0)),
    ]
    out_shape = [
        jax.ShapeDtypeStruct((m, PROJ_W), BF16),
        jax.ShapeDtypeStruct((m, 2 * GLA_K), F32),
        jax.ShapeDtypeStruct((m, DIFF_QK), BF16),
        jax.ShapeDtypeStruct((nb, t, DIFF_QK), BF16),
    ]
    if want_f32:
        out_specs.append(pl.BlockSpec((tm, DIFF_QK), lambda i: (i, 0)))
        out_shape.append(jax.ShapeDtypeStruct((m, DIFF_QK), F32))
    gq, gk, gqp, gkp = qk_gains
    return pl.pallas_call(
        functools.partial(_inproj_kernel, tn, use_rope, want_f32),
        grid=(m // tm,),
        in_specs=[
            pl.BlockSpec((tm, D_MODEL), lambda i: (i, 0)),
            pl.BlockSpec((1, 1, D_MODEL), cidx),
            pl.BlockSpec((1, 1, D_MODEL), cidx),
            pl.BlockSpec((1, D_MODEL), const),
            pl.BlockSpec((D_MODEL, W_COLS), const, pipeline_mode=once),
            pl.BlockSpec((D_MODEL, A_PAD), const, pipeline_mode=once),
            pl.BlockSpec((A_PAD, 2 * GLA_K), const, pipeline_mode=once),
            pl.BlockSpec((1, 2 * GLA_K), const),
            vec, vec, vec, vec, mat, mat, tab, tab,
        ],
        out_specs=out_specs,
        out_shape=out_shape,
        scratch_shapes=[pltpu.VMEM((tm, D_MODEL), BF16)],
        compiler_params=_params(("arbitrary",), VMEM_LIMIT_BIG),
        name="in_projection",
    )(x, sc, sh, gain, w_main, w_a, w2, b2, gq, gk, gqp, gkp, grp, rot, cos, sin)


def _diffattn_kernel(seg_len, lam_init, *refs):
    n_seg = len(seg_len)
    q_ref = refs[0]
    kv_refs = refs[1:1 + 2 * n_seg]
    lam_ref, gain_ref, o_ref = refs[1 + 2 * n_seg:4 + 2 * n_seg]
    kz0, kz1, vaug, s_a, s_b, m_a, m_b = refs[4 + 2 * n_seg:]
    i = pl.program_id(2)

    @pl.when(i == 0)
    def _():
        off = 0
        for s, tk in enumerate(seg_len):
            k = kv_refs[2 * s][0]
            lane = lax.broadcasted_iota(jnp.int32, k.shape, 1)
            zero = jnp.zeros_like(k)
            kz0[off:off + tk, :] = jnp.where(lane < DIFF_HEAD_DIM, k, zero)
            kz1[off:off + tk, :] = jnp.where(lane >= DIFF_HEAD_DIM, k, zero)
            v = kv_refs[2 * s + 1][0]
            vaug[off:off + tk, 0:DIFF_V_HEAD] = v
            vaug[off:off + tk, DIFF_V_HEAD:] = jnp.ones_like(v)
            off += tk

    @pl.when((pl.program_id(0) == 0) & (pl.program_id(1) == 0) & (i == 0))
    def _():
        s_b[...] = jnp.zeros_like(s_b)
        m_b[...] = jnp.zeros_like(m_b)

    lp = lam_ref[...]
    lam = (jnp.exp(jnp.sum(lp[0:1] * lp[1:2], axis=-1, keepdims=True))
           - jnp.exp(jnp.sum(lp[2:3] * lp[3:4], axis=-1, keepdims=True)) + lam_init)

    def score(w, kz, s_write, m_write):
        sc = _dot_nt(q_ref[0], kz[...])
        s_write[w] = sc
        m_write[w] = jnp.broadcast_to(sc.max(axis=-1, keepdims=True), m_write.shape[1:])

    def finish(w, s_read, m_read):
        p = jnp.exp2(s_read[w] - m_read[w][:, 0:1]).astype(BF16)
        acc = _dot(p, vaug[...])
        return acc[:, 0:DIFF_V_HEAD] / acc[:, DIFF_V_HEAD:]

    def step(s_write, m_write, s_read, m_read):
        score(0, kz0, s_write, m_write)
        score(1, kz1, s_write, m_write)
        o = finish(0, s_read, m_read) - lam * finish(1, s_read, m_read)
        ms = jnp.mean(o * o, axis=-1, keepdims=True)
        o = o * lax.rsqrt(ms + EPS) * gain_ref[...] * (1.0 - lam_init)
        o_ref[0] = o.astype(BF16)

    @pl.when(i % 2 == 0)
    def _():
        step(s_a, m_a, s_b, m_b)

    @pl.when(i % 2 == 1)
    def _():
        step(s_b, m_b, s_a, m_a)


def _diff_attention(qn, segments, lam_p, gain, lam_init, nb, tq_total, tq):
    hd = 2 * DIFF_HEAD_DIM
    nq = tq_total // tq
    in_specs = [pl.BlockSpec((1, tq, hd), lambda b, h, i: (b, jnp.minimum(i, nq - 1), h))]
    args = [qn]
    seg_len = tuple(seg[4] for seg in segments)
    tk_all = sum(seg_len)
    for (ka, kc, va, vc, tk) in segments:
        in_specs.append(pl.BlockSpec((1, tk, hd), lambda b, h, i, kc=kc: (b, 0, kc + h)))
        in_specs.append(pl.BlockSpec((1, tk, DIFF_V_HEAD), lambda b, h, i, vc=vc: (b, 0, vc + h)))
        args += [ka, va]
    in_specs.append(pl.BlockSpec((4, DIFF_HEAD_DIM), lambda b, h, i: (0, 0)))
    in_specs.append(pl.BlockSpec((1, DIFF_V_HEAD), lambda b, h, i: (0, 0)))
    args += [lam_p, gain]
    scratch = [
        pltpu.VMEM((tk_all, hd), BF16), pltpu.VMEM((tk_all, hd), BF16),
        pltpu.VMEM((tk_all, 2 * DIFF_V_HEAD), BF16),
        pltpu.VMEM((2, tq, tk_all), F32), pltpu.VMEM((2, tq, tk_all), F32),
        pltpu.VMEM((2, tq, hd), F32), pltpu.VMEM((2, tq, hd), F32),
    ]
    return pl.pallas_call(
        functools.partial(_diffattn_kernel, seg_len, lam_init),
        grid=(nb, DIFF_HEADS, nq + 1),
        in_specs=in_specs,
        out_specs=pl.BlockSpec((1, tq, DIFF_V_HEAD), lambda b, h, i: (b, jnp.maximum(i - 1, 0), h)),
        out_shape=jax.ShapeDtypeStruct((nb, tq_total, DIFF_V), BF16),
        scratch_shapes=scratch,
        compiler_params=_params(("arbitrary", "arbitrary", "arbitrary")),
        name="diff_attention",
    )(*args)


def _gla_tables():
    c = GLA_C
    nl = len(GLA_LEVELS)
    mst = np.zeros((2, GLA_MROWS, c), np.float32)
    xm = np.zeros((2, nl + 1, c), np.float32)
    pm = np.zeros((2, nl + 1, c, c), np.float32)
    idx = np.arange(c)
    for d in range(2):
        ip = idx if d == 0 else c - 1 - idx
        mst[d, 0:c] = (ip[None, :] <= ip[:, None])
        mst[d, c * (nl + 1)] = 1.0
        pm[d, 0] = np.eye(c)
        xm[d, 0] = 1.0
        for li, s in enumerate(GLA_LEVELS):
            ref = (ip // (2 * s)) * (2 * s) + s - 1
            mst[d, c * (li + 1):c * (li + 2)] = (ip[None, :] <= ref[:, None])
            odd = (ip // s) % 2 == 1
            xm[d, li + 1] = odd
            same = (ip[:, None] // (2 * s)) == (ip[None, :] // (2 * s))
            pm[d, li + 1] = odd[:, None] & (~odd)[None, :] & same
    pm = np.tile(pm, (1, 1, GLA_HEADS, 1))
    rows = np.arange(GLA_HEADS * c)[:, None] // c
    hm = (rows == (np.arange(GLA_K)[None, :] // GLA_DK)).astype(np.float32)
    hmx = hm[None, None] * np.tile(xm, (1, 1, GLA_HEADS))[:, :, :, None]
    ym = np.broadcast_to((1.0 - xm[:, 1:])[:, :, :, None], (2, nl, c, GLA_K)).copy()
    bd = ((np.arange(GLA_V)[:, None] // GLA_DV) == (np.arange(GLA_K)[None, :] // GLA_DK)).astype(np.float32)
    return mst, hmx, ym, pm, bd


def _gla_kernel(bb, q_ref, k_ref, v_ref, g_ref, mst_ref, hmx_ref, ym_ref, pm_ref, bd_ref, s0_ref,
                o_ref, sfin_ref, s_scr):
    c_idx = pl.program_id(2)

    @pl.when(c_idx == 0)
    def _():
        s_scr[...] = s0_ref[:, 0]

    c = GLA_C
    nl = len(GLA_LEVELS)
    m = mst_ref[0]
    seqs = range(bb)

    def stack(xb, li):
        return jnp.concatenate([xb] * GLA_HEADS, axis=0) * hmx_ref[0, li]

    r_all = []
    for b in seqs:
        g = g_ref[b]
        g1 = g.astype(BF16)
        g2 = (g - g1.astype(F32)).astype(BF16)
        r_all.append(_dot(m, g1) + _dot(m, g2))
    cum = [r[0:c] for r in r_all]
    tot = [r[c * (nl + 1):c * (nl + 1) + 1] for r in r_all]
    q = [q_ref[b].astype(F32) * (GLA_DK ** -0.5) for b in seqs]
    k = [k_ref[b].astype(F32) for b in seqs]

    o_inter = []
    for b in seqs:
        state = s_scr[b]
        o_inter.append(_dot_nt((q[b] * jnp.exp(cum[b])).astype(BF16), state.astype(BF16)))
        k_dec = (k[b] * jnp.exp(tot[b] - cum[b])).astype(BF16)
        s_new = state * jnp.exp(tot[b]) + _dot_tn(v_ref[b], k_dec) * bd_ref[...]
        s_scr[b] = s_new
        sfin_ref[b, 0] = s_new

    att = [pm_ref[0, 0] * _dot_nt(stack(q[b].astype(BF16), 0), k_ref[b]) for b in seqs]
    for li in range(nl):
        for b in seqs:
            ref = r_all[b][c * (li + 1):c * (li + 2)]
            e = jnp.exp(-jnp.abs(cum[b] - ref))
            xs = stack((q[b] * e).astype(BF16), li + 1)
            ys = (k[b] * e).astype(BF16) * ym_ref[0, li]
            att[b] = att[b] + pm_ref[0, li + 1] * _dot_nt(xs, ys)
    for b in seqs:
        v = v_ref[b]
        outs = []
        for h in range(GLA_HEADS):
            a_h = att[b][h * c:(h + 1) * c].astype(BF16)
            outs.append(_dot(a_h, v[:, h * GLA_DV:(h + 1) * GLA_DV]))
        o_ref[0, b] = o_inter[b] + jnp.concatenate(outs, axis=1)


def _gla(proj3, g3, s0, tables, nb, t, bb):
    mst, hmx, ym, pm, bd = tables
    nc = t // GLA_C
    nl = len(GLA_LEVELS)

    def chunk(d, c):
        return jnp.where(d == 0, c, nc - 1 - c)

    return pl.pallas_call(
        functools.partial(_gla_kernel, bb),
        grid=(nb // bb, 2, nc),
        in_specs=[
            pl.BlockSpec((bb, GLA_C, GLA_K), lambda b, d, c: (b, chunk(d, c), COL_QG // GLA_K)),
            pl.BlockSpec((bb, GLA_C, GLA_K), lambda b, d, c: (b, chunk(d, c), COL_KG // GLA_K)),
            pl.BlockSpec((bb, GLA_C, GLA_V), lambda b, d, c: (b, chunk(d, c), COL_VG // GLA_V)),
            pl.BlockSpec((bb, GLA_C, GLA_K), lambda b, d, c: (b, chunk(d, c), d)),
            pl.BlockSpec((1, GLA_MROWS, GLA_C), lambda b, d, c: (d, 0, 0)),
            pl.BlockSpec((1, nl + 1, GLA_HEADS * GLA_C, GLA_K), lambda b, d, c: (d, 0, 0, 0)),
            pl.BlockSpec((1, nl, GLA_C, GLA_K), lambda b, d, c: (d, 0, 0, 0)),
            pl.BlockSpec((1, nl + 1, GLA_HEADS * GLA_C, GLA_C), lambda b, d, c: (d, 0, 0, 0)),
            pl.BlockSpec((GLA_V, GLA_K), lambda b, d, c: (0, 0)),
            pl.BlockSpec((bb, 1, GLA_V, GLA_K), lambda b, d, c: (b, d, 0, 0)),
        ],
        out_specs=[
            pl.BlockSpec((1, bb, GLA_C, GLA_V), lambda b, d, c: (d, b, chunk(d, c), 0)),
            pl.BlockSpec((bb, 1, GLA_V, GLA_K), lambda b, d, c: (b, d, 0, 0)),
        ],
        out_shape=[
            jax.ShapeDtypeStruct((2, nb, t, GLA_V), F32),
            jax.ShapeDtypeStruct((nb, 2, GLA_V, GLA_K), F32),
        ],
        scratch_shapes=[pltpu.VMEM((bb, GLA_V, GLA_K), F32)],
        compiler_params=_params(("arbitrary", "arbitrary", "arbitrary")),
        name="gla",
    )(proj3, proj3, proj3, g3, mst, hmx, ym, pm, bd, s0)


def _fourier_kernel(t, u_ref, bcs_ref, ct_ref, o_ref, y_scr):
    i = pl.program_id(1)

    @pl.when(i == 0)
    def _():
        rows = min(t, 512)
        for r in range(t // rows):
            y = _dot(u_ref[0, r * rows:(r + 1) * rows, :], bcs_ref[...])
            y_scr[r * rows:(r + 1) * rows, :] = y[:, :FOU_W].astype(BF16)
            y_scr[t + r * rows:t + (r + 1) * rows, :] = y[:, FOU_W:].astype(BF16)

    o_ref[0] = _dot(ct_ref[...], y_scr[...]).astype(BF16)


def _fourier(proj3, bcs, ct, nb, t, tm):
    return pl.pallas_call(
        functools.partial(_fourier_kernel, t),
        grid=(nb, t // tm),
        in_specs=[
            pl.BlockSpec((1, t, FOU_W), lambda b, i: (b, 0, COL_UF // FOU_W)),
            pl.BlockSpec((FOU_W, 2 * FOU_W), lambda b, i: (0, 0)),
            pl.BlockSpec((tm, 2 * t), lambda b, i: (i, 0)),
        ],
        out_specs=pl.BlockSpec((1, tm, FOU_W), lambda b, i: (b, i, 0)),
        out_shape=jax.ShapeDtypeStruct((nb, t, FOU_W), BF16),
        scratch_shapes=[pltpu.VMEM((2 * t, FOU_W), BF16)],
        compiler_params=_params(("arbitrary", "arbitrary")),
        name="fourier_mix",
    )(proj3, bcs, ct)


FFT_N = 64
FFT_G = 8
FFT_CW = 256


def _fft_kernel(u_ref, bcs_ref, m1_ref, q2_ref, o_ref, z_scr, a_scr):
    n, gsz, cw = FFT_N, FFT_G, FFT_CW
    rows = n * gsz
    for r in range(n // gsz):
        y = _dot(u_ref[0, r * rows:(r + 1) * rows, :], bcs_ref[...])
        z_scr[gsz * r:gsz * (r + 1)] = y.reshape(gsz, n, 2 * cw)
    for g in range(n // gsz):
        zg = z_scr[:, gsz * g:gsz * (g + 1), :].reshape(rows, 2 * cw)
        rhs = jnp.concatenate([zg[:, :cw], zg[:, cw:]], axis=0).astype(BF16)
        a_scr[g] = _dot(m1_ref[...], rhs).astype(BF16)
    for f in range(n // gsz):
        slabs = [a_scr[g, part * rows + n * f:part * rows + n * (f + 1), :]
                 for part in range(2) for g in range(n // gsz)]
        res = _dot(q2_ref[f], jnp.concatenate(slabs, axis=0))
        o_ref[0, :, gsz * f:gsz * (f + 1), :] = res.reshape(n, gsz, cw)


def _fft_tables():
    n, gsz, t = FFT_N, FFT_G, FFT_N * FFT_N
    k = np.arange(FOU_GROUP_W)
    ang = 2.0 * np.pi * ((k[:, None] * k[None, :]) % FOU_GROUP_W) / FOU_GROUP_W
    eye2 = np.eye(FFT_CW // FOU_GROUP_W)
    bcs = np.concatenate([np.kron(eye2, np.cos(ang)), -np.kron(eye2, np.sin(ang))], axis=1) * FOU_GROUP_W ** -0.5
    i = np.arange(n)
    a1 = 2.0 * np.pi * ((i[:, None] * i[None, :]) % n) / n
    c1 = np.kron(np.cos(a1), np.eye(gsz))
    s1 = np.kron(np.sin(a1), np.eye(gsz))
    m1 = np.block([[c1, s1], [-s1, c1]])
    ngrp = n // gsz
    f = jnp.arange(ngrp, dtype=jnp.int32).reshape(ngrp, 1, 1, 1, 1, 1)
    f2 = jnp.arange(n, dtype=jnp.int32).reshape(1, n, 1, 1, 1, 1)
    a = jnp.arange(gsz, dtype=jnp.int32).reshape(1, 1, gsz, 1, 1, 1)
    g = jnp.arange(ngrp, dtype=jnp.int32).reshape(1, 1, 1, ngrp, 1, 1)
    ap = jnp.arange(gsz, dtype=jnp.int32).reshape(1, 1, 1, 1, gsz, 1)
    j = jnp.arange(gsz, dtype=jnp.int32).reshape(1, 1, 1, 1, 1, gsz)
    freq = gsz * f + a + n * f2
    theta = (((gsz * g + j) * freq) % t).astype(F32) * (2.0 * math.pi / t)
    keep = (a == ap)
    scale = t ** -0.5
    qc = jnp.where(keep, jnp.cos(theta) * scale, 0.0).reshape(ngrp, n * gsz, n * gsz)
    qs = jnp.where(keep, jnp.sin(theta) * scale, 0.0).reshape(ngrp, n * gsz, n * gsz)
    q2 = jnp.concatenate([qc, qs], axis=2).astype(BF16)
    return jnp.asarray(bcs, BF16), jnp.asarray(m1, BF16), q2


def _fourier_fft(proj3, tables, nb):
    bcs, m1, q2 = tables
    n, gsz, cw = FFT_N, FFT_G, FFT_CW
    t = n * n
    once = pl.Buffered(1)
    out = pl.pallas_call(
        _fft_kernel,
        grid=(nb, FOU_W // cw),
        in_specs=[
            pl.BlockSpec((1, t, cw), lambda b, h: (b, 0, COL_UF // cw + h)),
            pl.BlockSpec((cw, 2 * cw), lambda b, h: (0, 0), pipeline_mode=once),
            pl.BlockSpec((2 * n * gsz, 2 * n * gsz), lambda b, h: (0, 0), pipeline_mode=once),
            pl.BlockSpec((n // gsz, n * gsz, 2 * n * gsz), lambda b, h: (0, 0, 0), pipeline_mode=once),
        ],
        out_specs=pl.BlockSpec((1, n, n, cw), lambda b, h: (b, 0, 0, h)),
        out_shape=jax.ShapeDtypeStruct((nb, n, n, FOU_W), F32),
        scratch_shapes=[pltpu.VMEM((n, n, 2 * cw), F32), pltpu.VMEM((n // gsz, 2 * n * gsz, cw), BF16)],
        compiler_params=_params(("arbitrary", "arbitrary")),
        name="fourier_fft",
    )(proj3, bcs, m1, q2)
    return out.reshape(nb, t, FOU_W)


def _dft_tables(t):
    lo = min(t, 64)
    hi = t // lo
    f = jnp.arange(t, dtype=jnp.int32)[None, :]
    unit = 2.0 * math.pi / t
    a = ((lo * jnp.arange(hi, dtype=jnp.int32)[:, None] * f) % t).astype(F32) * unit
    b = ((jnp.arange(lo, dtype=jnp.int32)[:, None] * f) % t).astype(F32) * unit
    ca, sa = jnp.cos(a)[:, None, :], jnp.sin(a)[:, None, :]
    cb, sb = jnp.cos(b)[None, :, :], jnp.sin(b)[None, :, :]
    scale = t ** -0.5
    cos = ((ca * cb - sa * sb) * scale).reshape(t, t)
    msin = ((sa * cb + ca * sb) * -scale).reshape(t, t)
    return jnp.concatenate([cos, msin], axis=1).astype(BF16)


def _channel_dft_table():
    n = np.arange(FOU_GROUP_W)
    ang = 2.0 * np.pi * ((n[:, None] * n[None, :]) % FOU_GROUP_W) / FOU_GROUP_W
    eye = np.eye(FOU_GROUPS)
    bc = np.kron(eye, np.cos(ang)) * FOU_GROUP_W ** -0.5
    bs = np.kron(eye, np.sin(ang)) * FOU_GROUP_W ** -0.5
    return np.concatenate([bc, bs], axis=1).astype(np.float32)


def _merge_kernel(x_ref, g1_ref, fm_ref, of_ref, ob_ref, r_ref, od_ref, gt_ref,
                  wf_ref, wg_ref, wd_ref, wo_ref, gn_ref, grp_ref, o_ref):
    y_f = _dot(fm_ref[...].astype(BF16), wf_ref[...])
    og = of_ref[0] + ob_ref[0]
    ms = _dot((og * og).astype(BF16), grp_ref[...])
    r = r_ref[...].astype(F32)
    og = og * lax.rsqrt(ms + EPS) * gn_ref[...] * (r * jax.nn.sigmoid(r))
    y_g = _dot(og.astype(BF16), wg_ref[...])
    y_d = _dot(od_ref[...], wd_ref[...])
    gates = jax.nn.sigmoid(gt_ref[...].astype(F32))
    merged = (gates[:, 0:D_MODEL] * y_f + gates[:, D_MODEL:2 * D_MODEL] * y_g
              + gates[:, 2 * D_MODEL:3 * D_MODEL] * y_d)
    o_ref[...] = x_ref[...] + g1_ref[0] * _dot(merged.astype(BF16), wo_ref[...])


def _merge(x, g1, proj, fm, o_gla, od, w_fou, w_gla_o, w_diff_o, w_out, gn, grp, tokens_per_cond, tm):
    m = x.shape[0]
    per = tokens_per_cond // tm
    ncond = g1.shape[0]
    cidx = (lambda i: (i // per, 0, 0)) if ncond > 1 else (lambda i: (0, 0, 0))
    once = pl.Buffered(1)
    half = pl.BlockSpec((FOU_W, D_MODEL), lambda i: (0, 0), pipeline_mode=once)
    return pl.pallas_call(
        _merge_kernel,
        grid=(m // tm,),
        in_specs=[
            pl.BlockSpec((tm, D_MODEL), lambda i: (i, 0)),
            pl.BlockSpec((1, 1, D_MODEL), cidx),
            pl.BlockSpec((tm, FOU_W), lambda i: (i, 0)),
            pl.BlockSpec((1, tm, GLA_V), lambda i: (0, i, 0)),
            pl.BlockSpec((1, tm, GLA_V), lambda i: (1, i, 0)),
            pl.BlockSpec((tm, GLA_V), lambda i: (i, COL_RG // GLA_V)),
            pl.BlockSpec((tm, DIFF_V), lambda i: (i, 0)),
            pl.BlockSpec((tm, N_BRANCH * D_MODEL), lambda i: (i, COL_GATES)),
            half, half, half,
            pl.BlockSpec((D_MODEL, D_MODEL), lambda i: (0, 0), pipeline_mode=once),
            pl.BlockSpec((1, GLA_V), lambda i: (0, 0)),
            pl.BlockSpec((GLA_V, GLA_V), lambda i: (0, 0)),
        ],
        out_specs=pl.BlockSpec((tm, D_MODEL), lambda i: (i, 0)),
        out_shape=jax.ShapeDtypeStruct((m, D_MODEL), F32),
        compiler_params=_params(("arbitrary",)),
        name="merge",
    )(x, g1, fm, o_gla, o_gla, proj, od, proj, w_fou, w_gla_o, w_diff_o, w_out, gn, grp)


def _ffn_kernel(tf, x_ref, sc_ref, sh_ref, g2_ref, gain_ref, wg_ref, wu_ref, wd_ref, o_ref, h_scr):
    x = x_ref[...]
    ms = jnp.mean(x * x, axis=-1, keepdims=True)
    scale = gain_ref[...] * (1.0 + sc_ref[0])
    h_scr[...] = (x * lax.rsqrt(ms + EPS) * scale + sh_ref[0]).astype(BF16)
    acc = None
    for f in range(D_FF // tf):
        cols = slice(f * tf, (f + 1) * tf)
        gate = _dot(h_scr[...], wg_ref[:, cols])
        up = _dot(h_scr[...], wu_ref[:, cols])
        a = (gate * jax.nn.sigmoid(gate) * up).astype(BF16)
        part = _dot(a, wd_ref[cols, :])
        acc = part if acc is None else acc + part
    o_ref[...] = x_ref[...] + g2_ref[0] * acc


def _ffn(x, sc, sh, g2, gain, w_gate, w_up, w_down, tokens_per_cond, tm):
    m = x.shape[0]
    tf = 256
    per = tokens_per_cond // tm
    ncond = sc.shape[0]
    cidx = (lambda i: (i // per, 0, 0)) if ncond > 1 else (lambda i: (0, 0, 0))
    const = lambda i: (0, 0)
    once = pl.Buffered(1)
    return pl.pallas_call(
        functools.partial(_ffn_kernel, tf),
        grid=(m // tm,),
        in_specs=[
            pl.BlockSpec((tm, D_MODEL), lambda i: (i, 0)),
            pl.BlockSpec((1, 1, D_MODEL), cidx),
            pl.BlockSpec((1, 1, D_MODEL), cidx),
            pl.BlockSpec((1, 1, D_MODEL), cidx),
            pl.BlockSpec((1, D_MODEL), const),
            pl.BlockSpec((D_MODEL, D_FF), const, pipeline_mode=once),
            pl.BlockSpec((D_MODEL, D_FF), const, pipeline_mode=once),
            pl.BlockSpec((D_FF, D_MODEL), const, pipeline_mode=once),
        ],
        out_specs=pl.BlockSpec((tm, D_MODEL), lambda i: (i, 0)),
        out_shape=jax.ShapeDtypeStruct((m, D_MODEL), F32),
        scratch_shapes=[pltpu.VMEM((tm, D_MODEL), BF16)],
        compiler_params=_params(("arbitrary",)),
        name="ffn",
    )(x, sc, sh, g2, gain, w_gate, w_up, w_down)


def _group_mean_matrix(width, group):
    idx = np.arange(width) // group
    return (idx[:, None] == idx[None, :]).astype(np.float32) / group


def _rotate_half_matrix():
    r = np.zeros((DIFF_HEAD_DIM, DIFF_HEAD_DIM), np.float32)
    for axis in range(2):
        base = axis * ROPE_AXIS_DIM
        for f in range(ROPE_FREQS):
            r[base + ROPE_FREQS + f, base + f] = -1.0
            r[base + f, base + ROPE_FREQS + f] = 1.0
    return np.kron(np.eye(DIFF_QK // DIFF_HEAD_DIM), r).astype(np.float32)


def _rotate_half_perm():
    p = np.zeros((DIFF_HEAD_DIM,), np.int32)
    for axis in range(2):
        base = axis * ROPE_AXIS_DIM
        for f in range(ROPE_FREQS):
            p[base + f] = base + ROPE_FREQS + f
            p[base + ROPE_FREQS + f] = base + f
    return p


def _rope_tables(n_tokens):
    rows = n_tokens // GRID_W
    row = jnp.repeat(jnp.arange(rows), GRID_W).astype(F32)
    col = jnp.tile(jnp.arange(GRID_W), rows).astype(F32)
    inv = ROPE_BASE ** (-jnp.arange(ROPE_FREQS, dtype=F32) * 2.0 / ROPE_AXIS_DIM)
    ang_r = row[:, None] * inv
    ang_c = col[:, None] * inv
    ang = jnp.concatenate([ang_r, ang_r, ang_c, ang_c], axis=-1)
    reps = DIFF_QK // DIFF_HEAD_DIM
    return jnp.tile(jnp.cos(ang), (1, reps)), jnp.tile(jnp.sin(ang), (1, reps))


def _block_diag_state(s):
    eye = jnp.eye(GLA_HEADS, dtype=s.dtype)
    full = jnp.einsum('bdhkv,hg->bdhvgk', s, eye)
    return full.reshape(s.shape[0], 2, GLA_V, GLA_K)


def _unblock_state(sb):
    nb = sb.shape[0]
    s = sb.reshape(nb, 2, GLA_HEADS, GLA_DV, GLA_HEADS, GLA_DK)
    diag = jnp.stack([s[:, :, h, :, h, :] for h in range(GLA_HEADS)], axis=2)
    return jnp.swapaxes(diag, -1, -2)


def _run_path(x, mods, layers, consts, nb, t, ctx):
    m = nb * t
    tm = 512 if t >= 512 else t
    x = x.reshape(m, D_MODEL)
    is_latent = ctx is not None
    cos, sin = consts['rope'] if is_latent else (consts['ones'], consts['ones'])
    ct = consts['ct'][t]
    k_out, v_out, s_out = [], [], []
    for l, p in enumerate(layers):
        sh1, sc1, g1, sh2, sc2, g2 = [mods[l][:, None, j * D_MODEL:(j + 1) * D_MODEL] for j in range(6)]
        outs = _in_projection(x, sc1, sh1, p['norm1'], p['w_main'], p['w_a'], p['w2'], p['b2'],
                              p['qk_gains'], consts['grp64'], consts['rot'], cos, sin,
                              nb, t, is_latent, not is_latent, tm)
        proj, g, qn, kn = outs[:4]
        proj3 = proj.reshape(nb, t, PROJ_W)
        lam_init = 0.8 - 0.6 * math.exp(-0.3 * l)
        segments = [(kn, 0, proj3, COL_VD // DIFF_V_HEAD, t)]
        if is_latent:
            ck = ctx[1][:, l].reshape(nb, -1, DIFF_QK).astype(BF16)
            cv = ctx[2][:, l].reshape(nb, -1, DIFF_V).astype(BF16)
            segments.append((ck, 0, cv, 0, ck.shape[1]))
        od = _diff_attention(qn.reshape(nb, t, DIFF_QK), segments, p['diff_lambda'], p['diff_norm'],
                             lam_init, nb, t, min(t, 256))

        if is_latent:
            s0 = _block_diag_state(ctx[0][:, l])
        else:
            s0 = jnp.zeros((nb, 2, GLA_V, GLA_K), F32)
        o_gla, s_fin = _gla(proj3, g.reshape(nb, t, 2 * GLA_K), s0, consts['gla'], nb, t, math.gcd(nb, GLA_BB))

        if t == FFT_N * FFT_N:
            fm = _fourier_fft(proj3, consts['fft'], nb)
        else:
            fm = _fourier(proj3, consts['bcs'], ct, nb, t, min(t, 256))

        x = _merge(x, g1, proj, fm.reshape(m, FOU_W), o_gla.reshape(2, m, GLA_V), od.reshape(m, DIFF_V),
                   p['w_fou'], p['w_gla_o'], p['w_diff_o'], p['w_out'], p['gla_norm'], consts['grp128'],
                   t, tm)
        x = _ffn(x, sc2, sh2, g2, p['norm2'], p['w_ff_gate'], p['w_ff_up'], p['w_ff_down'], t, tm)

        if not is_latent:
            k_out.append(outs[4].reshape(nb, t, DIFF_HEADS, 2, DIFF_HEAD_DIM))
            v_out.append(proj3[:, :, COL_VD:COL_VD + DIFF_V].astype(F32).reshape(nb, t, DIFF_HEADS, DIFF_V_HEAD))
            s_out.append(_unblock_state(s_fin))
    return x.reshape(nb, t, D_MODEL), k_out, v_out, s_out


def kernel(x_prompt, x_sample, c, cache_diff_k, cache_diff_v, state_gla, c_ctx, w_mod, b_mod, norm1, norm2, w_in, w_gla_a2, b_gla_a, gla_norm, diff_qk_norm, diff_lambda, diff_norm, w_fou, w_gla_o, w_diff_o, w_out, w_ff_gate, w_ff_up, w_ff_down):
    nb_ctx, t_ctx, _ = x_prompt.shape
    nb_lat, t_lat, _ = x_sample.shape

    n_cond = 1 + nb_lat
    r_pad = -(-n_cond // 8) * 8
    cond = jnp.zeros((r_pad, D_MODEL), F32).at[0].set(c_ctx).at[1:n_cond].set(c)
    mods = _modulation(cond, w_mod, b_mod)
    mods_ctx = [mods[l, 0:1] for l in range(DEPTH)]
    mods_lat = [mods[l, 1:n_cond] for l in range(DEPTH)]

    a0 = 2048
    a1 = a0 + 2 * GATE_RANK
    pieces = {
        'u_f': (0, 512), 'q_g': (512, 768), 'k_g': (768, 1024), 'v_g': (1024, 1536), 'r_g': (1536, 2048),
        'q_d': (a1, a1 + 512), 'k_d': (a1 + 512, a1 + 1024), 'v_d': (a1 + 1024, a1 + 1536),
        'gates': (a1 + 1536, a1 + 1536 + 3072),
    }
    order = ['gates', 'u_f', 'v_g', 'r_g', 'v_d', 'q_g', 'k_g', 'q_d', 'k_d']
    w_main = jnp.concatenate([w_in[:, :, pieces[n][0]:pieces[n][1]] for n in order], axis=-1).astype(BF16)
    w_a = jnp.pad(w_in[:, :, a0:a1], ((0, 0), (0, 0), (0, A_PAD - 2 * GATE_RANK))).astype(BF16)
    w2 = jnp.zeros((DEPTH, A_PAD, 2 * GLA_K), F32)
    w2 = w2.at[:, 0:GATE_RANK, 0:GLA_K].set(w_gla_a2[:, 0])
    w2 = w2.at[:, GATE_RANK:2 * GATE_RANK, GLA_K:].set(w_gla_a2[:, 1])
    b2 = b_gla_a.reshape(DEPTH, 1, 2 * GLA_K)

    perm = _rotate_half_perm()
    reps = DIFF_QK // DIFF_HEAD_DIM
    layers = []
    for l in range(DEPTH):
        gq = diff_qk_norm[l, 0]
        gk = diff_qk_norm[l, 1]
        layers.append({
            'norm1': norm1[l][None, :], 'norm2': norm2[l][None, :],
            'w_main': w_main[l], 'w_a': w_a[l], 'w2': w2[l], 'b2': b2[l],
            'qk_gains': tuple(jnp.tile(v, reps)[None, :] for v in (gq, gk, gq[perm], gk[perm])),
            'diff_lambda': diff_lambda[l], 'diff_norm': diff_norm[l][None, :],
            'gla_norm': jnp.tile(gla_norm[l], GLA_HEADS)[None, :],
            'w_fou': w_fou[l].astype(BF16), 'w_gla_o': w_gla_o[l].astype(BF16),
            'w_diff_o': w_diff_o[l].astype(BF16), 'w_out': w_out[l].astype(BF16),
            'w_ff_gate': w_ff_gate[l].astype(BF16), 'w_ff_up': w_ff_up[l].astype(BF16),
            'w_ff_down': w_ff_down[l].astype(BF16),
        })

    mst, hmx, ym, pm, bd = _gla_tables()
    consts = {
        'grp64': jnp.asarray(_group_mean_matrix(DIFF_QK, DIFF_HEAD_DIM), BF16),
        'grp128': jnp.asarray(_group_mean_matrix(GLA_V, GLA_DV), BF16),
        'rot': jnp.asarray(_rotate_half_matrix(), BF16),
        'rope': _rope_tables(t_lat),
        'ones': jnp.ones((t_ctx, DIFF_QK), F32),
        'bcs': jnp.asarray(_channel_dft_table(), BF16),
        'ct': {tt: (None if tt == FFT_N * FFT_N else _dft_tables(tt)) for tt in {t_ctx, t_lat}},
        'fft': _fft_tables(),
        'gla': (jnp.asarray(mst, BF16), jnp.asarray(hmx, BF16), jnp.asarray(ym, BF16), jnp.asarray(pm),
                jnp.asarray(bd)),
    }

    y_prompt, k_list, v_list, s_list = _run_path(x_prompt, mods_ctx, layers, consts, nb_ctx, t_ctx, None)
    y_sample, _, _, _ = _run_path(x_sample, mods_lat, layers, consts, nb_lat, t_lat,
                                  (state_gla, cache_diff_k, cache_diff_v))
    return (y_prompt, y_sample, jnp.stack(k_list, axis=1), jnp.stack(v_list, axis=1),
            jnp.stack(s_list, axis=1))
```

```python
import functools
import math

import numpy as np
import jax
import jax.numpy as jnp
from jax import lax
from jax.experimental import pallas as pl
from jax.experimental.pallas import tpu as pltpu

F32 = jnp.float32
BF16 = jnp.bfloat16

D_MODEL = 1024
DEPTH = 4
GRID_W = 64
FOU_GROUPS = 4
FOU_GROUP_W = 128
FOU_W = 512
GLA_HEADS = 4
GLA_DK = 64
GLA_DV = 128
GLA_K = 256
GLA_V = 512
GATE_RANK = 16
GATE_TEMP = 16.0
DIFF_HEADS = 4
DIFF_HEAD_DIM = 64
DIFF_V_HEAD = 128
DIFF_QK = 512
DIFF_V = 512
ROPE_AXIS_DIM = 32
ROPE_FREQS = 16
ROPE_BASE = 10000.0
N_BRANCH = 3
D_FF = 2816
EPS = 1e-6

COL_GATES = 0
COL_UF = 3072
COL_VG = 3584
COL_RG = 4096
COL_VD = 4608
COL_QG = 5120
COL_KG = 5376
PROJ_W = 5632
W_COLS = PROJ_W + 2 * 512
A_PAD = 128

GLA_C = 64
GLA_LEVELS = (32, 16, 8, 4, 2, 1)
GLA_MROWS = 64 * (1 + len(GLA_LEVELS)) + 8
GLA_BB = 8
VMEM_LIMIT = 48 * 1024 * 1024
VMEM_LIMIT_BIG = 56 * 1024 * 1024
LOG2_E = math.log2(math.e)


def _params(sem, vmem=VMEM_LIMIT):
    return pltpu.CompilerParams(dimension_semantics=sem, vmem_limit_bytes=vmem)


def _dot(a, b):
    return jnp.dot(a, b, preferred_element_type=F32)


def _dot_nt(a, b):
    return lax.dot_general(a, b, (((1,), (1,)), ((), ())), preferred_element_type=F32)


def _dot_tn(a, b):
    return lax.dot_general(a, b, (((0,), (0,)), ((), ())), preferred_element_type=F32)


def _split3(x):
    x1 = x.astype(BF16)
    r1 = x - x1.astype(F32)
    x2 = r1.astype(BF16)
    x3 = (r1 - x2.astype(F32)).astype(BF16)
    return x1, x2, x3


def _mod_kernel(c_ref, w_ref, b_ref, o_ref):
    c = c_ref[...]
    s = c * jax.nn.sigmoid(c)
    s1, s2, s3 = _split3(s)
    w = w_ref[0]
    w1, w2, w3 = _split3(w)
    acc = _dot(s1, w1) + (_dot(s1, w2) + _dot(s2, w1)) + (_dot(s2, w2) + _dot(s1, w3) + _dot(s3, w1))
    o_ref[0] = acc + b_ref[0]


def _modulation(cond, w_mod, b_mod):
    r = cond.shape[0]
    tn = 768
    n = 6 * D_MODEL
    return pl.pallas_call(
        _mod_kernel,
        grid=(DEPTH, n // tn),
        in_specs=[
            pl.BlockSpec((r, D_MODEL), lambda l, j: (0, 0)),
            pl.BlockSpec((1, D_MODEL, tn), lambda l, j: (l, 0, j)),
            pl.BlockSpec((1, 1, tn), lambda l, j: (l, 0, j)),
        ],
        out_specs=pl.BlockSpec((1, r, tn), lambda l, j: (l, 0, j)),
        out_shape=jax.ShapeDtypeStruct((DEPTH, r, n), F32),
        compiler_params=_params(("arbitrary", "arbitrary")),
        name="modulation",
    )(cond, w_mod, b_mod.reshape(DEPTH, 1, n))


def _inproj_kernel(tn, use_rope, want_f32, *refs):
    (x_ref, sc_ref, sh_ref, gain_ref, w_ref, wa_ref, w2_ref, b2_ref,
     gq_ref, gk_ref, gqp_ref, gkp_ref, grp_ref, rot_ref, cos_ref, sin_ref) = refs[:16]
    proj_ref, g_ref, qn_ref, kn_ref = refs[16:20]
    kf_ref = refs[20] if want_f32 else None
    h_scr = refs[-1]

    x = x_ref[...]
    ms = jnp.mean(x * x, axis=-1, keepdims=True)
    scale = gain_ref[...] * (1.0 + sc_ref[0])
    hb = (x * lax.rsqrt(ms + EPS) * scale + sh_ref[0]).astype(BF16)
    h_scr[...] = hb
    a = _dot(hb, wa_ref[...])
    a1, a2, a3 = _split3(a)
    v1, v2, v3 = _split3(w2_ref[...])
    z = (_dot(a1, v1) + (_dot(a1, v2) + _dot(a2, v1))
         + (_dot(a2, v2) + _dot(a1, v3) + _dot(a3, v1))) + b2_ref[...]
    logsig = jnp.minimum(z, 0.0) - jnp.log1p(jnp.exp(-jnp.abs(z)))
    g_ref[...] = logsig * (1.0 / GATE_TEMP)
    def prep(col, g_ref_, gp_ref_, out_scale):
        y = _dot(h_scr[...], w_ref[:, col:col + DIFF_QK])
        msq = _dot((y * y).astype(BF16), grp_ref[...])
        r = lax.rsqrt(msq + EPS)
        out = y * r * g_ref_[...]
        if use_rope:
            y1 = y.astype(BF16)
            y2 = (y - y1.astype(F32)).astype(BF16)
            rot = _dot(y1, rot_ref[...]) + _dot(y2, rot_ref[...])
            out = out * cos_ref[...] + rot * r * gp_ref_[...] * sin_ref[...]
        return out * out_scale

    n_tiles = PROJ_W // tn
    for j in range(n_tiles):
        proj_ref[:, j * tn:(j + 1) * tn] = _dot(h_scr[...], w_ref[:, j * tn:(j + 1) * tn]).astype(BF16)
        if j == n_tiles // 4:
            qn_ref[...] = prep(PROJ_W, gq_ref, gqp_ref, DIFF_HEAD_DIM ** -0.5 * LOG2_E).astype(BF16)
        if j == (3 * n_tiles) // 4:
            kn = prep(PROJ_W + DIFF_QK, gk_ref, gkp_ref, 1.0)
            kn_ref[0] = kn.astype(BF16)
            if want_f32:
                kf_ref[...] = kn


def _in_projection(x, sc, sh, gain, w_main, w_a, w2, b2, qk_gains, grp, rot, cos, sin,
                   nb, t, use_rope, want_f32, tm):
    m = x.shape[0]
    tn = 512
    per = t // tm
    ncond = sc.shape[0]
    cidx = (lambda i: (i // per, 0, 0)) if ncond > 1 else (lambda i: (0, 0, 0))
    const = lambda i: (0, 0)
    once = pl.Buffered(1)
    vec = pl.BlockSpec((1, DIFF_QK), const)
    mat = pl.BlockSpec((DIFF_QK, DIFF_QK), const, pipeline_mode=once)
    tab = pl.BlockSpec((tm, DIFF_QK), lambda i: (i % per, 0))
    out_specs = [
        pl.BlockSpec((tm, PROJ_W), lambda i: (i, 0)),
        pl.BlockSpec((tm, 2 * GLA_K), lambda i: (i, 0)),
        pl.BlockSpec((tm, DIFF_QK), lambda i: (i, 0)),
        pl.BlockSpec((1, tm, DIFF_QK), lambda i: (i // per, i % per, 0)),
    ]
    out_shape = [
        jax.ShapeDtypeStruct((m, PROJ_W), BF16),
        jax.ShapeDtypeStruct((m, 2 * GLA_K), F32),
        jax.ShapeDtypeStruct((m, DIFF_QK), BF16),
        jax.ShapeDtypeStruct((nb, t, DIFF_QK), BF16),
    ]
    if want_f32:
        out_specs.append(pl.BlockSpec((tm, DIFF_QK), lambda i: (i, 0)))
        out_shape.append(jax.ShapeDtypeStruct((m, DIFF_QK), F32))
    gq, gk, gqp, gkp = qk_gains
    return pl.pallas_call(
        functools.partial(_inproj_kernel, tn, use_rope, want_f32),
        grid=(m // tm,),
        in_specs=[
            pl.BlockSpec((tm, D_MODEL), lambda i: (i, 0)),
            pl.BlockSpec((1, 1, D_MODEL), cidx),
            pl.BlockSpec((1, 1, D_MODEL), cidx),
            pl.BlockSpec((1, D_MODEL), const),
            pl.BlockSpec((D_MODEL, W_COLS), const, pipeline_mode=once),
            pl.BlockSpec((D_MODEL, A_PAD), const, pipeline_mode=once),
            pl.BlockSpec((A_PAD, 2 * GLA_K), const, pipeline_mode=once),
            pl.BlockSpec((1, 2 * GLA_K), const),
            vec, vec, vec, vec, mat, mat, tab, tab,
        ],
        out_specs=out_specs,
        out_shape=out_shape,
        scratch_shapes=[pltpu.VMEM((tm, D_MODEL), BF16)],
        compiler_params=_params(("arbitrary",), VMEM_LIMIT_BIG),
        name="in_projection",
    )(x, sc, sh, gain, w_main, w_a, w2, b2, gq, gk, gqp, gkp, grp, rot, cos, sin)


def _diffattn_kernel(seg_len, lam_init, *refs):
    n_seg = len(seg_len)
    q_ref = refs[0]
    kv_refs = refs[1:1 + 2 * n_seg]
    lam_ref, gain_ref, o_ref = refs[1 + 2 * n_seg:4 + 2 * n_seg]
    kz0, kz1, vaug, s_a, s_b, m_a, m_b = refs[4 + 2 * n_seg:]
    i = pl.program_id(2)

    @pl.when(i == 0)
    def _():
        off = 0
        for s, tk in enumerate(seg_len):
            k = kv_refs[2 * s][0]
            lane = lax.broadcasted_iota(jnp.int32, k.shape, 1)
            zero = jnp.zeros_like(k)
            kz0[off:off + tk, :] = jnp.where(lane < DIFF_HEAD_DIM, k, zero)
            kz1[off:off + tk, :] = jnp.where(lane >= DIFF_HEAD_DIM, k, zero)
            v = kv_refs[2 * s + 1][0]
            vaug[off:off + tk, 0:DIFF_V_HEAD] = v
            vaug[off:off + tk, DIFF_V_HEAD:] = jnp.ones_like(v)
            off += tk

    @pl.when((pl.program_id(0) == 0) & (pl.program_id(1) == 0) & (i == 0))
    def _():
        s_b[...] = jnp.zeros_like(s_b)
        m_b[...] = jnp.zeros_like(m_b)

    lp = lam_ref[...]
    lam = (jnp.exp(jnp.sum(lp[0:1] * lp[1:2], axis=-1, keepdims=True))
           - jnp.exp(jnp.sum(lp[2:3] * lp[3:4], axis=-1, keepdims=True)) + lam_init)

    def score(w, kz, s_write, m_write):
        sc = _dot_nt(q_ref[0], kz[...])
        s_write[w] = sc
        m_write[w] = jnp.broadcast_to(sc.max(axis=-1, keepdims=True), m_write.shape[1:])

    def finish(w, s_read, m_read):
        p = jnp.exp2(s_read[w] - m_read[w][:, 0:1]).astype(BF16)
        acc = _dot(p, vaug[...])
        return acc[:, 0:DIFF_V_HEAD] / acc[:, DIFF_V_HEAD:]

    def step(s_write, m_write, s_read, m_read):
        score(0, kz0, s_write, m_write)
        score(1, kz1, s_write, m_write)
        o = finish(0, s_read, m_read) - lam * finish(1, s_read, m_read)
        ms = jnp.mean(o * o, axis=-1, keepdims=True)
        o = o * lax.rsqrt(ms + EPS) * gain_ref[...] * (1.0 - lam_init)
        o_ref[0] = o.astype(BF16)

    @pl.when(i % 2 == 0)
    def _():
        step(s_a, m_a, s_b, m_b)

    @pl.when(i % 2 == 1)
    def _():
        step(s_b, m_b, s_a, m_a)


def _diff_attention(qn, segments, lam_p, gain, lam_init, nb, tq_total, tq):
    hd = 2 * DIFF_HEAD_DIM
    nq = tq_total // tq
    in_specs = [pl.BlockSpec((1, tq, hd), lambda b, h, i: (b, jnp.minimum(i, nq - 1), h))]
    args = [qn]
    seg_len = tuple(seg[4] for seg in segments)
    tk_all = sum(seg_len)
    for (ka, kc, va, vc, tk) in segments:
        in_specs.append(pl.BlockSpec((1, tk, hd), lambda b, h, i, kc=kc: (b, 0, kc + h)))
        in_specs.append(pl.BlockSpec((1, tk, DIFF_V_HEAD), lambda b, h, i, vc=vc: (b, 0, vc + h)))
        args += [ka, va]
    in_specs.append(pl.BlockSpec((4, DIFF_HEAD_DIM), lambda b, h, i: (0, 0)))
    in_specs.append(pl.BlockSpec((1, DIFF_V_HEAD), lambda b, h, i: (0, 0)))
    args += [lam_p, gain]
    scratch = [
        pltpu.VMEM((tk_all, hd), BF16), pltpu.VMEM((tk_all, hd), BF16),
        pltpu.VMEM((tk_all, 2 * DIFF_V_HEAD), BF16),
        pltpu.VMEM((2, tq, tk_all), F32), pltpu.VMEM((2, tq, tk_all), F32),
        pltpu.VMEM((2, tq, hd), F32), pltpu.VMEM((2, tq, hd), F32),
    ]
    return pl.pallas_call(
        functools.partial(_diffattn_kernel, seg_len, lam_init),
        grid=(nb, DIFF_HEADS, nq + 1),
        in_specs=in_specs,
        out_specs=pl.BlockSpec((1, tq, DIFF_V_HEAD), lambda b, h, i: (b, jnp.maximum(i - 1, 0), h)),
        out_shape=jax.ShapeDtypeStruct((nb, tq_total, DIFF_V), BF16),
        scratch_shapes=scratch,
        compiler_params=_params(("arbitrary", "arbitrary", "arbitrary")),
        name="diff_attention",
    )(*args)


def _gla_tables():
    c = GLA_C
    nl = len(GLA_LEVELS)
    mst = np.zeros((2, GLA_MROWS, c), np.float32)
    xm = np.zeros((2, nl + 1, c), np.float32)
    pm = np.zeros((2, nl + 1, c, c), np.float32)
    idx = np.arange(c)
    for d in range(2):
        ip = idx if d == 0 else c - 1 - idx
        mst[d, 0:c] = (ip[None, :] <= ip[:, None])
        mst[d, c * (nl + 1)] = 1.0
        pm[d, 0] = np.eye(c)
        xm[d, 0] = 1.0
        for li, s in enumerate(GLA_LEVELS):
            ref = (ip // (2 * s)) * (2 * s) + s - 1
            mst[d, c * (li + 1):c * (li + 2)] = (ip[None, :] <= ref[:, None])
            odd = (ip // s) % 2 == 1
            xm[d, li + 1] = odd
            same = (ip[:, None] // (2 * s)) == (ip[None, :] // (2 * s))
            pm[d, li + 1] = odd[:, None] & (~odd)[None, :] & same
    pm = np.tile(pm, (1, 1, 1, GLA_HEADS))
    rows = np.arange(GLA_HEADS * c)[:, None] // c
    hm = (rows == (np.arange(GLA_K)[None, :] // GLA_DK)).astype(np.float32)
    ykeep = np.concatenate([np.ones((2, 1, c), np.float32), 1.0 - xm[:, 1:]], axis=1)
    hmy = hm[None, None] * np.tile(ykeep, (1, 1, GLA_HEADS))[:, :, :, None]
    xq = np.broadcast_to(xm[:, 1:, :, None], (2, nl, c, GLA_K)).copy()
    vbd = (rows == (np.arange(GLA_V)[None, :] // GLA_DV)).astype(np.float32)
    return mst, hmy, xq, pm, vbd


def _gla_kernel(bb, q_ref, k_ref, v_ref, g_ref, mst_ref, hmy_ref, xq_ref, pm_ref, vbd_ref, s0_ref,
                o_ref, sfin_ref, s_scr):
    c_idx = pl.program_id(2)

    @pl.when(c_idx == 0)
    def _():
        s_scr[...] = s0_ref[:, 0]

    c = GLA_C
    nl = len(GLA_LEVELS)
    m = mst_ref[0]
    seqs = range(bb)

    def stack_keys(yb, li):
        return jnp.concatenate([yb] * GLA_HEADS, axis=0) * hmy_ref[0, li]

    r_all = []
    for b in seqs:
        g = g_ref[b]
        g1 = g.astype(BF16)
        g2 = (g - g1.astype(F32)).astype(BF16)
        r_all.append(_dot(m, g1) + _dot(m, g2))
    cum = [r[0:c] for r in r_all]
    tot = [r[c * (nl + 1):c * (nl + 1) + 1] for r in r_all]
    q = [q_ref[b].astype(F32) * (GLA_DK ** -0.5) for b in seqs]
    k = [k_ref[b].astype(F32) for b in seqs]

    o_inter = []
    for b in seqs:
        state = s_scr[b]
        q_dec = stack_keys((q[b] * jnp.exp(cum[b])).astype(BF16), 0)
        o_rows = _dot_nt(q_dec, state.astype(BF16))
        o_inter.append(jnp.concatenate([o_rows[h * c:(h + 1) * c] for h in range(GLA_HEADS)], axis=1))
        k_dec = stack_keys((k[b] * jnp.exp(tot[b] - cum[b])).astype(BF16), 0)
        vf = v_ref[b].astype(F32)
        v_t = jnp.concatenate(
            [jnp.concatenate([vf[:, (2 * p) * GLA_DV:(2 * p + 1) * GLA_DV],
                              vf[:, (2 * p + 1) * GLA_DV:(2 * p + 2) * GLA_DV]], axis=0).T
             for p in range(GLA_HEADS // 2)], axis=1).astype(BF16)
        s_scr[b] = state * jnp.exp(tot[b]) + _dot(v_t, k_dec)

    @pl.when(c_idx == pl.num_programs(2) - 1)
    def _():
        sfin_ref[:, 0] = s_scr[...]

    att = [pm_ref[0, 0] * _dot_nt(q[b].astype(BF16), stack_keys(k_ref[b], 0)) for b in seqs]
    for li in range(nl):
        for b in seqs:
            ref = r_all[b][c * (li + 1):c * (li + 2)]
            e = jnp.exp(-jnp.abs(cum[b] - ref))
            xs = (q[b] * e).astype(BF16) * xq_ref[0, li]
            ys = stack_keys((k[b] * e).astype(BF16), li + 1)
            att[b] = att[b] + pm_ref[0, li + 1] * _dot_nt(xs, ys)
    for b in seqs:
        v_bd = jnp.concatenate([v_ref[b]] * GLA_HEADS, axis=0) * vbd_ref[...]
        o_ref[0, b] = o_inter[b] + _dot(att[b].astype(BF16), v_bd)


def _gla(proj3, g3, s0, tables, nb, t, bb):
    mst, hmy, xq, pm, vbd = tables
    nc = t // GLA_C
    nl = len(GLA_LEVELS)

    def chunk(d, c):
        return jnp.where(d == 0, c, nc - 1 - c)

    return pl.pallas_call(
        functools.partial(_gla_kernel, bb),
        grid=(nb // bb, 2, nc),
        in_specs=[
            pl.BlockSpec((bb, GLA_C, GLA_K), lambda b, d, c: (b, chunk(d, c), COL_QG // GLA_K)),
            pl.BlockSpec((bb, GLA_C, GLA_K), lambda b, d, c: (b, chunk(d, c), COL_KG // GLA_K)),
            pl.BlockSpec((bb, GLA_C, GLA_V), lambda b, d, c: (b, chunk(d, c), COL_VG // GLA_V)),
            pl.BlockSpec((bb, GLA_C, GLA_K), lambda b, d, c: (b, chunk(d, c), d)),
            pl.BlockSpec((1, GLA_MROWS, GLA_C), lambda b, d, c: (d, 0, 0)),
            pl.BlockSpec((1, nl + 1, GLA_HEADS * GLA_C, GLA_K), lambda b, d, c: (d, 0, 0, 0)),
            pl.BlockSpec((1, nl, GLA_C, GLA_K), lambda b, d, c: (d, 0, 0, 0)),
            pl.BlockSpec((1, nl + 1, GLA_C, GLA_HEADS * GLA_C), lambda b, d, c: (d, 0, 0, 0)),
            pl.BlockSpec((GLA_HEADS * GLA_C, GLA_V), lambda b, d, c: (0, 0)),
            pl.BlockSpec((bb, 1, GLA_DV, GLA_K), lambda b, d, c: (b, d, 0, 0)),
        ],
        out_specs=[
            pl.BlockSpec((1, bb, GLA_C, GLA_V), lambda b, d, c: (d, b, chunk(d, c), 0)),
            pl.BlockSpec((bb, 1, GLA_DV, GLA_K), lambda b, d, c: (b, d, 0, 0)),
        ],
        out_shape=[
            jax.ShapeDtypeStruct((2, nb, t, GLA_V), F32),
            jax.ShapeDtypeStruct((nb, 2, GLA_DV, GLA_K), F32),
        ],
        scratch_shapes=[pltpu.VMEM((bb, GLA_DV, GLA_K), F32)],
        compiler_params=_params(("arbitrary", "arbitrary", "arbitrary")),
        name="gla",
    )(proj3, proj3, proj3, g3, mst, hmy, xq, pm, vbd, s0)


def _fourier_kernel(t, u_ref, bcs_ref, ct_ref, o_ref, y_scr):
    i = pl.program_id(1)

    @pl.when(i == 0)
    def _():
        rows = min(t, 512)
        for r in range(t // rows):
            y = _dot(u_ref[0, r * rows:(r + 1) * rows, :], bcs_ref[...])
            y_scr[r * rows:(r + 1) * rows, :] = y[:, :FOU_W].astype(BF16)
            y_scr[t + r * rows:t + (r + 1) * rows, :] = y[:, FOU_W:].astype(BF16)

    o_ref[0] = _dot(ct_ref[...], y_scr[...]).astype(BF16)


def _fourier(proj3, bcs, ct, nb, t, tm):
    return pl.pallas_call(
        functools.partial(_fourier_kernel, t),
        grid=(nb, t // tm),
        in_specs=[
            pl.BlockSpec((1, t, FOU_W), lambda b, i: (b, 0, COL_UF // FOU_W)),
            pl.BlockSpec((FOU_W, 2 * FOU_W), lambda b, i: (0, 0)),
            pl.BlockSpec((tm, 2 * t), lambda b, i: (i, 0)),
        ],
        out_specs=pl.BlockSpec((1, tm, FOU_W), lambda b, i: (b, i, 0)),
        out_shape=jax.ShapeDtypeStruct((nb, t, FOU_W), BF16),
        scratch_shapes=[pltpu.VMEM((2 * t, FOU_W), BF16)],
        compiler_params=_params(("arbitrary", "arbitrary")),
        name="fourier_mix",
    )(proj3, bcs, ct)


FFT_N = 64
FFT_G = 8
FFT_CW = 256


def _fft_kernel(u_ref, bcs_ref, m1_ref, q2_ref, o_ref, z_scr, a_scr):
    n, gsz, cw = FFT_N, FFT_G, FFT_CW
    rows = n * gsz
    for r in range(n // gsz):
        y = _dot(u_ref[0, r * rows:(r + 1) * rows, :], bcs_ref[...])
        z_scr[gsz * r:gsz * (r + 1)] = y.reshape(gsz, n, 2 * cw)
    for g in range(n // gsz):
        zg = z_scr[:, gsz * g:gsz * (g + 1), :].reshape(rows, 2 * cw)
        rhs = jnp.concatenate([zg[:, :cw], zg[:, cw:]], axis=0).astype(BF16)
        a_scr[g] = _dot(m1_ref[...], rhs).astype(BF16)
    for f in range(n // gsz):
        slabs = [a_scr[g, part * rows + n * f:part * rows + n * (f + 1), :]
                 for part in range(2) for g in range(n // gsz)]
        res = _dot(q2_ref[f], jnp.concatenate(slabs, axis=0))
        o_ref[0, :, gsz * f:gsz * (f + 1), :] = res.reshape(n, gsz, cw)


def _fft_tables():
    n, gsz, t = FFT_N, FFT_G, FFT_N * FFT_N
    k = np.arange(FOU_GROUP_W)
    ang = 2.0 * np.pi * ((k[:, None] * k[None, :]) % FOU_GROUP_W) / FOU_GROUP_W
    eye2 = np.eye(FFT_CW // FOU_GROUP_W)
    bcs = np.concatenate([np.kron(eye2, np.cos(ang)), -np.kron(eye2, np.sin(ang))], axis=1) * FOU_GROUP_W ** -0.5
    i = np.arange(n)
    a1 = 2.0 * np.pi * ((i[:, None] * i[None, :]) % n) / n
    c1 = np.kron(np.cos(a1), np.eye(gsz))
    s1 = np.kron(np.sin(a1), np.eye(gsz))
    m1 = np.block([[c1, s1], [-s1, c1]])
    ngrp = n // gsz
    shape = (ngrp, n * gsz, n * gsz)
    f = lax.broadcasted_iota(jnp.int32, shape, 0)
    row = lax.broadcasted_iota(jnp.int32, shape, 1)
    col = lax.broadcasted_iota(jnp.int32, shape, 2)
    f2, a = row // gsz, row % gsz
    g, ap, j = col // (gsz * gsz), (col // gsz) % gsz, col % gsz
    freq = gsz * f + a + n * f2
    theta = (((gsz * g + j) * freq) % t).astype(F32) * (2.0 * math.pi / t)
    keep = (a == ap)
    scale = t ** -0.5
    qc = jnp.where(keep, jnp.cos(theta) * scale, 0.0)
    qs = jnp.where(keep, jnp.sin(theta) * scale, 0.0)
    q2 = jnp.concatenate([qc, qs], axis=2).astype(BF16)
    return jnp.asarray(bcs, BF16), jnp.asarray(m1, BF16), q2


def _fourier_fft(proj3, tables, nb):
    bcs, m1, q2 = tables
    n, gsz, cw = FFT_N, FFT_G, FFT_CW
    t = n * n
    once = pl.Buffered(1)
    out = pl.pallas_call(
        _fft_kernel,
        grid=(nb, FOU_W // cw),
        in_specs=[
            pl.BlockSpec((1, t, cw), lambda b, h: (b, 0, COL_UF // cw + h)),
            pl.BlockSpec((cw, 2 * cw), lambda b, h: (0, 0), pipeline_mode=once),
            pl.BlockSpec((2 * n * gsz, 2 * n * gsz), lambda b, h: (0, 0), pipeline_mode=once),
            pl.BlockSpec((n // gsz, n * gsz, 2 * n * gsz), lambda b, h: (0, 0, 0), pipeline_mode=once),
        ],
        out_specs=pl.BlockSpec((1, n, n, cw), lambda b, h: (b, 0, 0, h)),
        out_shape=jax.ShapeDtypeStruct((nb, n, n, FOU_W), F32),
        scratch_shapes=[pltpu.VMEM((n, n, 2 * cw), F32), pltpu.VMEM((n // gsz, 2 * n * gsz, cw), BF16)],
        compiler_params=_params(("arbitrary", "arbitrary")),
        name="fourier_fft",
    )(proj3, bcs, m1, q2)
    return out.reshape(nb, t, FOU_W)


def _dft_tables(t):
    lo = min(t, 64)
    hi = t // lo
    f = jnp.arange(t, dtype=jnp.int32)[None, :]
    unit = 2.0 * math.pi / t
    a = ((lo * jnp.arange(hi, dtype=jnp.int32)[:, None] * f) % t).astype(F32) * unit
    b = ((jnp.arange(lo, dtype=jnp.int32)[:, None] * f) % t).astype(F32) * unit
    ca, sa = jnp.cos(a)[:, None, :], jnp.sin(a)[:, None, :]
    cb, sb = jnp.cos(b)[None, :, :], jnp.sin(b)[None, :, :]
    scale = t ** -0.5
    cos = ((ca * cb - sa * sb) * scale).reshape(t, t)
    msin = ((sa * cb + ca * sb) * -scale).reshape(t, t)
    return jnp.concatenate([cos, msin], axis=1).astype(BF16)


def _channel_dft_table():
    n = np.arange(FOU_GROUP_W)
    ang = 2.0 * np.pi * ((n[:, None] * n[None, :]) % FOU_GROUP_W) / FOU_GROUP_W
    eye = np.eye(FOU_GROUPS)
    bc = np.kron(eye, np.cos(ang)) * FOU_GROUP_W ** -0.5
    bs = np.kron(eye, np.sin(ang)) * FOU_GROUP_W ** -0.5
    return np.concatenate([bc, bs], axis=1).astype(np.float32)


def _merge_kernel(x_ref, g1_ref, fm_ref, of_ref, ob_ref, r_ref, od_ref, gt_ref,
                  wf_ref, wg_ref, wd_ref, wo_ref, gn_ref, grp_ref, o_ref):
    y_f = _dot(fm_ref[...].astype(BF16), wf_ref[...])
    og = of_ref[0] + ob_ref[0]
    ms = _dot((og * og).astype(BF16), grp_ref[...])
    r = r_ref[...].astype(F32)
    og = og * lax.rsqrt(ms + EPS) * gn_ref[...] * (r * jax.nn.sigmoid(r))
    y_g = _dot(og.astype(BF16), wg_ref[...])
    y_d = _dot(od_ref[...], wd_ref[...])
    gates = jax.nn.sigmoid(gt_ref[...].astype(F32))
    merged = (gates[:, 0:D_MODEL] * y_f + gates[:, D_MODEL:2 * D_MODEL] * y_g
              + gates[:, 2 * D_MODEL:3 * D_MODEL] * y_d)
    o_ref[...] = x_ref[...] + g1_ref[0] * _dot(merged.astype(BF16), wo_ref[...])


def _merge(x, g1, proj, fm, o_gla, od, w_fou, w_gla_o, w_diff_o, w_out, gn, grp, tokens_per_cond, tm):
    m = x.shape[0]
    per = tokens_per_cond // tm
    ncond = g1.shape[0]
    cidx = (lambda i: (i // per, 0, 0)) if ncond > 1 else (lambda i: (0, 0, 0))
    once = pl.Buffered(1)
    half = pl.BlockSpec((FOU_W, D_MODEL), lambda i: (0, 0), pipeline_mode=once)
    return pl.pallas_call(
        _merge_kernel,
        grid=(m // tm,),
        in_specs=[
            pl.BlockSpec((tm, D_MODEL), lambda i: (i, 0)),
            pl.BlockSpec((1, 1, D_MODEL), cidx),
            pl.BlockSpec((tm, FOU_W), lambda i: (i, 0)),
            pl.BlockSpec((1, tm, GLA_V), lambda i: (0, i, 0)),
            pl.BlockSpec((1, tm, GLA_V), lambda i: (1, i, 0)),
            pl.BlockSpec((tm, GLA_V), lambda i: (i, COL_RG // GLA_V)),
            pl.BlockSpec((tm, DIFF_V), lambda i: (i, 0)),
            pl.BlockSpec((tm, N_BRANCH * D_MODEL), lambda i: (i, COL_GATES)),
            half, half, half,
            pl.BlockSpec((D_MODEL, D_MODEL), lambda i: (0, 0), pipeline_mode=once),
            pl.BlockSpec((1, GLA_V), lambda i: (0, 0)),
            pl.BlockSpec((GLA_V, GLA_V), lambda i: (0, 0)),
        ],
        out_specs=pl.BlockSpec((tm, D_MODEL), lambda i: (i, 0)),
        out_shape=jax.ShapeDtypeStruct((m, D_MODEL), F32),
        compiler_params=_params(("arbitrary",)),
        name="merge",
    )(x, g1, fm, o_gla, o_gla, proj, od, proj, w_fou, w_gla_o, w_diff_o, w_out, gn, grp)


def _ffn_kernel(tf, x_ref, sc_ref, sh_ref, g2_ref, gain_ref, wg_ref, wu_ref, wd_ref, o_ref, h_scr):
    x = x_ref[...]
    ms = jnp.mean(x * x, axis=-1, keepdims=True)
    scale = gain_ref[...] * (1.0 + sc_ref[0])
    h_scr[...] = (x * lax.rsqrt(ms + EPS) * scale + sh_ref[0]).astype(BF16)
    acc = None
    for f in range(D_FF // tf):
        cols = slice(f * tf, (f + 1) * tf)
        gate = _dot(h_scr[...], wg_ref[:, cols])
        up = _dot(h_scr[...], wu_ref[:, cols])
        a = (gate * jax.nn.sigmoid(gate) * up).astype(BF16)
        part = _dot(a, wd_ref[cols, :])
        acc = part if acc is None else acc + part
    o_ref[...] = x_ref[...] + g2_ref[0] * acc


def _ffn(x, sc, sh, g2, gain, w_gate, w_up, w_down, tokens_per_cond, tm):
    m = x.shape[0]
    tf = 256
    per = tokens_per_cond // tm
    ncond = sc.shape[0]
    cidx = (lambda i: (i // per, 0, 0)) if ncond > 1 else (lambda i: (0, 0, 0))
    const = lambda i: (0, 0)
    once = pl.Buffered(1)
    return pl.pallas_call(
        functools.partial(_ffn_kernel, tf),
        grid=(m // tm,),
        in_specs=[
            pl.BlockSpec((tm, D_MODEL), lambda i: (i, 0)),
            pl.BlockSpec((1, 1, D_MODEL), cidx),
            pl.BlockSpec((1, 1, D_MODEL), cidx),
            pl.BlockSpec((1, 1, D_MODEL), cidx),
            pl.BlockSpec((1, D_MODEL), const),
            pl.BlockSpec((D_MODEL, D_FF), const, pipeline_mode=once),
            pl.BlockSpec((D_MODEL, D_FF), const, pipeline_mode=once),
            pl.BlockSpec((D_FF, D_MODEL), const, pipeline_mode=once),
        ],
        out_specs=pl.BlockSpec((tm, D_MODEL), lambda i: (i, 0)),
        out_shape=jax.ShapeDtypeStruct((m, D_MODEL), F32),
        scratch_shapes=[pltpu.VMEM((tm, D_MODEL), BF16)],
        compiler_params=_params(("arbitrary",)),
        name="ffn",
    )(x, sc, sh, g2, gain, w_gate, w_up, w_down)


def _group_mean_matrix(width, group):
    idx = np.arange(width) // group
    return (idx[:, None] == idx[None, :]).astype(np.float32) / group


def _rotate_half_matrix():
    r = np.zeros((DIFF_HEAD_DIM, DIFF_HEAD_DIM), np.float32)
    for axis in range(2):
        base = axis * ROPE_AXIS_DIM
        for f in range(ROPE_FREQS):
            r[base + ROPE_FREQS + f, base + f] = -1.0
            r[base + f, base + ROPE_FREQS + f] = 1.0
    return np.kron(np.eye(DIFF_QK // DIFF_HEAD_DIM), r).astype(np.float32)


def _rotate_half_perm():
    p = np.zeros((DIFF_HEAD_DIM,), np.int32)
    for axis in range(2):
        base = axis * ROPE_AXIS_DIM
        for f in range(ROPE_FREQS):
            p[base + f] = base + ROPE_FREQS + f
            p[base + ROPE_FREQS + f] = base + f
    return p


def _rope_tables(n_tokens):
    rows = n_tokens // GRID_W
    row = jnp.repeat(jnp.arange(rows), GRID_W).astype(F32)
    col = jnp.tile(jnp.arange(GRID_W), rows).astype(F32)
    inv = ROPE_BASE ** (-jnp.arange(ROPE_FREQS, dtype=F32) * 2.0 / ROPE_AXIS_DIM)
    ang_r = row[:, None] * inv
    ang_c = col[:, None] * inv
    ang = jnp.concatenate([ang_r, ang_r, ang_c, ang_c], axis=-1)
    reps = DIFF_QK // DIFF_HEAD_DIM
    return jnp.tile(jnp.cos(ang), (1, reps)), jnp.tile(jnp.sin(ang), (1, reps))


def _pack_state(s):
    return jnp.transpose(s, (0, 1, 4, 2, 3)).reshape(s.shape[0], 2, GLA_DV, GLA_K)


def _unpack_state(sp):
    s = sp.reshape(sp.shape[0], 2, GLA_DV, GLA_HEADS, GLA_DK)
    return jnp.transpose(s, (0, 1, 3, 4, 2))


def _run_path(x, mods, layers, consts, nb, t, ctx):
    m = nb * t
    tm = 512 if t >= 512 else t
    x = x.reshape(m, D_MODEL)
    is_latent = ctx is not None
    cos, sin = consts['rope'] if is_latent else (consts['ones'], consts['ones'])
    ct = consts['ct'][t]
    k_out, v_out, s_out = [], [], []
    for l, p in enumerate(layers):
        sh1, sc1, g1, sh2, sc2, g2 = [mods[l][:, None, j * D_MODEL:(j + 1) * D_MODEL] for j in range(6)]
        outs = _in_projection(x, sc1, sh1, p['norm1'], p['w_main'], p['w_a'], p['w2'], p['b2'],
                              p['qk_gains'], consts['grp64'], consts['rot'], cos, sin,
                              nb, t, is_latent, not is_latent, tm)
        proj, g, qn, kn = outs[:4]
        proj3 = proj.reshape(nb, t, PROJ_W)
        lam_init = 0.8 - 0.6 * math.exp(-0.3 * l)
        segments = [(kn, 0, proj3, COL_VD // DIFF_V_HEAD, t)]
        if is_latent:
            ck = ctx[1][:, l].reshape(nb, -1, DIFF_QK).astype(BF16)
            cv = ctx[2][:, l].reshape(nb, -1, DIFF_V).astype(BF16)
            segments.append((ck, 0, cv, 0, ck.shape[1]))
        od = _diff_attention(qn.reshape(nb, t, DIFF_QK), segments, p['diff_lambda'], p['diff_norm'],
                             lam_init, nb, t, min(t, 256))

        if is_latent:
            s0 = _pack_state(ctx[0][:, l])
        else:
            s0 = jnp.zeros((nb, 2, GLA_DV, GLA_K), F32)
        o_gla, s_fin = _gla(proj3, g.reshape(nb, t, 2 * GLA_K), s0, consts['gla'], nb, t, math.gcd(nb, GLA_BB))

        if t == FFT_N * FFT_N:
            fm = _fourier_fft(proj3, consts['fft'], nb)
        else:
            fm = _fourier(proj3, consts['bcs'], ct, nb, t, min(t, 256))

        x = _merge(x, g1, proj, fm.reshape(m, FOU_W), o_gla.reshape(2, m, GLA_V), od.reshape(m, DIFF_V),
                   p['w_fou'], p['w_gla_o'], p['w_diff_o'], p['w_out'], p['gla_norm'], consts['grp128'],
                   t, tm)
        x = _ffn(x, sc2, sh2, g2, p['norm2'], p['w_ff_gate'], p['w_ff_up'], p['w_ff_down'], t, tm)

        if not is_latent:
            k_out.append(outs[4].reshape(nb, t, DIFF_HEADS, 2, DIFF_HEAD_DIM))
            v_out.append(proj3[:, :, COL_VD:COL_VD + DIFF_V].astype(F32).reshape(nb, t, DIFF_HEADS, DIFF_V_HEAD))
            s_out.append(_unpack_state(s_fin))
    return x.reshape(nb, t, D_MODEL), k_out, v_out, s_out


def kernel(x_prompt, x_sample, c, cache_diff_k, cache_diff_v, state_gla, c_ctx, w_mod, b_mod, norm1, norm2, w_in, w_gla_a2, b_gla_a, gla_norm, diff_qk_norm, diff_lambda, diff_norm, w_fou, w_gla_o, w_diff_o, w_out, w_ff_gate, w_ff_up, w_ff_down):
    nb_ctx, t_ctx, _ = x_prompt.shape
    nb_lat, t_lat, _ = x_sample.shape

    n_cond = 1 + nb_lat
    r_pad = -(-n_cond // 8) * 8
    cond = jnp.zeros((r_pad, D_MODEL), F32).at[0].set(c_ctx).at[1:n_cond].set(c)
    mods = _modulation(cond, w_mod, b_mod)
    mods_ctx = [mods[l, 0:1] for l in range(DEPTH)]
    mods_lat = [mods[l, 1:n_cond] for l in range(DEPTH)]

    a0 = 2048
    a1 = a0 + 2 * GATE_RANK
    pieces = {
        'u_f': (0, 512), 'q_g': (512, 768), 'k_g': (768, 1024), 'v_g': (1024, 1536), 'r_g': (1536, 2048),
        'q_d': (a1, a1 + 512), 'k_d': (a1 + 512, a1 + 1024), 'v_d': (a1 + 1024, a1 + 1536),
        'gates': (a1 + 1536, a1 + 1536 + 3072),
    }
    order = ['gates', 'u_f', 'v_g', 'r_g', 'v_d', 'q_g', 'k_g', 'q_d', 'k_d']
    w_main = jnp.concatenate([w_in[:, :, pieces[n][0]:pieces[n][1]] for n in order], axis=-1).astype(BF16)
    w_a = jnp.pad(w_in[:, :, a0:a1], ((0, 0), (0, 0), (0, A_PAD - 2 * GATE_RANK))).astype(BF16)
    w2 = jnp.zeros((DEPTH, A_PAD, 2 * GLA_K), F32)
    w2 = w2.at[:, 0:GATE_RANK, 0:GLA_K].set(w_gla_a2[:, 0])
    w2 = w2.at[:, GATE_RANK:2 * GATE_RANK, GLA_K:].set(w_gla_a2[:, 1])
    b2 = b_gla_a.reshape(DEPTH, 1, 2 * GLA_K)

    perm = _rotate_half_perm()
    reps = DIFF_QK // DIFF_HEAD_DIM
    layers = []
    for l in range(DEPTH):
        gq = diff_qk_norm[l, 0]
        gk = diff_qk_norm[l, 1]
        layers.append({
            'norm1': norm1[l][None, :], 'norm2': norm2[l][None, :],
            'w_main': w_main[l], 'w_a': w_a[l], 'w2': w2[l], 'b2': b2[l],
            'qk_gains': tuple(jnp.tile(v, reps)[None, :] for v in (gq, gk, gq[perm], gk[perm])),
            'diff_lambda': diff_lambda[l], 'diff_norm': diff_norm[l][None, :],
            'gla_norm': jnp.tile(gla_norm[l], GLA_HEADS)[None, :],
            'w_fou': w_fou[l].astype(BF16), 'w_gla_o': w_gla_o[l].astype(BF16),
            'w_diff_o': w_diff_o[l].astype(BF16), 'w_out': w_out[l].astype(BF16),
            'w_ff_gate': w_ff_gate[l].astype(BF16), 'w_ff_up': w_ff_up[l].astype(BF16),
            'w_ff_down': w_ff_down[l].astype(BF16),
        })

    mst, hmy, xq, pm, vbd = _gla_tables()
    consts = {
        'grp64': jnp.asarray(_group_mean_matrix(DIFF_QK, DIFF_HEAD_DIM), BF16),
        'grp128': jnp.asarray(_group_mean_matrix(GLA_V, GLA_DV), BF16),
        'rot': jnp.asarray(_rotate_half_matrix(), BF16),
        'rope': _rope_tables(t_lat),
        'ones': jnp.ones((t_ctx, DIFF_QK), F32),
        'bcs': jnp.asarray(_channel_dft_table(), BF16),
        'ct': {tt: (None if tt == FFT_N * FFT_N else _dft_tables(tt)) for tt in {t_ctx, t_lat}},
        'fft': _fft_tables(),
        'gla': (jnp.asarray(mst, BF16), jnp.asarray(hmy, BF16), jnp.asarray(xq, BF16), jnp.asarray(pm),
                jnp.asarray(vbd, BF16)),
    }

    y_prompt, k_list, v_list, s_list = _run_path(x_prompt, mods_ctx, layers, consts, nb_ctx, t_ctx, None)
    y_sample, _, _, _ = _run_path(x_sample, mods_lat, layers, consts, nb_lat, t_lat,
                                  (state_gla, cache_diff_k, cache_diff_v))
    return (y_prompt, y_sample, jnp.stack(k_list, axis=1), jnp.stack(v_list, axis=1),
            jnp.stack(s_list, axis=1))
```

```python
import functools
import math

import numpy as np
import jax
import jax.numpy as jnp
from jax import lax
from jax.experimental import pallas as pl
from jax.experimental.pallas import tpu as pltpu

F32 = jnp.float32
BF16 = jnp.bfloat16

D_MODEL = 1024
DEPTH = 4
GRID_W = 64
FOU_GROUPS = 4
FOU_GROUP_W = 128
FOU_W = 512
GLA_HEADS = 4
GLA_DK = 64
GLA_DV = 128
GLA_K = 256
GLA_V = 512
GATE_RANK = 16
GATE_TEMP = 16.0
DIFF_HEADS = 4
DIFF_HEAD_DIM = 64
DIFF_V_HEAD = 128
DIFF_QK = 512
DIFF_V = 512
ROPE_AXIS_DIM = 32
ROPE_FREQS = 16
ROPE_BASE = 10000.0
N_BRANCH = 3
D_FF = 2816
EPS = 1e-6

COL_GATES = 0
COL_UF = 3072
COL_VG = 3584
COL_RG = 4096
COL_VD = 4608
COL_QG = 5120
COL_KG = 5376
PROJ_W = 5632
W_COLS = PROJ_W + 2 * 512
A_PAD = 128

GLA_C = 64
GLA_LEVELS = (32, 16, 8, 4, 2, 1)
GLA_MROWS = 64 * (1 + len(GLA_LEVELS)) + 8
GLA_BB = 8
VMEM_LIMIT = 48 * 1024 * 1024
VMEM_LIMIT_BIG = 56 * 1024 * 1024
LOG2_E = math.log2(math.e)


def _params(sem, vmem=VMEM_LIMIT):
    return pltpu.CompilerParams(dimension_semantics=sem, vmem_limit_bytes=vmem)


def _dot(a, b):
    return jnp.dot(a, b, preferred_element_type=F32)


def _dot_nt(a, b):
    return lax.dot_general(a, b, (((1,), (1,)), ((), ())), preferred_element_type=F32)


def _dot_tn(a, b):
    return lax.dot_general(a, b, (((0,), (0,)), ((), ())), preferred_element_type=F32)


def _split3(x):
    x1 = x.astype(BF16)
    r1 = x - x1.astype(F32)
    x2 = r1.astype(BF16)
    x3 = (r1 - x2.astype(F32)).astype(BF16)
    return x1, x2, x3


def _mod_kernel(c_ref, w_ref, b_ref, o_ref):
    c = c_ref[...]
    s = c * jax.nn.sigmoid(c)
    s1, s2, s3 = _split3(s)
    w = w_ref[0]
    w1, w2, w3 = _split3(w)
    acc = _dot(s1, w1) + (_dot(s1, w2) + _dot(s2, w1)) + (_dot(s2, w2) + _dot(s1, w3) + _dot(s3, w1))
    o_ref[0] = acc + b_ref[0]


def _modulation(cond, w_mod, b_mod):
    r = cond.shape[0]
    tn = 768
    n = 6 * D_MODEL
    return pl.pallas_call(
        _mod_kernel,
        grid=(DEPTH, n // tn),
        in_specs=[
            pl.BlockSpec((r, D_MODEL), lambda l, j: (0, 0)),
            pl.BlockSpec((1, D_MODEL, tn), lambda l, j: (l, 0, j)),
            pl.BlockSpec((1, 1, tn), lambda l, j: (l, 0, j)),
        ],
        out_specs=pl.BlockSpec((1, r, tn), lambda l, j: (l, 0, j)),
        out_shape=jax.ShapeDtypeStruct((DEPTH, r, n), F32),
        compiler_params=_params(("arbitrary", "arbitrary")),
        name="modulation",
    )(cond, w_mod, b_mod.reshape(DEPTH, 1, n))


def _inproj_kernel(tn, use_rope, want_f32, *refs):
    (x_ref, sc_ref, sh_ref, gain_ref, w_ref, wa_ref, w2_ref, b2_ref,
     gq_ref, gk_ref, gqp_ref, gkp_ref, grp_ref, rot_ref, cos_ref, sin_ref) = refs[:16]
    proj_ref, g_ref, qn_ref, kn_ref = refs[16:20]
    kf_ref = refs[20] if want_f32 else None
    h_scr = refs[-1]

    x = x_ref[...]
    ms = jnp.mean(x * x, axis=-1, keepdims=True)
    scale = gain_ref[...] * (1.0 + sc_ref[0])
    hb = (x * lax.rsqrt(ms + EPS) * scale + sh_ref[0]).astype(BF16)
    h_scr[...] = hb
    a = _dot(hb, wa_ref[...])
    a1, a2, a3 = _split3(a)
    v1, v2, v3 = _split3(w2_ref[...])
    z = (_dot(a1, v1) + (_dot(a1, v2) + _dot(a2, v1))
         + (_dot(a2, v2) + _dot(a1, v3) + _dot(a3, v1))) + b2_ref[...]
    logsig = jnp.minimum(z, 0.0) - jnp.log1p(jnp.exp(-jnp.abs(z)))
    g_ref[...] = logsig * (1.0 / GATE_TEMP)
    def prep(col, g_ref_, gp_ref_, out_scale):
        y = _dot(h_scr[...], w_ref[:, col:col + DIFF_QK])
        msq = _dot((y * y).astype(BF16), grp_ref[...])
        r = lax.rsqrt(msq + EPS)
        out = y * r * g_ref_[...]
        if use_rope:
            y1 = y.astype(BF16)
            y2 = (y - y1.astype(F32)).astype(BF16)
            rot = _dot(y1, rot_ref[...]) + _dot(y2, rot_ref[...])
            out = out * cos_ref[...] + rot * r * gp_ref_[...] * sin_ref[...]
        return out * out_scale

    n_tiles = PROJ_W // tn
    for j in range(n_tiles):
        proj_ref[:, j * tn:(j + 1) * tn] = _dot(h_scr[...], w_ref[:, j * tn:(j + 1) * tn]).astype(BF16)
        if j == n_tiles // 4:
            qn_ref[...] = prep(PROJ_W, gq_ref, gqp_ref, DIFF_HEAD_DIM ** -0.5 * LOG2_E).astype(BF16)
        if j == (3 * n_tiles) // 4:
            kn = prep(PROJ_W + DIFF_QK, gk_ref, gkp_ref, 1.0)
            kn_ref[0] = kn.astype(BF16)
            if want_f32:
                kf_ref[...] = kn


def _in_projection(x, sc, sh, gain, w_main, w_a, w2, b2, qk_gains, grp, rot, cos, sin,
                   nb, t, use_rope, want_f32, tm):
    m = x.shape[0]
    tn = 512
    per = t // tm
    ncond = sc.shape[0]
    cidx = (lambda i: (i // per, 0, 0)) if ncond > 1 else (lambda i: (0, 0, 0))
    const = lambda i: (0, 0)
    once = pl.Buffered(1)
    vec = pl.BlockSpec((1, DIFF_QK), const)
    mat = pl.BlockSpec((DIFF_QK, DIFF_QK), const, pipeline_mode=once)
    tab = pl.BlockSpec((tm, DIFF_QK), lambda i: (i % per, 0))
    out_specs = [
        pl.BlockSpec((tm, PROJ_W), lambda i: (i, 0)),
        pl.BlockSpec((tm, 2 * GLA_K), lambda i: (i, 0)),
        pl.BlockSpec((tm, DIFF_QK), lambda i: (i, 0)),
        pl.BlockSpec((1, tm, DIFF_QK), lambda i: (i // per, i % per, 0)),
    ]
    out_shape = [
        jax.ShapeDtypeStruct((m, PROJ_W), BF16),
        jax.ShapeDtypeStruct((m, 2 * GLA_K), F32),
        jax.ShapeDtypeStruct((m, DIFF_QK), BF16),
        jax.ShapeDtypeStruct((nb, t, DIFF_QK), BF16),
    ]
    if want_f32:
        out_specs.append(pl.BlockSpec((tm, DIFF_QK), lambda i: (i, 0)))
        out_shape.append(jax.ShapeDtypeStruct((m, DIFF_QK), F32))
    gq, gk, gqp, gkp = qk_gains
    return pl.pallas_call(
        functools.partial(_inproj_kernel, tn, use_rope, want_f32),
        grid=(m // tm,),
        in_specs=[
            pl.BlockSpec((tm, D_MODEL), lambda i: (i, 0)),
            pl.BlockSpec((1, 1, D_MODEL), cidx),
            pl.BlockSpec((1, 1, D_MODEL), cidx),
            pl.BlockSpec((1, D_MODEL), const),
            pl.BlockSpec((D_MODEL, W_COLS), const, pipeline_mode=once),
            pl.BlockSpec((D_MODEL, A_PAD), const, pipeline_mode=once),
            pl.BlockSpec((A_PAD, 2 * GLA_K), const, pipeline_mode=once),
            pl.BlockSpec((1, 2 * GLA_K), const),
            vec, vec, vec, vec, mat, mat, tab, tab,
        ],
        out_specs=out_specs,
        out_shape=out_shape,
        scratch_shapes=[pltpu.VMEM((tm, D_MODEL), BF16)],
        compiler_params=_params(("arbitrary",), VMEM_LIMIT_BIG),
        name="in_projection",
    )(x, sc, sh, gain, w_main, w_a, w2, b2, gq, gk, gqp, gkp, grp, rot, cos, sin)


def _diffattn_kernel(seg_len, lam_init, tq, *refs):
    n_seg = len(seg_len)
    q_ref = refs[0]
    kv_refs = refs[1:1 + 2 * n_seg]
    lam_ref, gain_ref, o_ref = refs[1 + 2 * n_seg:4 + 2 * n_seg]
    kz0, kz1, vaug, s_a, s_b, m_a, m_b = refs[4 + 2 * n_seg:]
    nq = q_ref.shape[1] // tq

    off = 0
    for s, tk in enumerate(seg_len):
        k = kv_refs[2 * s][0]
        lane = lax.broadcasted_iota(jnp.int32, k.shape, 1)
        zero = jnp.zeros_like(k)
        kz0[off:off + tk, :] = jnp.where(lane < DIFF_HEAD_DIM, k, zero)
        kz1[off:off + tk, :] = jnp.where(lane >= DIFF_HEAD_DIM, k, zero)
        v = kv_refs[2 * s + 1][0]
        vaug[off:off + tk, 0:DIFF_V_HEAD] = v
        vaug[off:off + tk, DIFF_V_HEAD:] = jnp.ones_like(v)
        off += tk

    lp = lam_ref[...]
    lam = (jnp.exp(jnp.sum(lp[0:1] * lp[1:2], axis=-1, keepdims=True))
           - jnp.exp(jnp.sum(lp[2:3] * lp[3:4], axis=-1, keepdims=True)) + lam_init)

    def rows(tile):
        return pl.ds(pl.multiple_of(tile * tq, tq), tq)

    def score(tile, s_write, m_write):
        q = q_ref[0, rows(tile), :]
        for w, kz in enumerate((kz0, kz1)):
            sc = _dot_nt(q, kz[...])
            s_write[w] = sc
            m_write[w] = jnp.broadcast_to(sc.max(axis=-1, keepdims=True), m_write.shape[1:])

    def finish(tile, s_read, m_read):
        outs = []
        for w in range(2):
            p = jnp.exp2(s_read[w] - m_read[w][:, 0:1]).astype(BF16)
            acc = _dot(p, vaug[...])
            outs.append(acc[:, 0:DIFF_V_HEAD] / acc[:, DIFF_V_HEAD:])
        o = outs[0] - lam * outs[1]
        ms = jnp.mean(o * o, axis=-1, keepdims=True)
        o = o * lax.rsqrt(ms + EPS) * gain_ref[...] * (1.0 - lam_init)
        o_ref[0, rows(tile), :] = o.astype(BF16)

    score(0, s_a, m_a)
    n_pairs = (nq - 1) // 2

    def pair(j, carry):
        score(2 * j + 1, s_b, m_b)
        finish(2 * j, s_a, m_a)
        score(2 * j + 2, s_a, m_a)
        finish(2 * j + 1, s_b, m_b)
        return carry

    lax.fori_loop(0, n_pairs, pair, 0)
    done = 2 * n_pairs
    if (nq - 1) % 2 == 1:
        score(done + 1, s_b, m_b)
        finish(done, s_a, m_a)
        finish(done + 1, s_b, m_b)
    else:
        finish(done, s_a, m_a)


def _diff_attention(qn, segments, lam_p, gain, lam_init, nb, tq_total, tq):
    hd = 2 * DIFF_HEAD_DIM
    in_specs = [pl.BlockSpec((1, tq_total, hd), lambda b, h: (b, 0, h))]
    args = [qn]
    seg_len = tuple(seg[4] for seg in segments)
    tk_all = sum(seg_len)
    for (ka, kc, va, vc, tk) in segments:
        in_specs.append(pl.BlockSpec((1, tk, hd), lambda b, h, kc=kc: (b, 0, kc + h)))
        in_specs.append(pl.BlockSpec((1, tk, DIFF_V_HEAD), lambda b, h, vc=vc: (b, 0, vc + h)))
        args += [ka, va]
    in_specs.append(pl.BlockSpec((4, DIFF_HEAD_DIM), lambda b, h: (0, 0)))
    in_specs.append(pl.BlockSpec((1, DIFF_V_HEAD), lambda b, h: (0, 0)))
    args += [lam_p, gain]
    scratch = [
        pltpu.VMEM((tk_all, hd), BF16), pltpu.VMEM((tk_all, hd), BF16),
        pltpu.VMEM((tk_all, 2 * DIFF_V_HEAD), BF16),
        pltpu.VMEM((2, tq, tk_all), F32), pltpu.VMEM((2, tq, tk_all), F32),
        pltpu.VMEM((2, tq, hd), F32), pltpu.VMEM((2, tq, hd), F32),
    ]
    return pl.pallas_call(
        functools.partial(_diffattn_kernel, seg_len, lam_init, tq),
        grid=(nb, DIFF_HEADS),
        in_specs=in_specs,
        out_specs=pl.BlockSpec((1, tq_total, DIFF_V_HEAD), lambda b, h: (b, 0, h)),
        out_shape=jax.ShapeDtypeStruct((nb, tq_total, DIFF_V), BF16),
        scratch_shapes=scratch,
        compiler_params=_params(("arbitrary", "arbitrary")),
        name="diff_attention",
    )(*args)


def _gla_tables():
    c = GLA_C
    nl = len(GLA_LEVELS)
    mst = np.zeros((2, GLA_MROWS, c), np.float32)
    xm = np.zeros((2, nl + 1, c), np.float32)
    pm = np.zeros((2, nl + 1, c, c), np.float32)
    idx = np.arange(c)
    for d in range(2):
        ip = idx if d == 0 else c - 1 - idx
        mst[d, 0:c] = (ip[None, :] <= ip[:, None])
        mst[d, c * (nl + 1)] = 1.0
        pm[d, 0] = np.eye(c)
        xm[d, 0] = 1.0
        for li, s in enumerate(GLA_LEVELS):
            ref = (ip // (2 * s)) * (2 * s) + s - 1
            mst[d, c * (li + 1):c * (li + 2)] = (ip[None, :] <= ref[:, None])
            odd = (ip // s) % 2 == 1
            xm[d, li + 1] = odd
            same = (ip[:, None] // (2 * s)) == (ip[None, :] // (2 * s))
            pm[d, li + 1] = odd[:, None] & (~odd)[None, :] & same
    pm = np.tile(pm, (1, 1, 1, GLA_HEADS))
    rows = np.arange(GLA_HEADS * c)[:, None] // c
    hm = (rows == (np.arange(GLA_K)[None, :] // GLA_DK)).astype(np.float32)
    ykeep = np.concatenate([np.ones((2, 1, c), np.float32), 1.0 - xm[:, 1:]], axis=1)
    hmy = hm[None, None] * np.tile(ykeep, (1, 1, GLA_HEADS))[:, :, :, None]
    xq = np.broadcast_to(xm[:, 1:, :, None], (2, nl, c, GLA_K)).copy()
    vbd = (rows == (np.arange(GLA_V)[None, :] // GLA_DV)).astype(np.float32)
    return mst, hmy, xq, pm, vbd


def _gla_kernel(bb, q_ref, k_ref, v_ref, g_ref, mst_ref, hmy_ref, xq_ref, pm_ref, vbd_ref, s0_ref,
                o_ref, sfin_ref, s_scr):
    c_idx = pl.program_id(2)

    @pl.when(c_idx == 0)
    def _():
        s_scr[...] = s0_ref[:, 0]

    c = GLA_C
    nl = len(GLA_LEVELS)
    m = mst_ref[0]
    seqs = range(bb)

    def stack_keys(yb, li):
        return jnp.concatenate([yb] * GLA_HEADS, axis=0) * hmy_ref[0, li]

    r_all = []
    for b in seqs:
        g = g_ref[b]
        g1 = g.astype(BF16)
        g2 = (g - g1.astype(F32)).astype(BF16)
        r_all.append(_dot(m, g1) + _dot(m, g2))
    cum = [r[0:c] for r in r_all]
    tot = [r[c * (nl + 1):c * (nl + 1) + 1] for r in r_all]
    q = [q_ref[b].astype(F32) * (GLA_DK ** -0.5) for b in seqs]
    k = [k_ref[b].astype(F32) for b in seqs]

    o_inter = []
    for b in seqs:
        state = s_scr[b]
        q_dec = stack_keys((q[b] * jnp.exp(cum[b])).astype(BF16), 0)
        o_rows = _dot_nt(q_dec, state.astype(BF16))
        o_inter.append(jnp.concatenate([o_rows[h * c:(h + 1) * c] for h in range(GLA_HEADS)], axis=1))
        k_dec = stack_keys((k[b] * jnp.exp(tot[b] - cum[b])).astype(BF16), 0)
        vf = v_ref[b].astype(F32)
        v_t = jnp.concatenate(
            [jnp.concatenate([vf[:, (2 * p) * GLA_DV:(2 * p + 1) * GLA_DV],
                              vf[:, (2 * p + 1) * GLA_DV:(2 * p + 2) * GLA_DV]], axis=0).T
             for p in range(GLA_HEADS // 2)], axis=1).astype(BF16)
        s_scr[b] = state * jnp.exp(tot[b]) + _dot(v_t, k_dec)

    @pl.when(c_idx == pl.num_programs(2) - 1)
    def _():
        sfin_ref[:, 0] = s_scr[...]

    att = [pm_ref[0, 0] * _dot_nt(q[b].astype(BF16), stack_keys(k_ref[b], 0)) for b in seqs]
    for li in range(nl):
        for b in seqs:
            ref = r_all[b][c * (li + 1):c * (li + 2)]
            e = jnp.exp(-jnp.abs(cum[b] - ref))
            xs = (q[b] * e).astype(BF16) * xq_ref[0, li]
            ys = stack_keys((k[b] * e).astype(BF16), li + 1)
            att[b] = att[b] + pm_ref[0, li + 1] * _dot_nt(xs, ys)
    for b in seqs:
        v_bd = jnp.concatenate([v_ref[b]] * GLA_HEADS, axis=0) * vbd_ref[...]
        o_ref[0, b] = o_inter[b] + _dot(att[b].astype(BF16), v_bd)


def _gla(proj3, g3, s0, tables, nb, t, bb):
    mst, hmy, xq, pm, vbd = tables
    nc = t // GLA_C
    nl = len(GLA_LEVELS)

    def chunk(d, c):
        return jnp.where(d == 0, c, nc - 1 - c)

    return pl.pallas_call(
        functools.partial(_gla_kernel, bb),
        grid=(nb // bb, 2, nc),
        in_specs=[
            pl.BlockSpec((bb, GLA_C, GLA_K), lambda b, d, c: (b, chunk(d, c), COL_QG // GLA_K)),
            pl.BlockSpec((bb, GLA_C, GLA_K), lambda b, d, c: (b, chunk(d, c), COL_KG // GLA_K)),
            pl.BlockSpec((bb, GLA_C, GLA_V), lambda b, d, c: (b, chunk(d, c), COL_VG // GLA_V)),
            pl.BlockSpec((bb, GLA_C, GLA_K), lambda b, d, c: (b, chunk(d, c), d)),
            pl.BlockSpec((1, GLA_MROWS, GLA_C), lambda b, d, c: (d, 0, 0)),
            pl.BlockSpec((1, nl + 1, GLA_HEADS * GLA_C, GLA_K), lambda b, d, c: (d, 0, 0, 0)),
            pl.BlockSpec((1, nl, GLA_C, GLA_K), lambda b, d, c: (d, 0, 0, 0)),
            pl.BlockSpec((1, nl + 1, GLA_C, GLA_HEADS * GLA_C), lambda b, d, c: (d, 0, 0, 0)),
            pl.BlockSpec((GLA_HEADS * GLA_C, GLA_V), lambda b, d, c: (0, 0)),
            pl.BlockSpec((bb, 1, GLA_DV, GLA_K), lambda b, d, c: (b, d, 0, 0)),
        ],
        out_specs=[
            pl.BlockSpec((1, bb, GLA_C, GLA_V), lambda b, d, c: (d, b, chunk(d, c), 0)),
            pl.BlockSpec((bb, 1, GLA_DV, GLA_K), lambda b, d, c: (b, d, 0, 0)),
        ],
        out_shape=[
            jax.ShapeDtypeStruct((2, nb, t, GLA_V), F32),
            jax.ShapeDtypeStruct((nb, 2, GLA_DV, GLA_K), F32),
        ],
        scratch_shapes=[pltpu.VMEM((bb, GLA_DV, GLA_K), F32)],
        compiler_params=_params(("arbitrary", "arbitrary", "arbitrary")),
        name="gla",
    )(proj3, proj3, proj3, g3, mst, hmy, xq, pm, vbd, s0)


def _fourier_kernel(t, u_ref, bcs_ref, ct_ref, o_ref, y_scr):
    i = pl.program_id(1)

    @pl.when(i == 0)
    def _():
        rows = min(t, 512)
        for r in range(t // rows):
            y = _dot(u_ref[0, r * rows:(r + 1) * rows, :], bcs_ref[...])
            y_scr[r * rows:(r + 1) * rows, :] = y[:, :FOU_W].astype(BF16)
            y_scr[t + r * rows:t + (r + 1) * rows, :] = y[:, FOU_W:].astype(BF16)

    o_ref[0] = _dot(ct_ref[...], y_scr[...]).astype(BF16)


def _fourier(proj3, bcs, ct, nb, t, tm):
    return pl.pallas_call(
        functools.partial(_fourier_kernel, t),
        grid=(nb, t // tm),
        in_specs=[
            pl.BlockSpec((1, t, FOU_W), lambda b, i: (b, 0, COL_UF // FOU_W)),
            pl.BlockSpec((FOU_W, 2 * FOU_W), lambda b, i: (0, 0)),
            pl.BlockSpec((tm, 2 * t), lambda b, i: (i, 0)),
        ],
        out_specs=pl.BlockSpec((1, tm, FOU_W), lambda b, i: (b, i, 0)),
        out_shape=jax.ShapeDtypeStruct((nb, t, FOU_W), BF16),
        scratch_shapes=[pltpu.VMEM((2 * t, FOU_W), BF16)],
        compiler_params=_params(("arbitrary", "arbitrary")),
        name="fourier_mix",
    )(proj3, bcs, ct)


FFT_N = 64
FFT_G = 8
FFT_CW = 256


def _fft_kernel(u_ref, bcs_ref, m1_ref, q2_ref, o_ref, z_scr, a_scr):
    n, gsz, cw = FFT_N, FFT_G, FFT_CW
    rows = n * gsz
    for r in range(n // gsz):
        y = _dot(u_ref[0, r * rows:(r + 1) * rows, :], bcs_ref[...])
        z_scr[gsz * r:gsz * (r + 1)] = y.reshape(gsz, n, 2 * cw)
    for g in range(n // gsz):
        zg = z_scr[:, gsz * g:gsz * (g + 1), :].reshape(rows, 2 * cw)
        rhs = jnp.concatenate([zg[:, :cw], zg[:, cw:]], axis=0).astype(BF16)
        a_scr[g] = _dot(m1_ref[...], rhs).astype(BF16)
    for f in range(n // gsz):
        slabs = [a_scr[g, part * rows + n * f:part * rows + n * (f + 1), :]
                 for part in range(2) for g in range(n // gsz)]
        res = _dot(q2_ref[f], jnp.concatenate(slabs, axis=0))
        o_ref[0, :, gsz * f:gsz * (f + 1), :] = res.reshape(n, gsz, cw)


def _fft_tables():
    n, gsz, t = FFT_N, FFT_G, FFT_N * FFT_N
    k = np.arange(FOU_GROUP_W)
    ang = 2.0 * np.pi * ((k[:, None] * k[None, :]) % FOU_GROUP_W) / FOU_GROUP_W
    eye2 = np.eye(FFT_CW // FOU_GROUP_W)
    bcs = np.concatenate([np.kron(eye2, np.cos(ang)), -np.kron(eye2, np.sin(ang))], axis=1) * FOU_GROUP_W ** -0.5
    i = np.arange(n)
    a1 = 2.0 * np.pi * ((i[:, None] * i[None, :]) % n) / n
    c1 = np.kron(np.cos(a1), np.eye(gsz))
    s1 = np.kron(np.sin(a1), np.eye(gsz))
    m1 = np.block([[c1, s1], [-s1, c1]])
    ngrp = n // gsz
    shape = (ngrp, n * gsz, n * gsz)
    f = lax.broadcasted_iota(jnp.int32, shape, 0)
    row = lax.broadcasted_iota(jnp.int32, shape, 1)
    col = lax.broadcasted_iota(jnp.int32, shape, 2)
    f2, a = row // gsz, row % gsz
    g, ap, j = col // (gsz * gsz), (col // gsz) % gsz, col % gsz
    freq = gsz * f + a + n * f2
    theta = (((gsz * g + j) * freq) % t).astype(F32) * (2.0 * math.pi / t)
    keep = (a == ap)
    scale = t ** -0.5
    qc = jnp.where(keep, jnp.cos(theta) * scale, 0.0)
    qs = jnp.where(keep, jnp.sin(theta) * scale, 0.0)
    q2 = jnp.concatenate([qc, qs], axis=2).astype(BF16)
    return jnp.asarray(bcs, BF16), jnp.asarray(m1, BF16), q2


def _fourier_fft(proj3, tables, nb):
    bcs, m1, q2 = tables
    n, gsz, cw = FFT_N, FFT_G, FFT_CW
    t = n * n
    once = pl.Buffered(1)
    out = pl.pallas_call(
        _fft_kernel,
        grid=(nb, FOU_W // cw),
        in_specs=[
            pl.BlockSpec((1, t, cw), lambda b, h: (b, 0, COL_UF // cw + h)),
            pl.BlockSpec((cw, 2 * cw), lambda b, h: (0, 0), pipeline_mode=once),
            pl.BlockSpec((2 * n * gsz, 2 * n * gsz), lambda b, h: (0, 0), pipeline_mode=once),
            pl.BlockSpec((n // gsz, n * gsz, 2 * n * gsz), lambda b, h: (0, 0, 0), pipeline_mode=once),
        ],
        out_specs=pl.BlockSpec((1, n, n, cw), lambda b, h: (b, 0, 0, h)),
        out_shape=jax.ShapeDtypeStruct((nb, n, n, FOU_W), F32),
        scratch_shapes=[pltpu.VMEM((n, n, 2 * cw), F32), pltpu.VMEM((n // gsz, 2 * n * gsz, cw), BF16)],
        compiler_params=_params(("arbitrary", "arbitrary")),
        name="fourier_fft",
    )(proj3, bcs, m1, q2)
    return out.reshape(nb, t, FOU_W)


def _dft_tables(t):
    lo = min(t, 64)
    hi = t // lo
    f = jnp.arange(t, dtype=jnp.int32)[None, :]
    unit = 2.0 * math.pi / t
    a = ((lo * jnp.arange(hi, dtype=jnp.int32)[:, None] * f) % t).astype(F32) * unit
    b = ((jnp.arange(lo, dtype=jnp.int32)[:, None] * f) % t).astype(F32) * unit
    ca, sa = jnp.cos(a)[:, None, :], jnp.sin(a)[:, None, :]
    cb, sb = jnp.cos(b)[None, :, :], jnp.sin(b)[None, :, :]
    scale = t ** -0.5
    cos = ((ca * cb - sa * sb) * scale).reshape(t, t)
    msin = ((sa * cb + ca * sb) * -scale).reshape(t, t)
    return jnp.concatenate([cos, msin], axis=1).astype(BF16)


def _channel_dft_table():
    n = np.arange(FOU_GROUP_W)
    ang = 2.0 * np.pi * ((n[:, None] * n[None, :]) % FOU_GROUP_W) / FOU_GROUP_W
    eye = np.eye(FOU_GROUPS)
    bc = np.kron(eye, np.cos(ang)) * FOU_GROUP_W ** -0.5
    bs = np.kron(eye, np.sin(ang)) * FOU_GROUP_W ** -0.5
    return np.concatenate([bc, bs], axis=1).astype(np.float32)


def _merge_kernel(x_ref, g1_ref, fm_ref, of_ref, ob_ref, r_ref, od_ref, gt_ref,
                  wf_ref, wg_ref, wd_ref, wo_ref, gn_ref, grp_ref, o_ref):
    y_f = _dot(fm_ref[...].astype(BF16), wf_ref[...])
    og = of_ref[0] + ob_ref[0]
    ms = _dot((og * og).astype(BF16), grp_ref[...])
    r = r_ref[...].astype(F32)
    og = og * lax.rsqrt(ms + EPS) * gn_ref[...] * (r * jax.nn.sigmoid(r))
    y_g = _dot(og.astype(BF16), wg_ref[...])
    y_d = _dot(od_ref[...], wd_ref[...])
    gates = jax.nn.sigmoid(gt_ref[...].astype(F32))
    merged = (gates[:, 0:D_MODEL] * y_f + gates[:, D_MODEL:2 * D_MODEL] * y_g
              + gates[:, 2 * D_MODEL:3 * D_MODEL] * y_d)
    o_ref[...] = x_ref[...] + g1_ref[0] * _dot(merged.astype(BF16), wo_ref[...])


def _merge(x, g1, proj, fm, o_gla, od, w_fou, w_gla_o, w_diff_o, w_out, gn, grp, tokens_per_cond, tm):
    m = x.shape[0]
    per = tokens_per_cond // tm
    ncond = g1.shape[0]
    cidx = (lambda i: (i // per, 0, 0)) if ncond > 1 else (lambda i: (0, 0, 0))
    once = pl.Buffered(1)
    half = pl.BlockSpec((FOU_W, D_MODEL), lambda i: (0, 0), pipeline_mode=once)
    return pl.pallas_call(
        _merge_kernel,
        grid=(m // tm,),
        in_specs=[
            pl.BlockSpec((tm, D_MODEL), lambda i: (i, 0)),
            pl.BlockSpec((1, 1, D_MODEL), cidx),
            pl.BlockSpec((tm, FOU_W), lambda i: (i, 0)),
            pl.BlockSpec((1, tm, GLA_V), lambda i: (0, i, 0)),
            pl.BlockSpec((1, tm, GLA_V), lambda i: (1, i, 0)),
            pl.BlockSpec((tm, GLA_V), lambda i: (i, COL_RG // GLA_V)),
            pl.BlockSpec((tm, DIFF_V), lambda i: (i, 0)),
            pl.BlockSpec((tm, N_BRANCH * D_MODEL), lambda i: (i, COL_GATES)),
            half, half, half,
            pl.BlockSpec((D_MODEL, D_MODEL), lambda i: (0, 0), pipeline_mode=once),
            pl.BlockSpec((1, GLA_V), lambda i: (0, 0)),
            pl.BlockSpec((GLA_V, GLA_V), lambda i: (0, 0)),
        ],
        out_specs=pl.BlockSpec((tm, D_MODEL), lambda i: (i, 0)),
        out_shape=jax.ShapeDtypeStruct((m, D_MODEL), F32),
        compiler_params=_params(("arbitrary",)),
        name="merge",
    )(x, g1, fm, o_gla, o_gla, proj, od, proj, w_fou, w_gla_o, w_diff_o, w_out, gn, grp)


def _ffn_kernel(tf, x_ref, sc_ref, sh_ref, g2_ref, gain_ref, wg_ref, wu_ref, wd_ref, o_ref, h_scr):
    x = x_ref[...]
    ms = jnp.mean(x * x, axis=-1, keepdims=True)
    scale = gain_ref[...] * (1.0 + sc_ref[0])
    h_scr[...] = (x * lax.rsqrt(ms + EPS) * scale + sh_ref[0]).astype(BF16)
    acc = None
    for f in range(D_FF // tf):
        cols = slice(f * tf, (f + 1) * tf)
        gate = _dot(h_scr[...], wg_ref[:, cols])
        up = _dot(h_scr[...], wu_ref[:, cols])
        a = (gate * jax.nn.sigmoid(gate) * up).astype(BF16)
        part = _dot(a, wd_ref[cols, :])
        acc = part if acc is None else acc + part
    o_ref[...] = x_ref[...] + g2_ref[0] * acc


def _ffn(x, sc, sh, g2, gain, w_gate, w_up, w_down, tokens_per_cond, tm):
    m = x.shape[0]
    tf = 256
    per = tokens_per_cond // tm
    ncond = sc.shape[0]
    cidx = (lambda i: (i // per, 0, 0)) if ncond > 1 else (lambda i: (0, 0, 0))
    const = lambda i: (0, 0)
    once = pl.Buffered(1)
    return pl.pallas_call(
        functools.partial(_ffn_kernel, tf),
        grid=(m // tm,),
        in_specs=[
            pl.BlockSpec((tm, D_MODEL), lambda i: (i, 0)),
            pl.BlockSpec((1, 1, D_MODEL), cidx),
            pl.BlockSpec((1, 1, D_MODEL), cidx),
            pl.BlockSpec((1, 1, D_MODEL), cidx),
            pl.BlockSpec((1, D_MODEL), const),
            pl.BlockSpec((D_MODEL, D_FF), const, pipeline_mode=once),
            pl.BlockSpec((D_MODEL, D_FF), const, pipeline_mode=once),
            pl.BlockSpec((D_FF, D_MODEL), const, pipeline_mode=once),
        ],
        out_specs=pl.BlockSpec((tm, D_MODEL), lambda i: (i, 0)),
        out_shape=jax.ShapeDtypeStruct((m, D_MODEL), F32),
        scratch_shapes=[pltpu.VMEM((tm, D_MODEL), BF16)],
        compiler_params=_params(("arbitrary",)),
        name="ffn",
    )(x, sc, sh, g2, gain, w_gate, w_up, w_down)


def _group_mean_matrix(width, group):
    idx = np.arange(width) // group
    return (idx[:, None] == idx[None, :]).astype(np.float32) / group


def _rotate_half_matrix():
    r = np.zeros((DIFF_HEAD_DIM, DIFF_HEAD_DIM), np.float32)
    for axis in range(2):
        base = axis * ROPE_AXIS_DIM
        for f in range(ROPE_FREQS):
            r[base + ROPE_FREQS + f, base + f] = -1.0
            r[base + f, base + ROPE_FREQS + f] = 1.0
    return np.kron(np.eye(DIFF_QK // DIFF_HEAD_DIM), r).astype(np.float32)


def _rotate_half_perm():
    p = np.zeros((DIFF_HEAD_DIM,), np.int32)
    for axis in range(2):
        base = axis * ROPE_AXIS_DIM
        for f in range(ROPE_FREQS):
            p[base + f] = base + ROPE_FREQS + f
            p[base + ROPE_FREQS + f] = base + f
    return p


def _rope_tables(n_tokens):
    rows = n_tokens // GRID_W
    row = jnp.repeat(jnp.arange(rows), GRID_W).astype(F32)
    col = jnp.tile(jnp.arange(GRID_W), rows).astype(F32)
    inv = ROPE_BASE ** (-jnp.arange(ROPE_FREQS, dtype=F32) * 2.0 / ROPE_AXIS_DIM)
    ang_r = row[:, None] * inv
    ang_c = col[:, None] * inv
    ang = jnp.concatenate([ang_r, ang_r, ang_c, ang_c], axis=-1)
    reps = DIFF_QK // DIFF_HEAD_DIM
    return jnp.tile(jnp.cos(ang), (1, reps)), jnp.tile(jnp.sin(ang), (1, reps))


def _pack_state(s):
    return jnp.transpose(s, (0, 1, 4, 2, 3)).reshape(s.shape[0], 2, GLA_DV, GLA_K)


def _unpack_state(sp):
    s = sp.reshape(sp.shape[0], 2, GLA_DV, GLA_HEADS, GLA_DK)
    return jnp.transpose(s, (0, 1, 3, 4, 2))


def _run_path(x, mods, layers, consts, nb, t, ctx):
    m = nb * t
    tm = 512 if t >= 512 else t
    x = x.reshape(m, D_MODEL)
    is_latent = ctx is not None
    cos, sin = consts['rope'] if is_latent else (consts['ones'], consts['ones'])
    ct = consts['ct'][t]
    k_out, v_out, s_out = [], [], []
    for l, p in enumerate(layers):
        sh1, sc1, g1, sh2, sc2, g2 = [mods[l][:, None, j * D_MODEL:(j + 1) * D_MODEL] for j in range(6)]
        outs = _in_projection(x, sc1, sh1, p['norm1'], p['w_main'], p['w_a'], p['w2'], p['b2'],
                              p['qk_gains'], consts['grp64'], consts['rot'], cos, sin,
                              nb, t, is_latent, not is_latent, tm)
        proj, g, qn, kn = outs[:4]
        proj3 = proj.reshape(nb, t, PROJ_W)
        lam_init = 0.8 - 0.6 * math.exp(-0.3 * l)
        segments = [(kn, 0, proj3, COL_VD // DIFF_V_HEAD, t)]
        if is_latent:
            ck = ctx[1][:, l].reshape(nb, -1, DIFF_QK).astype(BF16)
            cv = ctx[2][:, l].reshape(nb, -1, DIFF_V).astype(BF16)
            segments.append((ck, 0, cv, 0, ck.shape[1]))
        od = _diff_attention(qn.reshape(nb, t, DIFF_QK), segments, p['diff_lambda'], p['diff_norm'],
                             lam_init, nb, t, min(t, 256))

        if is_latent:
            s0 = _pack_state(ctx[0][:, l])
        else:
            s0 = jnp.zeros((nb, 2, GLA_DV, GLA_K), F32)
        o_gla, s_fin = _gla(proj3, g.reshape(nb, t, 2 * GLA_K), s0, consts['gla'], nb, t, math.gcd(nb, GLA_BB))

        if t == FFT_N * FFT_N:
            fm = _fourier_fft(proj3, consts['fft'], nb)
        else:
            fm = _fourier(proj3, consts['bcs'], ct, nb, t, min(t, 256))

        x = _merge(x, g1, proj, fm.reshape(m, FOU_W), o_gla.reshape(2, m, GLA_V), od.reshape(m, DIFF_V),
                   p['w_fou'], p['w_gla_o'], p['w_diff_o'], p['w_out'], p['gla_norm'], consts['grp128'],
                   t, tm)
        x = _ffn(x, sc2, sh2, g2, p['norm2'], p['w_ff_gate'], p['w_ff_up'], p['w_ff_down'], t, tm)

        if not is_latent:
            k_out.append(outs[4].reshape(nb, t, DIFF_HEADS, 2, DIFF_HEAD_DIM))
            v_out.append(proj3[:, :, COL_VD:COL_VD + DIFF_V].astype(F32).reshape(nb, t, DIFF_HEADS, DIFF_V_HEAD))
            s_out.append(_unpack_state(s_fin))
    return x.reshape(nb, t, D_MODEL), k_out, v_out, s_out


def kernel(x_prompt, x_sample, c, cache_diff_k, cache_diff_v, state_gla, c_ctx, w_mod, b_mod, norm1, norm2, w_in, w_gla_a2, b_gla_a, gla_norm, diff_qk_norm, diff_lambda, diff_norm, w_fou, w_gla_o, w_diff_o, w_out, w_ff_gate, w_ff_up, w_ff_down):
    nb_ctx, t_ctx, _ = x_prompt.shape
    nb_lat, t_lat, _ = x_sample.shape

    n_cond = 1 + nb_lat
    r_pad = -(-n_cond // 8) * 8
    cond = jnp.zeros((r_pad, D_MODEL), F32).at[0].set(c_ctx).at[1:n_cond].set(c)
    mods = _modulation(cond, w_mod, b_mod)
    mods_ctx = [mods[l, 0:1] for l in range(DEPTH)]
    mods_lat = [mods[l, 1:n_cond] for l in range(DEPTH)]

    a0 = 2048
    a1 = a0 + 2 * GATE_RANK
    pieces = {
        'u_f': (0, 512), 'q_g': (512, 768), 'k_g': (768, 1024), 'v_g': (1024, 1536), 'r_g': (1536, 2048),
        'q_d': (a1, a1 + 512), 'k_d': (a1 + 512, a1 + 1024), 'v_d': (a1 + 1024, a1 + 1536),
        'gates': (a1 + 1536, a1 + 1536 + 3072),
    }
    order = ['gates', 'u_f', 'v_g', 'r_g', 'v_d', 'q_g', 'k_g', 'q_d', 'k_d']
    w_main = jnp.concatenate([w_in[:, :, pieces[n][0]:pieces[n][1]] for n in order], axis=-1).astype(BF16)
    w_a = jnp.pad(w_in[:, :, a0:a1], ((0, 0), (0, 0), (0, A_PAD - 2 * GATE_RANK))).astype(BF16)
    w2 = jnp.zeros((DEPTH, A_PAD, 2 * GLA_K), F32)
    w2 = w2.at[:, 0:GATE_RANK, 0:GLA_K].set(w_gla_a2[:, 0])
    w2 = w2.at[:, GATE_RANK:2 * GATE_RANK, GLA_K:].set(w_gla_a2[:, 1])
    b2 = b_gla_a.reshape(DEPTH, 1, 2 * GLA_K)

    perm = _rotate_half_perm()
    reps = DIFF_QK // DIFF_HEAD_DIM
    layers = []
    for l in range(DEPTH):
        gq = diff_qk_norm[l, 0]
        gk = diff_qk_norm[l, 1]
        layers.append({
            'norm1': norm1[l][None, :], 'norm2': norm2[l][None, :],
            'w_main': w_main[l], 'w_a': w_a[l], 'w2': w2[l], 'b2': b2[l],
            'qk_gains': tuple(jnp.tile(v, reps)[None, :] for v in (gq, gk, gq[perm], gk[perm])),
            'diff_lambda': diff_lambda[l], 'diff_norm': diff_norm[l][None, :],
            'gla_norm': jnp.tile(gla_norm[l], GLA_HEADS)[None, :],
            'w_fou': w_fou[l].astype(BF16), 'w_gla_o': w_gla_o[l].astype(BF16),
            'w_diff_o': w_diff_o[l].astype(BF16), 'w_out': w_out[l].astype(BF16),
            'w_ff_gate': w_ff_gate[l].astype(BF16), 'w_ff_up': w_ff_up[l].astype(BF16),
            'w_ff_down': w_ff_down[l].astype(BF16),
        })

    mst, hmy, xq, pm, vbd = _gla_tables()
    consts = {
        'grp64': jnp.asarray(_group_mean_matrix(DIFF_QK, DIFF_HEAD_DIM), BF16),
        'grp128': jnp.asarray(_group_mean_matrix(GLA_V, GLA_DV), BF16),
        'rot': jnp.asarray(_rotate_half_matrix(), BF16),
        'rope': _rope_tables(t_lat),
        'ones': jnp.ones((t_ctx, DIFF_QK), F32),
        'bcs': jnp.asarray(_channel_dft_table(), BF16),
        'ct': {tt: (None if tt == FFT_N * FFT_N else _dft_tables(tt)) for tt in {t_ctx, t_lat}},
        'fft': _fft_tables(),
        'gla': (jnp.asarray(mst, BF16), jnp.asarray(hmy, BF16), jnp.asarray(xq, BF16), jnp.asarray(pm),
                jnp.asarray(vbd, BF16)),
    }

    y_prompt, k_list, v_list, s_list = _run_path(x_prompt, mods_ctx, layers, consts, nb_ctx, t_ctx, None)
    y_sample, _, _, _ = _run_path(x_sample, mods_lat, layers, consts, nb_lat, t_lat,
                                  (state_gla, cache_diff_k, cache_diff_v))
    return (y_prompt, y_sample, jnp.stack(k_list, axis=1), jnp.stack(v_list, axis=1),
            jnp.stack(s_list, axis=1))
```

```python
import functools
import math

import numpy as np
import jax
import jax.numpy as jnp
from jax import lax
from jax.experimental import pallas as pl
from jax.experimental.pallas import tpu as pltpu

F32 = jnp.float32
BF16 = jnp.bfloat16

D_MODEL = 1024
DEPTH = 4
GRID_W = 64
FOU_GROUPS = 4
FOU_GROUP_W = 128
FOU_W = 512
GLA_HEADS = 4
GLA_DK = 64
GLA_DV = 128
GLA_K = 256
GLA_V = 512
GATE_RANK = 16
GATE_TEMP = 16.0
DIFF_HEADS = 4
DIFF_HEAD_DIM = 64
DIFF_V_HEAD = 128
DIFF_QK = 512
DIFF_V = 512
ROPE_AXIS_DIM = 32
ROPE_FREQS = 16
ROPE_BASE = 10000.0
N_BRANCH = 3
D_FF = 2816
EPS = 1e-6

COL_GATES = 0
COL_UF = 3072
COL_VG = 3584
COL_RG = 4096
COL_VD = 4608
COL_QG = 5120
COL_KG = 5376
PROJ_W = 5632
W_COLS = PROJ_W + 2 * 512
A_PAD = 128

GLA_C = 64
GLA_LEVELS = (32, 16, 8, 4, 2, 1)
GLA_MROWS = 64 * (1 + len(GLA_LEVELS)) + 8
GLA_BB = 8
VMEM_LIMIT = 48 * 1024 * 1024
VMEM_LIMIT_BIG = 56 * 1024 * 1024
LOG2_E = math.log2(math.e)


def _params(sem, vmem=VMEM_LIMIT):
    return pltpu.CompilerParams(dimension_semantics=sem, vmem_limit_bytes=vmem)


def _dot(a, b):
    return jnp.dot(a, b, preferred_element_type=F32)


def _dot_nt(a, b):
    return lax.dot_general(a, b, (((1,), (1,)), ((), ())), preferred_element_type=F32)


def _dot_tn(a, b):
    return lax.dot_general(a, b, (((0,), (0,)), ((), ())), preferred_element_type=F32)


def _split2(x):
    x1 = x.astype(BF16)
    return x1, (x - x1.astype(F32)).astype(BF16)


def _split3(x):
    x1 = x.astype(BF16)
    r1 = x - x1.astype(F32)
    x2 = r1.astype(BF16)
    x3 = (r1 - x2.astype(F32)).astype(BF16)
    return x1, x2, x3


def _mod_kernel(c_ref, w_ref, b_ref, o_ref):
    c = c_ref[...]
    s = c * jax.nn.sigmoid(c)
    s1, s2, s3 = _split3(s)
    w = w_ref[0]
    w1, w2, w3 = _split3(w)
    acc = _dot(s1, w1) + (_dot(s1, w2) + _dot(s2, w1)) + (_dot(s2, w2) + _dot(s1, w3) + _dot(s3, w1))
    o_ref[0] = acc + b_ref[0]


def _modulation(cond, w_mod, b_mod):
    r = cond.shape[0]
    tn = 768
    n = 6 * D_MODEL
    return pl.pallas_call(
        _mod_kernel,
        grid=(DEPTH, n // tn),
        in_specs=[
            pl.BlockSpec((r, D_MODEL), lambda l, j: (0, 0)),
            pl.BlockSpec((1, D_MODEL, tn), lambda l, j: (l, 0, j)),
            pl.BlockSpec((1, 1, tn), lambda l, j: (l, 0, j)),
        ],
        out_specs=pl.BlockSpec((1, r, tn), lambda l, j: (l, 0, j)),
        out_shape=jax.ShapeDtypeStruct((DEPTH, r, n), F32),
        compiler_params=_params(("arbitrary", "arbitrary")),
        name="modulation",
    )(cond, w_mod, b_mod.reshape(DEPTH, 1, n))


def _inproj_kernel(tn, use_rope, want_f32, *refs):
    (x_ref, sc_ref, sh_ref, gain_ref, w_ref, wa_ref, w2_ref, b2_ref,
     gq_ref, gk_ref, gqp_ref, gkp_ref, grp_ref, rot_ref, cos_ref, sin_ref) = refs[:16]
    proj_ref, g_ref, qn_ref, kn_ref = refs[16:20]
    kf_ref = refs[20] if want_f32 else None
    h_scr = refs[-1]

    x = x_ref[...]
    ms = jnp.mean(x * x, axis=-1, keepdims=True)
    scale = gain_ref[...] * (1.0 + sc_ref[0])
    hb = (x * lax.rsqrt(ms + EPS) * scale + sh_ref[0]).astype(BF16)
    h_scr[...] = hb

    def tile(j):
        proj_ref[:, j * tn:(j + 1) * tn] = _dot(h_scr[...], w_ref[:, j * tn:(j + 1) * tn]).astype(BF16)

    a = _dot(hb, wa_ref[...])
    y_q = _dot(h_scr[...], w_ref[:, PROJ_W:PROJ_W + DIFF_QK])
    y_k = _dot(h_scr[...], w_ref[:, PROJ_W + DIFF_QK:PROJ_W + 2 * DIFF_QK])
    tile(0)
    a1, a2 = _split2(a)
    v1, v2 = _split2(w2_ref[...])
    yb_q, yb_k = y_q.astype(BF16), y_k.astype(BF16)
    sq_q, sq_k = (y_q * y_q).astype(BF16), (y_k * y_k).astype(BF16)
    z = _dot(a1, v1) + (_dot(a1, v2) + _dot(a2, v1)) + b2_ref[...]
    ms_q = _dot(sq_q, grp_ref[...])
    ms_k = _dot(sq_k, grp_ref[...])
    if use_rope:
        rot_q = _dot(yb_q, rot_ref[...])
        rot_k = _dot(yb_k, rot_ref[...])
    tile(1)
    tile(2)
    logsig = jnp.minimum(z, 0.0) - jnp.log1p(jnp.exp(-jnp.abs(z)))
    g_ref[...] = logsig * (1.0 / GATE_TEMP)

    def normed(y, msq, rot, g_ref_, gp_ref_, out_scale):
        r = lax.rsqrt(msq + EPS)
        out = y * r * g_ref_[...]
        if use_rope:
            out = out * cos_ref[...] + rot * r * gp_ref_[...] * sin_ref[...]
        return out * out_scale

    tile(3)
    qn_ref[...] = normed(y_q, ms_q, rot_q if use_rope else None, gq_ref, gqp_ref,
                         DIFF_HEAD_DIM ** -0.5 * LOG2_E).astype(BF16)
    tile(4)
    kn = normed(y_k, ms_k, rot_k if use_rope else None, gk_ref, gkp_ref, 1.0)
    kn_ref[0] = kn.astype(BF16)
    if want_f32:
        kf_ref[...] = kn
    for j in range(5, PROJ_W // tn):
        tile(j)


def _in_projection(x, sc, sh, gain, w_main, w_a, w2, b2, qk_gains, grp, rot, cos, sin,
                   nb, t, use_rope, want_f32, tm):
    m = x.shape[0]
    tn = 512
    per = t // tm
    ncond = sc.shape[0]
    cidx = (lambda i: (i // per, 0, 0)) if ncond > 1 else (lambda i: (0, 0, 0))
    const = lambda i: (0, 0)
    once = pl.Buffered(1)
    vec = pl.BlockSpec((1, DIFF_QK), const)
    mat = pl.BlockSpec((DIFF_QK, DIFF_QK), const, pipeline_mode=once)
    tab = pl.BlockSpec((tm, DIFF_QK), lambda i: (i % per, 0))
    out_specs = [
        pl.BlockSpec((tm, PROJ_W), lambda i: (i, 0)),
        pl.BlockSpec((tm, 2 * GLA_K), lambda i: (i, 0)),
        pl.BlockSpec((tm, DIFF_QK), lambda i: (i, 0)),
        pl.BlockSpec((1, tm, DIFF_QK), lambda i: (i // per, i % per, 0)),
    ]
    out_shape = [
        jax.ShapeDtypeStruct((m, PROJ_W), BF16),
        jax.ShapeDtypeStruct((m, 2 * GLA_K), F32),
        jax.ShapeDtypeStruct((m, DIFF_QK), BF16),
        jax.ShapeDtypeStruct((nb, t, DIFF_QK), BF16),
    ]
    if want_f32:
        out_specs.append(pl.BlockSpec((tm, DIFF_QK), lambda i: (i, 0)))
        out_shape.append(jax.ShapeDtypeStruct((m, DIFF_QK), F32))
    gq, gk, gqp, gkp = qk_gains
    return pl.pallas_call(
        functools.partial(_inproj_kernel, tn, use_rope, want_f32),
        grid=(m // tm,),
        in_specs=[
            pl.BlockSpec((tm, D_MODEL), lambda i: (i, 0)),
            pl.BlockSpec((1, 1, D_MODEL), cidx),
            pl.BlockSpec((1, 1, D_MODEL), cidx),
            pl.BlockSpec((1, D_MODEL), const),
            pl.BlockSpec((D_MODEL, W_COLS), const, pipeline_mode=once),
            pl.BlockSpec((D_MODEL, A_PAD), const, pipeline_mode=once),
            pl.BlockSpec((A_PAD, 2 * GLA_K), const, pipeline_mode=once),
            pl.BlockSpec((1, 2 * GLA_K), const),
            vec, vec, vec, vec, mat, mat, tab, tab,
        ],
        out_specs=out_specs,
        out_shape=out_shape,
        scratch_shapes=[pltpu.VMEM((tm, D_MODEL), BF16)],
        compiler_params=_params(("arbitrary",), VMEM_LIMIT_BIG),
        name="in_projection",
    )(x, sc, sh, gain, w_main, w_a, w2, b2, gq, gk, gqp, gkp, grp, rot, cos, sin)


def _diffattn_kernel(seg_len, lam_init, tq, *refs):
    n_seg = len(seg_len)
    q_ref = refs[0]
    kv_refs = refs[1:1 + 2 * n_seg]
    lam_ref, gain_ref, o_ref = refs[1 + 2 * n_seg:4 + 2 * n_seg]
    kz0, kz1, vaug, s_a, s_b, m_a, m_b = refs[4 + 2 * n_seg:]
    nq = q_ref.shape[1] // tq

    off = 0
    for s, tk in enumerate(seg_len):
        k = kv_refs[2 * s][0]
        lane = lax.broadcasted_iota(jnp.int32, k.shape, 1)
        zero = jnp.zeros_like(k)
        kz0[off:off + tk, :] = jnp.where(lane < DIFF_HEAD_DIM, k, zero)
        kz1[off:off + tk, :] = jnp.where(lane >= DIFF_HEAD_DIM, k, zero)
        v = kv_refs[2 * s + 1][0]
        vaug[off:off + tk, 0:DIFF_V_HEAD] = v
        vaug[off:off + tk, DIFF_V_HEAD:] = jnp.ones_like(v)
        off += tk

    lp = lam_ref[...]
    lam = (jnp.exp(jnp.sum(lp[0:1] * lp[1:2], axis=-1, keepdims=True))
           - jnp.exp(jnp.sum(lp[2:3] * lp[3:4], axis=-1, keepdims=True)) + lam_init)

    def rows(tile):
        return pl.ds(pl.multiple_of(tile * tq, tq), tq)

    def score(tile, s_write, m_write):
        q = q_ref[0, rows(tile), :]
        for w, kz in enumerate((kz0, kz1)):
            sc = _dot_nt(q, kz[...])
            s_write[w] = sc
            m_write[w] = jnp.broadcast_to(sc.max(axis=-1, keepdims=True), m_write.shape[1:])

    def finish(tile, s_read, m_read):
        outs = []
        for w in range(2):
            p = jnp.exp2(s_read[w] - m_read[w][:, 0:1]).astype(BF16)
            acc = _dot(p, vaug[...])
            outs.append(acc[:, 0:DIFF_V_HEAD] / acc[:, DIFF_V_HEAD:])
        o = outs[0] - lam * outs[1]
        ms = jnp.mean(o * o, axis=-1, keepdims=True)
        o = o * lax.rsqrt(ms + EPS) * gain_ref[...] * (1.0 - lam_init)
        o_ref[0, rows(tile), :] = o.astype(BF16)

    score(0, s_a, m_a)
    n_pairs = (nq - 1) // 2

    def pair(j, carry):
        score(2 * j + 1, s_b, m_b)
        finish(2 * j, s_a, m_a)
        score(2 * j + 2, s_a, m_a)
        finish(2 * j + 1, s_b, m_b)
        return carry

    lax.fori_loop(0, n_pairs, pair, 0)
    done = 2 * n_pairs
    if (nq - 1) % 2 == 1:
        score(done + 1, s_b, m_b)
        finish(done, s_a, m_a)
        finish(done + 1, s_b, m_b)
    else:
        finish(done, s_a, m_a)


def _diff_attention(qn, segments, lam_p, gain, lam_init, nb, tq_total, tq):
    hd = 2 * DIFF_HEAD_DIM
    in_specs = [pl.BlockSpec((1, tq_total, hd), lambda b, h: (b, 0, h))]
    args = [qn]
    seg_len = tuple(seg[4] for seg in segments)
    tk_all = sum(seg_len)
    for (ka, kc, va, vc, tk) in segments:
        in_specs.append(pl.BlockSpec((1, tk, hd), lambda b, h, kc=kc: (b, 0, kc + h)))
        in_specs.append(pl.BlockSpec((1, tk, DIFF_V_HEAD), lambda b, h, vc=vc: (b, 0, vc + h)))
        args += [ka, va]
    in_specs.append(pl.BlockSpec((4, DIFF_HEAD_DIM), lambda b, h: (0, 0)))
    in_specs.append(pl.BlockSpec((1, DIFF_V_HEAD), lambda b, h: (0, 0)))
    args += [lam_p, gain]
    scratch = [
        pltpu.VMEM((tk_all, hd), BF16), pltpu.VMEM((tk_all, hd), BF16),
        pltpu.VMEM((tk_all, 2 * DIFF_V_HEAD), BF16),
        pltpu.VMEM((2, tq, tk_all), F32), pltpu.VMEM((2, tq, tk_all), F32),
        pltpu.VMEM((2, tq, hd), F32), pltpu.VMEM((2, tq, hd), F32),
    ]
    return pl.pallas_call(
        functools.partial(_diffattn_kernel, seg_len, lam_init, tq),
        grid=(nb, DIFF_HEADS),
        in_specs=in_specs,
        out_specs=pl.BlockSpec((1, tq_total, DIFF_V_HEAD), lambda b, h: (b, 0, h)),
        out_shape=jax.ShapeDtypeStruct((nb, tq_total, DIFF_V), BF16),
        scratch_shapes=scratch,
        compiler_params=_params(("arbitrary", "arbitrary")),
        name="diff_attention",
    )(*args)


def _gla_tables():
    c = GLA_C
    nl = len(GLA_LEVELS)
    mst = np.zeros((2, GLA_MROWS, c), np.float32)
    xm = np.zeros((2, nl + 1, c), np.float32)
    pm = np.zeros((2, nl + 1, c, c), np.float32)
    idx = np.arange(c)
    for d in range(2):
        ip = idx if d == 0 else c - 1 - idx
        mst[d, 0:c] = (ip[None, :] <= ip[:, None])
        mst[d, c * (nl + 1)] = 1.0
        pm[d, 0] = np.eye(c)
        xm[d, 0] = 1.0
        for li, s in enumerate(GLA_LEVELS):
            ref = (ip // (2 * s)) * (2 * s) + s - 1
            mst[d, c * (li + 1):c * (li + 2)] = (ip[None, :] <= ref[:, None])
            odd = (ip // s) % 2 == 1
            xm[d, li + 1] = odd
            same = (ip[:, None] // (2 * s)) == (ip[None, :] // (2 * s))
            pm[d, li + 1] = odd[:, None] & (~odd)[None, :] & same
    pm = np.tile(pm, (1, 1, 1, GLA_HEADS))
    rows = np.arange(GLA_HEADS * c)[:, None] // c
    hm = (rows == (np.arange(GLA_K)[None, :] // GLA_DK)).astype(np.float32)
    ykeep = np.concatenate([np.ones((2, 1, c), np.float32), 1.0 - xm[:, 1:]], axis=1)
    hmy = hm[None, None] * np.tile(ykeep, (1, 1, GLA_HEADS))[:, :, :, None]
    xq = np.broadcast_to(xm[:, 1:, :, None], (2, nl, c, GLA_K)).copy()
    vbd = (rows == (np.arange(GLA_V)[None, :] // GLA_DV)).astype(np.float32)
    return mst, hmy, xq, pm, vbd


def _gla_kernel(bb, q_ref, k_ref, v_ref, g_ref, mst_ref, hmy_ref, xq_ref, pm_ref, vbd_ref, s0_ref,
                o_ref, sfin_ref, s_scr):
    c_idx = pl.program_id(2)

    @pl.when(c_idx == 0)
    def _():
        s_scr[...] = s0_ref[:, 0]

    c = GLA_C
    nl = len(GLA_LEVELS)
    m = mst_ref[0]
    seqs = range(bb)

    def stack_keys(yb, li):
        return jnp.concatenate([yb] * GLA_HEADS, axis=0) * hmy_ref[0, li]

    r_all = []
    for b in seqs:
        g1, g2 = _split2(g_ref[b])
        r_all.append(_dot(m, g1) + _dot(m, g2))
    cum = [r[0:c] for r in r_all]
    tot = [r[c * (nl + 1):c * (nl + 1) + 1] for r in r_all]
    q = [q_ref[b].astype(F32) * (GLA_DK ** -0.5) for b in seqs]
    k = [k_ref[b].astype(F32) for b in seqs]

    o_inter = []
    for b in seqs:
        state = s_scr[b]
        q_dec = stack_keys((q[b] * jnp.exp(cum[b])).astype(BF16), 0)
        o_rows = _dot_nt(q_dec, state.astype(BF16))
        o_inter.append(jnp.concatenate([o_rows[h * c:(h + 1) * c] for h in range(GLA_HEADS)], axis=1))
        k_dec = stack_keys((k[b] * jnp.exp(tot[b] - cum[b])).astype(BF16), 0)
        vf = v_ref[b].astype(F32)
        v_t = jnp.concatenate(
            [jnp.concatenate([vf[:, (2 * p) * GLA_DV:(2 * p + 1) * GLA_DV],
                              vf[:, (2 * p + 1) * GLA_DV:(2 * p + 2) * GLA_DV]], axis=0).T
             for p in range(GLA_HEADS // 2)], axis=1).astype(BF16)
        s_scr[b] = state * jnp.exp(tot[b]) + _dot(v_t, k_dec)

    @pl.when(c_idx == pl.num_programs(2) - 1)
    def _():
        sfin_ref[:, 0] = s_scr[...]

    att = [pm_ref[0, 0] * _dot_nt(q[b].astype(BF16), stack_keys(k_ref[b], 0)) for b in seqs]
    for li in range(nl):
        for b in seqs:
            ref = r_all[b][c * (li + 1):c * (li + 2)]
            e = jnp.exp(-jnp.abs(cum[b] - ref))
            xs = (q[b] * e).astype(BF16) * xq_ref[0, li]
            ys = stack_keys((k[b] * e).astype(BF16), li + 1)
            att[b] = att[b] + pm_ref[0, li + 1] * _dot_nt(xs, ys)
    for b in seqs:
        v_bd = jnp.concatenate([v_ref[b]] * GLA_HEADS, axis=0) * vbd_ref[...]
        o_ref[0, b] = o_inter[b] + _dot(att[b].astype(BF16), v_bd)


def _gla(proj3, g3, s0, tables, nb, t, bb):
    mst, hmy, xq, pm, vbd = tables
    nc = t // GLA_C
    nl = len(GLA_LEVELS)

    def chunk(d, c):
        return jnp.where(d == 0, c, nc - 1 - c)

    return pl.pallas_call(
        functools.partial(_gla_kernel, bb),
        grid=(nb // bb, 2, nc),
        in_specs=[
            pl.BlockSpec((bb, GLA_C, GLA_K), lambda b, d, c: (b, chunk(d, c), COL_QG // GLA_K)),
            pl.BlockSpec((bb, GLA_C, GLA_K), lambda b, d, c: (b, chunk(d, c), COL_KG // GLA_K)),
            pl.BlockSpec((bb, GLA_C, GLA_V), lambda b, d, c: (b, chunk(d, c), COL_VG // GLA_V)),
            pl.BlockSpec((bb, GLA_C, GLA_K), lambda b, d, c: (b, chunk(d, c), d)),
            pl.BlockSpec((1, GLA_MROWS, GLA_C), lambda b, d, c: (d, 0, 0)),
            pl.BlockSpec((1, nl + 1, GLA_HEADS * GLA_C, GLA_K), lambda b, d, c: (d, 0, 0, 0)),
            pl.BlockSpec((1, nl, GLA_C, GLA_K), lambda b, d, c: (d, 0, 0, 0)),
            pl.BlockSpec((1, nl + 1, GLA_C, GLA_HEADS * GLA_C), lambda b, d, c: (d, 0, 0, 0)),
            pl.BlockSpec((GLA_HEADS * GLA_C, GLA_V), lambda b, d, c: (0, 0)),
            pl.BlockSpec((bb, 1, GLA_DV, GLA_K), lambda b, d, c: (b, d, 0, 0)),
        ],
        out_specs=[
            pl.BlockSpec((1, bb, GLA_C, GLA_V), lambda b, d, c: (d, b, chunk(d, c), 0)),
            pl.BlockSpec((bb, 1, GLA_DV, GLA_K), lambda b, d, c: (b, d, 0, 0)),
        ],
        out_shape=[
            jax.ShapeDtypeStruct((2, nb, t, GLA_V), F32),
            jax.ShapeDtypeStruct((nb, 2, GLA_DV, GLA_K), F32),
        ],
        scratch_shapes=[pltpu.VMEM((bb, GLA_DV, GLA_K), F32)],
        compiler_params=_params(("arbitrary", "arbitrary", "arbitrary")),
        name="gla",
    )(proj3, proj3, proj3, g3, mst, hmy, xq, pm, vbd, s0)


def _fourier_kernel(t, u_ref, bcs_ref, ct_ref, o_ref, y_scr):
    i = pl.program_id(1)

    @pl.when(i == 0)
    def _():
        rows = min(t, 512)
        for r in range(t // rows):
            y = _dot(u_ref[0, r * rows:(r + 1) * rows, :], bcs_ref[...])
            y_scr[r * rows:(r + 1) * rows, :] = y[:, :FOU_W].astype(BF16)
            y_scr[t + r * rows:t + (r + 1) * rows, :] = y[:, FOU_W:].astype(BF16)

    o_ref[0] = _dot(ct_ref[...], y_scr[...]).astype(BF16)


def _fourier(proj3, bcs, ct, nb, t, tm):
    return pl.pallas_call(
        functools.partial(_fourier_kernel, t),
        grid=(nb, t // tm),
        in_specs=[
            pl.BlockSpec((1, t, FOU_W), lambda b, i: (b, 0, COL_UF // FOU_W)),
            pl.BlockSpec((FOU_W, 2 * FOU_W), lambda b, i: (0, 0)),
            pl.BlockSpec((tm, 2 * t), lambda b, i: (i, 0)),
        ],
        out_specs=pl.BlockSpec((1, tm, FOU_W), lambda b, i: (b, i, 0)),
        out_shape=jax.ShapeDtypeStruct((nb, t, FOU_W), BF16),
        scratch_shapes=[pltpu.VMEM((2 * t, FOU_W), BF16)],
        compiler_params=_params(("arbitrary", "arbitrary")),
        name="fourier_mix",
    )(proj3, bcs, ct)


FFT_N = 64
FFT_G = 8
FFT_CW = 256


def _fft_kernel(u_ref, bcs_ref, m1_ref, q2_ref, o_ref, z_scr, a_scr):
    n, gsz, cw = FFT_N, FFT_G, FFT_CW
    rows = n * gsz
    for r in range(n // gsz):
        y = _dot(u_ref[0, r * rows:(r + 1) * rows, :], bcs_ref[...])
        z_scr[gsz * r:gsz * (r + 1)] = y.reshape(gsz, n, 2 * cw)
    for g in range(n // gsz):
        zg = z_scr[:, gsz * g:gsz * (g + 1), :].reshape(rows, 2 * cw)
        rhs = jnp.concatenate([zg[:, :cw], zg[:, cw:]], axis=0).astype(BF16)
        a_scr[g] = _dot(m1_ref[...], rhs).astype(BF16)
    for f in range(n // gsz):
        slabs = [a_scr[g, part * rows + n * f:part * rows + n * (f + 1), :]
                 for part in range(2) for g in range(n // gsz)]
        res = _dot(q2_ref[f], jnp.concatenate(slabs, axis=0))
        o_ref[0, :, gsz * f:gsz * (f + 1), :] = res.reshape(n, gsz, cw)


def _fft_tables():
    n, gsz, t = FFT_N, FFT_G, FFT_N * FFT_N
    k = np.arange(FOU_GROUP_W)
    ang = 2.0 * np.pi * ((k[:, None] * k[None, :]) % FOU_GROUP_W) / FOU_GROUP_W
    eye2 = np.eye(FFT_CW // FOU_GROUP_W)
    bcs = np.concatenate([np.kron(eye2, np.cos(ang)), -np.kron(eye2, np.sin(ang))], axis=1) * FOU_GROUP_W ** -0.5
    i = np.arange(n)
    a1 = 2.0 * np.pi * ((i[:, None] * i[None, :]) % n) / n
    c1 = np.kron(np.cos(a1), np.eye(gsz))
    s1 = np.kron(np.sin(a1), np.eye(gsz))
    m1 = np.block([[c1, s1], [-s1, c1]])
    ngrp = n // gsz
    shape = (ngrp, n * gsz, n * gsz)
    f = lax.broadcasted_iota(jnp.int32, shape, 0)
    row = lax.broadcasted_iota(jnp.int32, shape, 1)
    col = lax.broadcasted_iota(jnp.int32, shape, 2)
    f2, a = row // gsz, row % gsz
    g, ap, j = col // (gsz * gsz), (col // gsz) % gsz, col % gsz
    freq = gsz * f + a + n * f2
    theta = (((gsz * g + j) * freq) % t).astype(F32) * (2.0 * math.pi / t)
    keep = (a == ap)
    scale = t ** -0.5
    qc = jnp.where(keep, jnp.cos(theta) * scale, 0.0)
    qs = jnp.where(keep, jnp.sin(theta) * scale, 0.0)
    q2 = jnp.concatenate([qc, qs], axis=2).astype(BF16)
    return jnp.asarray(bcs, BF16), jnp.asarray(m1, BF16), q2


def _fourier_fft(proj3, tables, nb):
    bcs, m1, q2 = tables
    n, gsz, cw = FFT_N, FFT_G, FFT_CW
    t = n * n
    once = pl.Buffered(1)
    out = pl.pallas_call(
        _fft_kernel,
        grid=(nb, FOU_W // cw),
        in_specs=[
            pl.BlockSpec((1, t, cw), lambda b, h: (b, 0, COL_UF // cw + h)),
            pl.BlockSpec((cw, 2 * cw), lambda b, h: (0, 0), pipeline_mode=once),
            pl.BlockSpec((2 * n * gsz, 2 * n * gsz), lambda b, h: (0, 0), pipeline_mode=once),
            pl.BlockSpec((n // gsz, n * gsz, 2 * n * gsz), lambda b, h: (0, 0, 0), pipeline_mode=once),
        ],
        out_specs=pl.BlockSpec((1, n, n, cw), lambda b, h: (b, 0, 0, h)),
        out_shape=jax.ShapeDtypeStruct((nb, n, n, FOU_W), F32),
        scratch_shapes=[pltpu.VMEM((n, n, 2 * cw), F32), pltpu.VMEM((n // gsz, 2 * n * gsz, cw), BF16)],
        compiler_params=_params(("arbitrary", "arbitrary")),
        name="fourier_fft",
    )(proj3, bcs, m1, q2)
    return out.reshape(nb, t, FOU_W)


def _dft_tables(t):
    lo = min(t, 64)
    hi = t // lo
    f = jnp.arange(t, dtype=jnp.int32)[None, :]
    unit = 2.0 * math.pi / t
    a = ((lo * jnp.arange(hi, dtype=jnp.int32)[:, None] * f) % t).astype(F32) * unit
    b = ((jnp.arange(lo, dtype=jnp.int32)[:, None] * f) % t).astype(F32) * unit
    ca, sa = jnp.cos(a)[:, None, :], jnp.sin(a)[:, None, :]
    cb, sb = jnp.cos(b)[None, :, :], jnp.sin(b)[None, :, :]
    scale = t ** -0.5
    cos = ((ca * cb - sa * sb) * scale).reshape(t, t)
    msin = ((sa * cb + ca * sb) * -scale).reshape(t, t)
    return jnp.concatenate([cos, msin], axis=1).astype(BF16)


def _channel_dft_table():
    n = np.arange(FOU_GROUP_W)
    ang = 2.0 * np.pi * ((n[:, None] * n[None, :]) % FOU_GROUP_W) / FOU_GROUP_W
    eye = np.eye(FOU_GROUPS)
    bc = np.kron(eye, np.cos(ang)) * FOU_GROUP_W ** -0.5
    bs = np.kron(eye, np.sin(ang)) * FOU_GROUP_W ** -0.5
    return np.concatenate([bc, bs], axis=1).astype(np.float32)


def _merge_kernel(x_ref, g1_ref, fm_ref, of_ref, ob_ref, r_ref, od_ref, gt_ref,
                  wf_ref, wg_ref, wd_ref, wo_ref, gn_ref, grp_ref, o_ref):
    y_f = _dot(fm_ref[...].astype(BF16), wf_ref[...])
    og = of_ref[0] + ob_ref[0]
    ms = _dot((og * og).astype(BF16), grp_ref[...])
    r = r_ref[...].astype(F32)
    og = og * lax.rsqrt(ms + EPS) * gn_ref[...] * (r * jax.nn.sigmoid(r))
    y_g = _dot(og.astype(BF16), wg_ref[...])
    y_d = _dot(od_ref[...], wd_ref[...])
    gates = jax.nn.sigmoid(gt_ref[...].astype(F32))
    merged = (gates[:, 0:D_MODEL] * y_f + gates[:, D_MODEL:2 * D_MODEL] * y_g
              + gates[:, 2 * D_MODEL:3 * D_MODEL] * y_d)
    o_ref[...] = x_ref[...] + g1_ref[0] * _dot(merged.astype(BF16), wo_ref[...])


def _merge(x, g1, proj, fm, o_gla, od, w_fou, w_gla_o, w_diff_o, w_out, gn, grp, tokens_per_cond, tm):
    m = x.shape[0]
    per = tokens_per_cond // tm
    ncond = g1.shape[0]
    cidx = (lambda i: (i // per, 0, 0)) if ncond > 1 else (lambda i: (0, 0, 0))
    once = pl.Buffered(1)
    half = pl.BlockSpec((FOU_W, D_MODEL), lambda i: (0, 0), pipeline_mode=once)
    return pl.pallas_call(
        _merge_kernel,
        grid=(m // tm,),
        in_specs=[
            pl.BlockSpec((tm, D_MODEL), lambda i: (i, 0)),
            pl.BlockSpec((1, 1, D_MODEL), cidx),
            pl.BlockSpec((tm, FOU_W), lambda i: (i, 0)),
            pl.BlockSpec((1, tm, GLA_V), lambda i: (0, i, 0)),
            pl.BlockSpec((1, tm, GLA_V), lambda i: (1, i, 0)),
            pl.BlockSpec((tm, GLA_V), lambda i: (i, COL_RG // GLA_V)),
            pl.BlockSpec((tm, DIFF_V), lambda i: (i, 0)),
            pl.BlockSpec((tm, N_BRANCH * D_MODEL), lambda i: (i, COL_GATES)),
            half, half, half,
            pl.BlockSpec((D_MODEL, D_MODEL), lambda i: (0, 0), pipeline_mode=once),
            pl.BlockSpec((1, GLA_V), lambda i: (0, 0)),
            pl.BlockSpec((GLA_V, GLA_V), lambda i: (0, 0)),
        ],
        out_specs=pl.BlockSpec((tm, D_MODEL), lambda i: (i, 0)),
        out_shape=jax.ShapeDtypeStruct((m, D_MODEL), F32),
        compiler_params=_params(("arbitrary",)),
        name="merge",
    )(x, g1, fm, o_gla, o_gla, proj, od, proj, w_fou, w_gla_o, w_diff_o, w_out, gn, grp)


def _ffn_kernel(tf, x_ref, sc_ref, sh_ref, g2_ref, gain_ref, wg_ref, wu_ref, wd_ref, o_ref, h_scr):
    x = x_ref[...]
    ms = jnp.mean(x * x, axis=-1, keepdims=True)
    scale = gain_ref[...] * (1.0 + sc_ref[0])
    h_scr[...] = (x * lax.rsqrt(ms + EPS) * scale + sh_ref[0]).astype(BF16)
    acc = None
    for f in range(D_FF // tf):
        cols = slice(f * tf, (f + 1) * tf)
        gate = _dot(h_scr[...], wg_ref[:, cols])
        up = _dot(h_scr[...], wu_ref[:, cols])
        a = (gate * jax.nn.sigmoid(gate) * up).astype(BF16)
        part = _dot(a, wd_ref[cols, :])
        acc = part if acc is None else acc + part
    o_ref[...] = x_ref[...] + g2_ref[0] * acc


def _ffn(x, sc, sh, g2, gain, w_gate, w_up, w_down, tokens_per_cond, tm):
    m = x.shape[0]
    tf = 256
    per = tokens_per_cond // tm
    ncond = sc.shape[0]
    cidx = (lambda i: (i // per, 0, 0)) if ncond > 1 else (lambda i: (0, 0, 0))
    const = lambda i: (0, 0)
    once = pl.Buffered(1)
    return pl.pallas_call(
        functools.partial(_ffn_kernel, tf),
        grid=(m // tm,),
        in_specs=[
            pl.BlockSpec((tm, D_MODEL), lambda i: (i, 0)),
            pl.BlockSpec((1, 1, D_MODEL), cidx),
            pl.BlockSpec((1, 1, D_MODEL), cidx),
            pl.BlockSpec((1, 1, D_MODEL), cidx),
            pl.BlockSpec((1, D_MODEL), const),
            pl.BlockSpec((D_MODEL, D_FF), const, pipeline_mode=once),
            pl.BlockSpec((D_MODEL, D_FF), const, pipeline_mode=once),
            pl.BlockSpec((D_FF, D_MODEL), const, pipeline_mode=once),
        ],
        out_specs=pl.BlockSpec((tm, D_MODEL), lambda i: (i, 0)),
        out_shape=jax.ShapeDtypeStruct((m, D_MODEL), F32),
        scratch_shapes=[pltpu.VMEM((tm, D_MODEL), BF16)],
        compiler_params=_params(("arbitrary",)),
        name="ffn",
    )(x, sc, sh, g2, gain, w_gate, w_up, w_down)


def _group_mean_matrix(width, group):
    idx = np.arange(width) // group
    return (idx[:, None] == idx[None, :]).astype(np.float32) / group


def _rotate_half_matrix():
    r = np.zeros((DIFF_HEAD_DIM, DIFF_HEAD_DIM), np.float32)
    for axis in range(2):
        base = axis * ROPE_AXIS_DIM
        for f in range(ROPE_FREQS):
            r[base + ROPE_FREQS + f, base + f] = -1.0
            r[base + f, base + ROPE_FREQS + f] = 1.0
    return np.kron(np.eye(DIFF_QK // DIFF_HEAD_DIM), r).astype(np.float32)


def _rotate_half_perm():
    p = np.zeros((DIFF_HEAD_DIM,), np.int32)
    for axis in range(2):
        base = axis * ROPE_AXIS_DIM
        for f in range(ROPE_FREQS):
            p[base + f] = base + ROPE_FREQS + f
            p[base + ROPE_FREQS + f] = base + f
    return p


def _rope_tables(n_tokens):
    rows = n_tokens // GRID_W
    row = jnp.repeat(jnp.arange(rows), GRID_W).astype(F32)
    col = jnp.tile(jnp.arange(GRID_W), rows).astype(F32)
    inv = ROPE_BASE ** (-jnp.arange(ROPE_FREQS, dtype=F32) * 2.0 / ROPE_AXIS_DIM)
    ang_r = row[:, None] * inv
    ang_c = col[:, None] * inv
    ang = jnp.concatenate([ang_r, ang_r, ang_c, ang_c], axis=-1)
    reps = DIFF_QK // DIFF_HEAD_DIM
    return jnp.tile(jnp.cos(ang), (1, reps)), jnp.tile(jnp.sin(ang), (1, reps))


def _pack_state(s):
    return jnp.transpose(s, (0, 1, 4, 2, 3)).reshape(s.shape[0], 2, GLA_DV, GLA_K)


def _unpack_state(sp):
    s = sp.reshape(sp.shape[0], 2, GLA_DV, GLA_HEADS, GLA_DK)
    return jnp.transpose(s, (0, 1, 3, 4, 2))


def _run_path(x, mods, layers, consts, nb, t, ctx):
    m = nb * t
    tm = 512 if t >= 512 else t
    x = x.reshape(m, D_MODEL)
    is_latent = ctx is not None
    cos, sin = consts['rope'] if is_latent else (consts['ones'], consts['ones'])
    ct = consts['ct'][t]
    k_out, v_out, s_out = [], [], []
    for l, p in enumerate(layers):
        sh1, sc1, g1, sh2, sc2, g2 = [mods[l][:, None, j * D_MODEL:(j + 1) * D_MODEL] for j in range(6)]
        outs = _in_projection(x, sc1, sh1, p['norm1'], p['w_main'], p['w_a'], p['w2'], p['b2'],
                              p['qk_gains'], consts['grp64'], consts['rot'], cos, sin,
                              nb, t, is_latent, not is_latent, tm)
        proj, g, qn, kn = outs[:4]
        proj3 = proj.reshape(nb, t, PROJ_W)
        lam_init = 0.8 - 0.6 * math.exp(-0.3 * l)
        segments = [(kn, 0, proj3, COL_VD // DIFF_V_HEAD, t)]
        if is_latent:
            ck = ctx[1][:, l].reshape(nb, -1, DIFF_QK).astype(BF16)
            cv = ctx[2][:, l].reshape(nb, -1, DIFF_V).astype(BF16)
            segments.append((ck, 0, cv, 0, ck.shape[1]))
        od = _diff_attention(qn.reshape(nb, t, DIFF_QK), segments, p['diff_lambda'], p['diff_norm'],
                             lam_init, nb, t, 256 if t > 256 else t // 2)

        if is_latent:
            s0 = _pack_state(ctx[0][:, l])
        else:
            s0 = jnp.zeros((nb, 2, GLA_DV, GLA_K), F32)
        o_gla, s_fin = _gla(proj3, g.reshape(nb, t, 2 * GLA_K), s0, consts['gla'], nb, t, math.gcd(nb, GLA_BB))

        if t == FFT_N * FFT_N:
            fm = _fourier_fft(proj3, consts['fft'], nb)
        else:
            fm = _fourier(proj3, consts['bcs'], ct, nb, t, min(t, 256))

        x = _merge(x, g1, proj, fm.reshape(m, FOU_W), o_gla.reshape(2, m, GLA_V), od.reshape(m, DIFF_V),
                   p['w_fou'], p['w_gla_o'], p['w_diff_o'], p['w_out'], p['gla_norm'], consts['grp128'],
                   t, tm)
        x = _ffn(x, sc2, sh2, g2, p['norm2'], p['w_ff_gate'], p['w_ff_up'], p['w_ff_down'], t, tm)

        if not is_latent:
            k_out.append(outs[4].reshape(nb, t, DIFF_HEADS, 2, DIFF_HEAD_DIM))
            v_out.append(proj3[:, :, COL_VD:COL_VD + DIFF_V].astype(F32).reshape(nb, t, DIFF_HEADS, DIFF_V_HEAD))
            s_out.append(_unpack_state(s_fin))
    return x.reshape(nb, t, D_MODEL), k_out, v_out, s_out


def kernel(x_prompt, x_sample, c, cache_diff_k, cache_diff_v, state_gla, c_ctx, w_mod, b_mod, norm1, norm2, w_in, w_gla_a2, b_gla_a, gla_norm, diff_qk_norm, diff_lambda, diff_norm, w_fou, w_gla_o, w_diff_o, w_out, w_ff_gate, w_ff_up, w_ff_down):
    nb_ctx, t_ctx, _ = x_prompt.shape
    nb_lat, t_lat, _ = x_sample.shape

    n_cond = 1 + nb_lat
    r_pad = -(-n_cond // 8) * 8
    cond = jnp.zeros((r_pad, D_MODEL), F32).at[0].set(c_ctx).at[1:n_cond].set(c)
    mods = _modulation(cond, w_mod, b_mod)
    mods_ctx = [mods[l, 0:1] for l in range(DEPTH)]
    mods_lat = [mods[l, 1:n_cond] for l in range(DEPTH)]

    a0 = 2048
    a1 = a0 + 2 * GATE_RANK
    pieces = {
        'u_f': (0, 512), 'q_g': (512, 768), 'k_g': (768, 1024), 'v_g': (1024, 1536), 'r_g': (1536, 2048),
        'q_d': (a1, a1 + 512), 'k_d': (a1 + 512, a1 + 1024), 'v_d': (a1 + 1024, a1 + 1536),
        'gates': (a1 + 1536, a1 + 1536 + 3072),
    }
    order = ['gates', 'u_f', 'v_g', 'r_g', 'v_d', 'q_g', 'k_g', 'q_d', 'k_d']
    w_main = jnp.concatenate([w_in[:, :, pieces[n][0]:pieces[n][1]] for n in order], axis=-1).astype(BF16)
    w_a = jnp.pad(w_in[:, :, a0:a1], ((0, 0), (0, 0), (0, A_PAD - 2 * GATE_RANK))).astype(BF16)
    w2 = jnp.zeros((DEPTH, A_PAD, 2 * GLA_K), F32)
    w2 = w2.at[:, 0:GATE_RANK, 0:GLA_K].set(w_gla_a2[:, 0])
    w2 = w2.at[:, GATE_RANK:2 * GATE_RANK, GLA_K:].set(w_gla_a2[:, 1])
    b2 = b_gla_a.reshape(DEPTH, 1, 2 * GLA_K)

    perm = _rotate_half_perm()
    reps = DIFF_QK // DIFF_HEAD_DIM
    layers = []
    for l in range(DEPTH):
        gq = diff_qk_norm[l, 0]
        gk = diff_qk_norm[l, 1]
        layers.append({
            'norm1': norm1[l][None, :], 'norm2': norm2[l][None, :],
            'w_main': w_main[l], 'w_a': w_a[l], 'w2': w2[l], 'b2': b2[l],
            'qk_gains': tuple(jnp.tile(v, reps)[None, :] for v in (gq, gk, gq[perm], gk[perm])),
            'diff_lambda': diff_lambda[l], 'diff_norm': diff_norm[l][None, :],
            'gla_norm': jnp.tile(gla_norm[l], GLA_HEADS)[None, :],
            'w_fou': w_fou[l].astype(BF16), 'w_gla_o': w_gla_o[l].astype(BF16),
            'w_diff_o': w_diff_o[l].astype(BF16), 'w_out': w_out[l].astype(BF16),
            'w_ff_gate': w_ff_gate[l].astype(BF16), 'w_ff_up': w_ff_up[l].astype(BF16),
            'w_ff_down': w_ff_down[l].astype(BF16),
        })

    mst, hmy, xq, pm, vbd = _gla_tables()
    consts = {
        'grp64': jnp.asarray(_group_mean_matrix(DIFF_QK, DIFF_HEAD_DIM), BF16),
        'grp128': jnp.asarray(_group_mean_matrix(GLA_V, GLA_DV), BF16),
        'rot': jnp.asarray(_rotate_half_matrix(), BF16),
        'rope': _rope_tables(t_lat),
        'ones': jnp.ones((t_ctx, DIFF_QK), F32),
        'bcs': jnp.asarray(_channel_dft_table(), BF16),
        'ct': {tt: (None if tt == FFT_N * FFT_N else _dft_tables(tt)) for tt in {t_ctx, t_lat}},
        'fft': _fft_tables(),
        'gla': (jnp.asarray(mst, BF16), jnp.asarray(hmy, BF16), jnp.asarray(xq, BF16), jnp.asarray(pm),
                jnp.asarray(vbd, BF16)),
    }

    y_prompt, k_list, v_list, s_list = _run_path(x_prompt, mods_ctx, layers, consts, nb_ctx, t_ctx, None)
    y_sample, _, _, _ = _run_path(x_sample, mods_lat, layers, consts, nb_lat, t_lat,
                                  (state_gla, cache_diff_k, cache_diff_v))
    return (y_prompt, y_sample, jnp.stack(k_list, axis=1), jnp.stack(v_list, axis=1),
            jnp.stack(s_list, axis=1))
```

```python
import functools
import math

import numpy as np
import jax
import jax.numpy as jnp
from jax import lax
from jax.experimental import pallas as pl
from jax.experimental.pallas import tpu as pltpu

F32 = jnp.float32
BF16 = jnp.bfloat16

D_MODEL = 1024
DEPTH = 4
GRID_W = 64
FOU_GROUPS = 4
FOU_GROUP_W = 128
FOU_W = 512
GLA_HEADS = 4
GLA_DK = 64
GLA_DV = 128
GLA_K = 256
GLA_V = 512
GATE_RANK = 16
GATE_TEMP = 16.0
DIFF_HEADS = 4
DIFF_HEAD_DIM = 64
DIFF_V_HEAD = 128
DIFF_QK = 512
DIFF_V = 512
ROPE_AXIS_DIM = 32
ROPE_FREQS = 16
ROPE_BASE = 10000.0
N_BRANCH = 3
D_FF = 2816
EPS = 1e-6

COL_GATES = 0
COL_UF = 3072
COL_VG = 3584
COL_RG = 4096
COL_VD = 4608
COL_QG = 5120
COL_KG = 5376
PROJ_W = 5632
W_COLS = PROJ_W + 2 * DIFF_QK
A_PAD = 128

SUBLANES = 8
LANES = 128
VMEM_LIMIT = 48 * 1024 * 1024
VMEM_LIMIT_BIG = 56 * 1024 * 1024

GLA_C = 64
GLA_LEVELS = (32, 16, 8, 4, 2, 1)
GLA_MROWS = GLA_C * (1 + len(GLA_LEVELS)) + SUBLANES
GLA_BB = 8
ROW_TILE = 512
ATTN_TQ = 256
PROJ_TILE_N = 512
FFN_TILE_F = 256
MOD_TILE_N = 768
DFT_ROWS = 256
LOG2_E = math.log2(math.e)


def _choose_tiles(nb, t):
    tm = ROW_TILE if t % ROW_TILE == 0 else t
    tq = ATTN_TQ if t > ATTN_TQ else t // 2
    return tm, tq, math.gcd(nb, GLA_BB), min(t, DFT_ROWS)


def _params(sem, vmem=VMEM_LIMIT):
    return pltpu.CompilerParams(dimension_semantics=sem, vmem_limit_bytes=vmem)


def _dot(a, b):
    return jnp.dot(a, b, preferred_element_type=F32)


def _dot_nt(a, b):
    return lax.dot_general(a, b, (((1,), (1,)), ((), ())), preferred_element_type=F32)


def _dot_tn(a, b):
    return lax.dot_general(a, b, (((0,), (0,)), ((), ())), preferred_element_type=F32)


def _split2(x):
    x1 = x.astype(BF16)
    return x1, (x - x1.astype(F32)).astype(BF16)


def _split3(x):
    x1 = x.astype(BF16)
    r1 = x - x1.astype(F32)
    x2 = r1.astype(BF16)
    x3 = (r1 - x2.astype(F32)).astype(BF16)
    return x1, x2, x3


def _mod_kernel(c_ref, w_ref, b_ref, o_ref):
    c = c_ref[...]
    s = c * jax.nn.sigmoid(c)
    s1, s2, s3 = _split3(s)
    w = w_ref[0]
    w1, w2, w3 = _split3(w)
    acc = _dot(s1, w1) + (_dot(s1, w2) + _dot(s2, w1)) + (_dot(s2, w2) + _dot(s1, w3) + _dot(s3, w1))
    o_ref[0] = acc + b_ref[0]


def _modulation(cond, w_mod, b_mod):
    r = cond.shape[0]
    tn = MOD_TILE_N
    n = 6 * D_MODEL
    return pl.pallas_call(
        _mod_kernel,
        grid=(DEPTH, n // tn),
        in_specs=[
            pl.BlockSpec((r, D_MODEL), lambda l, j: (0, 0)),
            pl.BlockSpec((1, D_MODEL, tn), lambda l, j: (l, 0, j)),
            pl.BlockSpec((1, 1, tn), lambda l, j: (l, 0, j)),
        ],
        out_specs=pl.BlockSpec((1, r, tn), lambda l, j: (l, 0, j)),
        out_shape=jax.ShapeDtypeStruct((DEPTH, r, n), F32),
        compiler_params=_params(("arbitrary", "arbitrary")),
        name="modulation",
    )(cond, w_mod, b_mod.reshape(DEPTH, 1, n))


def _inproj_kernel(tn, use_rope, want_f32, *refs):
    (x_ref, sc_ref, sh_ref, gain_ref, w_ref, wa_ref, w2_ref, b2_ref,
     gq_ref, gk_ref, gqp_ref, gkp_ref, grp_ref, rot_ref, cos_ref, sin_ref) = refs[:16]
    proj_ref, g_ref, qn_ref, kn_ref = refs[16:20]
    kf_ref = refs[20] if want_f32 else None
    h_scr = refs[-1]

    x = x_ref[...]
    ms = jnp.mean(x * x, axis=-1, keepdims=True)
    scale = gain_ref[...] * (1.0 + sc_ref[0])
    hb = (x * lax.rsqrt(ms + EPS) * scale + sh_ref[0]).astype(BF16)
    h_scr[...] = hb

    def tile(j):
        proj_ref[:, j * tn:(j + 1) * tn] = _dot(h_scr[...], w_ref[:, j * tn:(j + 1) * tn]).astype(BF16)

    a = _dot(hb, wa_ref[...])
    y_q = _dot(h_scr[...], w_ref[:, PROJ_W:PROJ_W + DIFF_QK])
    y_k = _dot(h_scr[...], w_ref[:, PROJ_W + DIFF_QK:PROJ_W + 2 * DIFF_QK])
    tile(0)
    a1, a2 = _split2(a)
    v1, v2 = _split2(w2_ref[...])
    yb_q, yb_k = y_q.astype(BF16), y_k.astype(BF16)
    sq_q, sq_k = (y_q * y_q).astype(BF16), (y_k * y_k).astype(BF16)
    z = _dot(a1, v1) + (_dot(a1, v2) + _dot(a2, v1)) + b2_ref[...]
    ms_q = _dot(sq_q, grp_ref[...])
    ms_k = _dot(sq_k, grp_ref[...])
    if use_rope:
        rot_q = _dot(yb_q, rot_ref[...])
        rot_k = _dot(yb_k, rot_ref[...])
    tile(1)
    tile(2)
    logsig = jnp.minimum(z, 0.0) - jnp.log1p(jnp.exp(-jnp.abs(z)))
    g_ref[...] = logsig * (1.0 / GATE_TEMP)

    def normed(y, msq, rot, g_ref_, gp_ref_, out_scale):
        r = lax.rsqrt(msq + EPS)
        out = y * r * g_ref_[...]
        if use_rope:
            out = out * cos_ref[...] + rot * r * gp_ref_[...] * sin_ref[...]
        return out * out_scale

    tile(3)
    qn_ref[...] = normed(y_q, ms_q, rot_q if use_rope else None, gq_ref, gqp_ref,
                         DIFF_HEAD_DIM ** -0.5 * LOG2_E).astype(BF16)
    tile(4)
    kn = normed(y_k, ms_k, rot_k if use_rope else None, gk_ref, gkp_ref, 1.0)
    kn_ref[0] = kn.astype(BF16)
    if want_f32:
        kf_ref[...] = kn
    for j in range(5, PROJ_W // tn):
        tile(j)


def _in_projection(x, sc, sh, gain, w_main, w_a, w2, b2, qk_gains, grp, rot, cos, sin,
                   nb, t, use_rope, want_f32, tm):
    m = x.shape[0]
    tn = PROJ_TILE_N
    per = t // tm
    ncond = sc.shape[0]
    cidx = (lambda i: (i // per, 0, 0)) if ncond > 1 else (lambda i: (0, 0, 0))
    const = lambda i: (0, 0)
    once = pl.Buffered(1)
    vec = pl.BlockSpec((1, DIFF_QK), const)
    mat = pl.BlockSpec((DIFF_QK, DIFF_QK), const, pipeline_mode=once)
    tab = pl.BlockSpec((tm, DIFF_QK), lambda i: (i % per, 0))
    out_specs = [
        pl.BlockSpec((tm, PROJ_W), lambda i: (i, 0)),
        pl.BlockSpec((tm, 2 * GLA_K), lambda i: (i, 0)),
        pl.BlockSpec((tm, DIFF_QK), lambda i: (i, 0)),
        pl.BlockSpec((1, tm, DIFF_QK), lambda i: (i // per, i % per, 0)),
    ]
    out_shape = [
        jax.ShapeDtypeStruct((m, PROJ_W), BF16),
        jax.ShapeDtypeStruct((m, 2 * GLA_K), F32),
        jax.ShapeDtypeStruct((m, DIFF_QK), BF16),
        jax.ShapeDtypeStruct((nb, t, DIFF_QK), BF16),
    ]
    if want_f32:
        out_specs.append(pl.BlockSpec((tm, DIFF_QK), lambda i: (i, 0)))
        out_shape.append(jax.ShapeDtypeStruct((m, DIFF_QK), F32))
    gq, gk, gqp, gkp = qk_gains
    return pl.pallas_call(
        functools.partial(_inproj_kernel, tn, use_rope, want_f32),
        grid=(m // tm,),
        in_specs=[
            pl.BlockSpec((tm, D_MODEL), lambda i: (i, 0)),
            pl.BlockSpec((1, 1, D_MODEL), cidx),
            pl.BlockSpec((1, 1, D_MODEL), cidx),
            pl.BlockSpec((1, D_MODEL), const),
            pl.BlockSpec((D_MODEL, W_COLS), const, pipeline_mode=once),
            pl.BlockSpec((D_MODEL, A_PAD), const, pipeline_mode=once),
            pl.BlockSpec((A_PAD, 2 * GLA_K), const, pipeline_mode=once),
            pl.BlockSpec((1, 2 * GLA_K), const),
            vec, vec, vec, vec, mat, mat, tab, tab,
        ],
        out_specs=out_specs,
        out_shape=out_shape,
        scratch_shapes=[pltpu.VMEM((tm, D_MODEL), BF16)],
        compiler_params=_params(("arbitrary",), VMEM_LIMIT_BIG),
        name="in_projection",
    )(x, sc, sh, gain, w_main, w_a, w2, b2, gq, gk, gqp, gkp, grp, rot, cos, sin)


def _diffattn_kernel(seg_len, lam_init, tq, *refs):
    n_seg = len(seg_len)
    q_ref = refs[0]
    kv_refs = refs[1:1 + 2 * n_seg]
    lam_ref, gain_ref, o_ref = refs[1 + 2 * n_seg:4 + 2 * n_seg]
    kall, vaug, s_a, s_b, m_a, m_b = refs[4 + 2 * n_seg:]
    nq = q_ref.shape[1] // tq

    off = 0
    for s, tk in enumerate(seg_len):
        kall[off:off + tk, :] = kv_refs[2 * s][0]
        v = kv_refs[2 * s + 1][0]
        vaug[off:off + tk, 0:DIFF_V_HEAD] = v
        vaug[off:off + tk, DIFF_V_HEAD:] = jnp.ones_like(v)
        off += tk

    lp = lam_ref[...]
    lam = (jnp.exp(jnp.sum(lp[0:1] * lp[1:2], axis=-1, keepdims=True))
           - jnp.exp(jnp.sum(lp[2:3] * lp[3:4], axis=-1, keepdims=True)) + lam_init)

    def rows(tile):
        return pl.ds(pl.multiple_of(tile * tq, tq), tq)

    def score(tile, s_write, m_write):
        q = q_ref[0, rows(tile), :]
        lane = lax.broadcasted_iota(jnp.int32, q.shape, 1)
        zero = jnp.zeros_like(q)
        q2 = jnp.concatenate([jnp.where(lane < DIFF_HEAD_DIM, q, zero),
                              jnp.where(lane >= DIFF_HEAD_DIM, q, zero)], axis=0)
        sc = _dot_nt(q2, kall[...])
        s_write[...] = sc.reshape(s_write.shape)
        m_write[...] = jnp.broadcast_to(sc.max(axis=-1, keepdims=True), (2 * tq, m_write.shape[2])).reshape(m_write.shape)

    def finish(tile, s_read, m_read):
        outs = []
        for w in range(2):
            p = jnp.exp2(s_read[w] - m_read[w][:, 0:1]).astype(BF16)
            acc = _dot(p, vaug[...])
            outs.append(acc[:, 0:DIFF_V_HEAD] / acc[:, DIFF_V_HEAD:])
        o = outs[0] - lam * outs[1]
        ms = jnp.mean(o * o, axis=-1, keepdims=True)
        o = o * lax.rsqrt(ms + EPS) * gain_ref[...] * (1.0 - lam_init)
        o_ref[0, rows(tile), :] = o.astype(BF16)

    score(0, s_a, m_a)
    n_pairs = (nq - 1) // 2

    def pair(j, carry):
        score(2 * j + 1, s_b, m_b)
        finish(2 * j, s_a, m_a)
        score(2 * j + 2, s_a, m_a)
        finish(2 * j + 1, s_b, m_b)
        return carry

    lax.fori_loop(0, n_pairs, pair, 0)
    done = 2 * n_pairs
    if (nq - 1) % 2 == 1:
        score(done + 1, s_b, m_b)
        finish(done, s_a, m_a)
        finish(done + 1, s_b, m_b)
    else:
        finish(done, s_a, m_a)


def _diff_attention(qn, segments, lam_p, gain, lam_init, nb, tq_total, tq):
    hd = 2 * DIFF_HEAD_DIM
    in_specs = [pl.BlockSpec((1, tq_total, hd), lambda b, h: (b, 0, h))]
    args = [qn]
    seg_len = tuple(seg[4] for seg in segments)
    tk_all = sum(seg_len)
    for (ka, kc, va, vc, tk) in segments:
        in_specs.append(pl.BlockSpec((1, tk, hd), lambda b, h, kc=kc: (b, 0, kc + h)))
        in_specs.append(pl.BlockSpec((1, tk, DIFF_V_HEAD), lambda b, h, vc=vc: (b, 0, vc + h)))
        args += [ka, va]
    in_specs.append(pl.BlockSpec((4, DIFF_HEAD_DIM), lambda b, h: (0, 0)))
    in_specs.append(pl.BlockSpec((1, DIFF_V_HEAD), lambda b, h: (0, 0)))
    args += [lam_p, gain]
    scratch = [
        pltpu.VMEM((tk_all, hd), BF16),
        pltpu.VMEM((tk_all, 2 * DIFF_V_HEAD), BF16),
        pltpu.VMEM((2, tq, tk_all), F32), pltpu.VMEM((2, tq, tk_all), F32),
        pltpu.VMEM((2, tq, hd), F32), pltpu.VMEM((2, tq, hd), F32),
    ]
    return pl.pallas_call(
        functools.partial(_diffattn_kernel, seg_len, lam_init, tq),
        grid=(nb, DIFF_HEADS),
        in_specs=in_specs,
        out_specs=pl.BlockSpec((1, tq_total, DIFF_V_HEAD), lambda b, h: (b, 0, h)),
        out_shape=jax.ShapeDtypeStruct((nb, tq_total, DIFF_V), BF16),
        scratch_shapes=scratch,
        compiler_params=_params(("arbitrary", "arbitrary")),
        name="diff_attention",
    )(*args)


def _gla_tables():
    c = GLA_C
    nl = len(GLA_LEVELS)
    mst = np.zeros((2, GLA_MROWS, c), np.float32)
    xm = np.zeros((2, nl + 1, c), np.float32)
    pm = np.zeros((2, nl + 1, c, c), np.float32)
    idx = np.arange(c)
    for d in range(2):
        ip = idx if d == 0 else c - 1 - idx
        mst[d, 0:c] = (ip[None, :] <= ip[:, None])
        mst[d, c * (nl + 1)] = 1.0
        pm[d, 0] = np.eye(c)
        xm[d, 0] = 1.0
        for li, s in enumerate(GLA_LEVELS):
            ref = (ip // (2 * s)) * (2 * s) + s - 1
            mst[d, c * (li + 1):c * (li + 2)] = (ip[None, :] <= ref[:, None])
            odd = (ip // s) % 2 == 1
            xm[d, li + 1] = odd
            same = (ip[:, None] // (2 * s)) == (ip[None, :] // (2 * s))
            pm[d, li + 1] = odd[:, None] & (~odd)[None, :] & same
    pm = np.tile(pm, (1, 1, 1, GLA_HEADS))
    rows = np.arange(GLA_HEADS * c)[:, None] // c
    hm = (rows == (np.arange(GLA_K)[None, :] // GLA_DK)).astype(np.float32)
    ykeep = np.concatenate([np.ones((2, 1, c), np.float32), 1.0 - xm[:, 1:]], axis=1)
    hmy = hm[None, None] * np.tile(ykeep, (1, 1, GLA_HEADS))[:, :, :, None]
    xq = np.broadcast_to(xm[:, 1:, :, None], (2, nl, c, GLA_K)).copy()
    vbd = (rows == (np.arange(GLA_V)[None, :] // GLA_DV)).astype(np.float32)
    return mst, hmy, xq, pm, vbd


def _gla_kernel(bb, q_ref, k_ref, v_ref, g_ref, mst_ref, hmy_ref, xq_ref, pm_ref, vbd_ref, s0_ref,
                o_ref, sfin_ref, s_scr):
    c_idx = pl.program_id(2)

    @pl.when(c_idx == 0)
    def _():
        s_scr[...] = s0_ref[:, 0]

    c = GLA_C
    nl = len(GLA_LEVELS)
    m = mst_ref[0]
    seqs = range(bb)

    def stack_keys(yb, li):
        return jnp.concatenate([yb] * GLA_HEADS, axis=0) * hmy_ref[0, li]

    r_all = []
    for b in seqs:
        g1, g2 = _split2(g_ref[b])
        r_all.append(_dot(m, g1) + _dot(m, g2))
    cum = [r[0:c] for r in r_all]
    tot = [r[c * (nl + 1):c * (nl + 1) + 1] for r in r_all]
    q = [q_ref[b].astype(F32) * (GLA_DK ** -0.5) for b in seqs]
    k = [k_ref[b].astype(F32) for b in seqs]

    o_inter = []
    for b in seqs:
        state = s_scr[b]
        q_dec = stack_keys((q[b] * jnp.exp(cum[b])).astype(BF16), 0)
        o_rows = _dot_nt(q_dec, state.astype(BF16))
        o_inter.append(jnp.concatenate([o_rows[h * c:(h + 1) * c] for h in range(GLA_HEADS)], axis=1))
        k_dec = stack_keys((k[b] * jnp.exp(tot[b] - cum[b])).astype(BF16), 0)
        vf = v_ref[b].astype(F32)
        v_t = jnp.concatenate(
            [jnp.concatenate([vf[:, (2 * p) * GLA_DV:(2 * p + 1) * GLA_DV],
                              vf[:, (2 * p + 1) * GLA_DV:(2 * p + 2) * GLA_DV]], axis=0).T
             for p in range(GLA_HEADS // 2)], axis=1).astype(BF16)
        s_scr[b] = state * jnp.exp(tot[b]) + _dot(v_t, k_dec)

    @pl.when(c_idx == pl.num_programs(2) - 1)
    def _():
        sfin_ref[:, 0] = s_scr[...]

    att = [pm_ref[0, 0] * _dot_nt(q[b].astype(BF16), stack_keys(k_ref[b], 0)) for b in seqs]
    for li in range(nl):
        for b in seqs:
            ref = r_all[b][c * (li + 1):c * (li + 2)]
            e = jnp.exp(-jnp.abs(cum[b] - ref))
            xs = (q[b] * e).astype(BF16) * xq_ref[0, li]
            ys = stack_keys((k[b] * e).astype(BF16), li + 1)
            att[b] = att[b] + pm_ref[0, li + 1] * _dot_nt(xs, ys)
    for b in seqs:
        v_bd = jnp.concatenate([v_ref[b]] * GLA_HEADS, axis=0) * vbd_ref[...]
        o_ref[0, b] = o_inter[b] + _dot(att[b].astype(BF16), v_bd)


def _gla(proj3, g3, s0, tables, nb, t, bb):
    mst, hmy, xq, pm, vbd = tables
    nc = t // GLA_C
    nl = len(GLA_LEVELS)

    def chunk(d, c):
        return jnp.where(d == 0, c, nc - 1 - c)

    return pl.pallas_call(
        functools.partial(_gla_kernel, bb),
        grid=(nb // bb, 2, nc),
        in_specs=[
            pl.BlockSpec((bb, GLA_C, GLA_K), lambda b, d, c: (b, chunk(d, c), COL_QG // GLA_K)),
            pl.BlockSpec((bb, GLA_C, GLA_K), lambda b, d, c: (b, chunk(d, c), COL_KG // GLA_K)),
            pl.BlockSpec((bb, GLA_C, GLA_V), lambda b, d, c: (b, chunk(d, c), COL_VG // GLA_V)),
            pl.BlockSpec((bb, GLA_C, GLA_K), lambda b, d, c: (b, chunk(d, c), d)),
            pl.BlockSpec((1, GLA_MROWS, GLA_C), lambda b, d, c: (d, 0, 0)),
            pl.BlockSpec((1, nl + 1, GLA_HEADS * GLA_C, GLA_K), lambda b, d, c: (d, 0, 0, 0)),
            pl.BlockSpec((1, nl, GLA_C, GLA_K), lambda b, d, c: (d, 0, 0, 0)),
            pl.BlockSpec((1, nl + 1, GLA_C, GLA_HEADS * GLA_C), lambda b, d, c: (d, 0, 0, 0)),
            pl.BlockSpec((GLA_HEADS * GLA_C, GLA_V), lambda b, d, c: (0, 0)),
            pl.BlockSpec((bb, 1, GLA_DV, GLA_K), lambda b, d, c: (b, d, 0, 0)),
        ],
        out_specs=[
            pl.BlockSpec((1, bb, GLA_C, GLA_V), lambda b, d, c: (d, b, chunk(d, c), 0)),
            pl.BlockSpec((bb, 1, GLA_DV, GLA_K), lambda b, d, c: (b, d, 0, 0)),
        ],
        out_shape=[
            jax.ShapeDtypeStruct((2, nb, t, GLA_V), F32),
            jax.ShapeDtypeStruct((nb, 2, GLA_DV, GLA_K), F32),
        ],
        scratch_shapes=[pltpu.VMEM((bb, GLA_DV, GLA_K), F32)],
        compiler_params=_params(("arbitrary", "arbitrary", "arbitrary")),
        name="gla",
    )(proj3, proj3, proj3, g3, mst, hmy, xq, pm, vbd, s0)


def _fourier_kernel(t, u_ref, bcs_ref, ct_ref, o_ref, y_scr):
    i = pl.program_id(1)

    @pl.when(i == 0)
    def _():
        rows = min(t, ROW_TILE)
        for r in range(t // rows):
            y = _dot(u_ref[0, r * rows:(r + 1) * rows, :], bcs_ref[...])
            y_scr[r * rows:(r + 1) * rows, :] = y[:, :FOU_W].astype(BF16)
            y_scr[t + r * rows:t + (r + 1) * rows, :] = y[:, FOU_W:].astype(BF16)

    o_ref[0] = _dot(ct_ref[...], y_scr[...]).astype(BF16)


def _fourier(proj3, bcs, ct, nb, t, tm):
    return pl.pallas_call(
        functools.partial(_fourier_kernel, t),
        grid=(nb, t // tm),
        in_specs=[
            pl.BlockSpec((1, t, FOU_W), lambda b, i: (b, 0, COL_UF // FOU_W)),
            pl.BlockSpec((FOU_W, 2 * FOU_W), lambda b, i: (0, 0)),
            pl.BlockSpec((tm, 2 * t), lambda b, i: (i, 0)),
        ],
        out_specs=pl.BlockSpec((1, tm, FOU_W), lambda b, i: (b, i, 0)),
        out_shape=jax.ShapeDtypeStruct((nb, t, FOU_W), BF16),
        scratch_shapes=[pltpu.VMEM((2 * t, FOU_W), BF16)],
        compiler_params=_params(("arbitrary", "arbitrary")),
        name="fourier_mix",
    )(proj3, bcs, ct)


FFT_N = 64
FFT_G = 8
FFT_CW = 256


def _fft_kernel(u_ref, bcs_ref, m1_ref, q2_ref, o_ref, z_scr, a_scr):
    n, gsz, cw = FFT_N, FFT_G, FFT_CW
    rows = n * gsz
    for r in range(n // gsz):
        y = _dot(u_ref[0, r * rows:(r + 1) * rows, :], bcs_ref[...])
        z_scr[gsz * r:gsz * (r + 1)] = y.reshape(gsz, n, 2 * cw)
    for g in range(n // gsz):
        zg = z_scr[:, gsz * g:gsz * (g + 1), :].reshape(rows, 2 * cw)
        rhs = jnp.concatenate([zg[:, :cw], zg[:, cw:]], axis=0).astype(BF16)
        a_scr[g] = _dot(m1_ref[...], rhs).astype(BF16)
    for f in range(n // gsz):
        slabs = [a_scr[g, part * rows + n * f:part * rows + n * (f + 1), :]
                 for part in range(2) for g in range(n // gsz)]
        res = _dot(q2_ref[f], jnp.concatenate(slabs, axis=0))
        o_ref[0, :, gsz * f:gsz * (f + 1), :] = res.reshape(n, gsz, cw)


def _fft_tables():
    n, gsz, t = FFT_N, FFT_G, FFT_N * FFT_N
    k = np.arange(FOU_GROUP_W)
    ang = 2.0 * np.pi * ((k[:, None] * k[None, :]) % FOU_GROUP_W) / FOU_GROUP_W
    eye2 = np.eye(FFT_CW // FOU_GROUP_W)
    bcs = np.concatenate([np.kron(eye2, np.cos(ang)), -np.kron(eye2, np.sin(ang))], axis=1) * FOU_GROUP_W ** -0.5
    i = np.arange(n)
    a1 = 2.0 * np.pi * ((i[:, None] * i[None, :]) % n) / n
    c1 = np.kron(np.cos(a1), np.eye(gsz))
    s1 = np.kron(np.sin(a1), np.eye(gsz))
    m1 = np.block([[c1, s1], [-s1, c1]])
    ngrp = n // gsz
    shape = (ngrp, n * gsz, n * gsz)
    f = lax.broadcasted_iota(jnp.int32, shape, 0)
    row = lax.broadcasted_iota(jnp.int32, shape, 1)
    col = lax.broadcasted_iota(jnp.int32, shape, 2)
    f2, a = row // gsz, row % gsz
    g, ap, j = col // (gsz * gsz), (col // gsz) % gsz, col % gsz
    freq = gsz * f + a + n * f2
    theta = (((gsz * g + j) * freq) % t).astype(F32) * (2.0 * math.pi / t)
    keep = (a == ap)
    scale = t ** -0.5
    qc = jnp.where(keep, jnp.cos(theta) * scale, 0.0)
    qs = jnp.where(keep, jnp.sin(theta) * scale, 0.0)
    q2 = jnp.concatenate([qc, qs], axis=2).astype(BF16)
    return jnp.asarray(bcs, BF16), jnp.asarray(m1, BF16), q2


def _fourier_fft(proj3, tables, nb):
    bcs, m1, q2 = tables
    n, gsz, cw = FFT_N, FFT_G, FFT_CW
    t = n * n
    once = pl.Buffered(1)
    out = pl.pallas_call(
        _fft_kernel,
        grid=(nb, FOU_W // cw),
        in_specs=[
            pl.BlockSpec((1, t, cw), lambda b, h: (b, 0, COL_UF // cw + h)),
            pl.BlockSpec((cw, 2 * cw), lambda b, h: (0, 0), pipeline_mode=once),
            pl.BlockSpec((2 * n * gsz, 2 * n * gsz), lambda b, h: (0, 0), pipeline_mode=once),
            pl.BlockSpec((n // gsz, n * gsz, 2 * n * gsz), lambda b, h: (0, 0, 0), pipeline_mode=once),
        ],
        out_specs=pl.BlockSpec((1, n, n, cw), lambda b, h: (b, 0, 0, h)),
        out_shape=jax.ShapeDtypeStruct((nb, n, n, FOU_W), F32),
        scratch_shapes=[pltpu.VMEM((n, n, 2 * cw), F32), pltpu.VMEM((n // gsz, 2 * n * gsz, cw), BF16)],
        compiler_params=_params(("arbitrary", "arbitrary")),
        name="fourier_fft",
    )(proj3, bcs, m1, q2)
    return out.reshape(nb, t, FOU_W)


def _dft_tables(t):
    lo = min(t, 64)
    hi = t // lo
    f = jnp.arange(t, dtype=jnp.int32)[None, :]
    unit = 2.0 * math.pi / t
    a = ((lo * jnp.arange(hi, dtype=jnp.int32)[:, None] * f) % t).astype(F32) * unit
    b = ((jnp.arange(lo, dtype=jnp.int32)[:, None] * f) % t).astype(F32) * unit
    ca, sa = jnp.cos(a)[:, None, :], jnp.sin(a)[:, None, :]
    cb, sb = jnp.cos(b)[None, :, :], jnp.sin(b)[None, :, :]
    scale = t ** -0.5
    cos = ((ca * cb - sa * sb) * scale).reshape(t, t)
    msin = ((sa * cb + ca * sb) * -scale).reshape(t, t)
    return jnp.concatenate([cos, msin], axis=1).astype(BF16)


def _channel_dft_table():
    n = np.arange(FOU_GROUP_W)
    ang = 2.0 * np.pi * ((n[:, None] * n[None, :]) % FOU_GROUP_W) / FOU_GROUP_W
    eye = np.eye(FOU_GROUPS)
    bc = np.kron(eye, np.cos(ang)) * FOU_GROUP_W ** -0.5
    bs = np.kron(eye, np.sin(ang)) * FOU_GROUP_W ** -0.5
    return np.concatenate([bc, bs], axis=1).astype(np.float32)


def _merge_kernel(x_ref, g1_ref, fm_ref, of_ref, ob_ref, r_ref, od_ref, gt_ref,
                  wf_ref, wg_ref, wd_ref, wo_ref, gn_ref, grp_ref, o_ref):
    y_f = _dot(fm_ref[...].astype(BF16), wf_ref[...])
    og = of_ref[0] + ob_ref[0]
    ms = _dot((og * og).astype(BF16), grp_ref[...])
    r = r_ref[...].astype(F32)
    og = og * lax.rsqrt(ms + EPS) * gn_ref[...] * (r * jax.nn.sigmoid(r))
    y_g = _dot(og.astype(BF16), wg_ref[...])
    y_d = _dot(od_ref[...], wd_ref[...])
    gates = jax.nn.sigmoid(gt_ref[...].astype(F32))
    merged = (gates[:, 0:D_MODEL] * y_f + gates[:, D_MODEL:2 * D_MODEL] * y_g
              + gates[:, 2 * D_MODEL:3 * D_MODEL] * y_d)
    o_ref[...] = x_ref[...] + g1_ref[0] * _dot(merged.astype(BF16), wo_ref[...])


def _merge(x, g1, proj, fm, o_gla, od, w_fou, w_gla_o, w_diff_o, w_out, gn, grp, tokens_per_cond, tm):
    m = x.shape[0]
    per = tokens_per_cond // tm
    ncond = g1.shape[0]
    cidx = (lambda i: (i // per, 0, 0)) if ncond > 1 else (lambda i: (0, 0, 0))
    once = pl.Buffered(1)
    half = pl.BlockSpec((FOU_W, D_MODEL), lambda i: (0, 0), pipeline_mode=once)
    return pl.pallas_call(
        _merge_kernel,
        grid=(m // tm,),
        in_specs=[
            pl.BlockSpec((tm, D_MODEL), lambda i: (i, 0)),
            pl.BlockSpec((1, 1, D_MODEL), cidx),
            pl.BlockSpec((tm, FOU_W), lambda i: (i, 0)),
            pl.BlockSpec((1, tm, GLA_V), lambda i: (0, i, 0)),
            pl.BlockSpec((1, tm, GLA_V), lambda i: (1, i, 0)),
            pl.BlockSpec((tm, GLA_V), lambda i: (i, COL_RG // GLA_V)),
            pl.BlockSpec((tm, DIFF_V), lambda i: (i, 0)),
            pl.BlockSpec((tm, N_BRANCH * D_MODEL), lambda i: (i, COL_GATES)),
            half, half, half,
            pl.BlockSpec((D_MODEL, D_MODEL), lambda i: (0, 0), pipeline_mode=once),
            pl.BlockSpec((1, GLA_V), lambda i: (0, 0)),
            pl.BlockSpec((GLA_V, GLA_V), lambda i: (0, 0)),
        ],
        out_specs=pl.BlockSpec((tm, D_MODEL), lambda i: (i, 0)),
        out_shape=jax.ShapeDtypeStruct((m, D_MODEL), F32),
        compiler_params=_params(("arbitrary",)),
        name="merge",
    )(x, g1, fm, o_gla, o_gla, proj, od, proj, w_fou, w_gla_o, w_diff_o, w_out, gn, grp)


def _ffn_kernel(tf, x_ref, sc_ref, sh_ref, g2_ref, gain_ref, wg_ref, wu_ref, wd_ref, o_ref, h_scr):
    x = x_ref[...]
    ms = jnp.mean(x * x, axis=-1, keepdims=True)
    scale = gain_ref[...] * (1.0 + sc_ref[0])
    h_scr[...] = (x * lax.rsqrt(ms + EPS) * scale + sh_ref[0]).astype(BF16)
    acc = None
    for f in range(D_FF // tf):
        cols = slice(f * tf, (f + 1) * tf)
        gate = _dot(h_scr[...], wg_ref[:, cols])
        up = _dot(h_scr[...], wu_ref[:, cols])
        a = (gate * jax.nn.sigmoid(gate) * up).astype(BF16)
        part = _dot(a, wd_ref[cols, :])
        acc = part if acc is None else acc + part
    o_ref[...] = x_ref[...] + g2_ref[0] * acc


def _ffn(x, sc, sh, g2, gain, w_gate, w_up, w_down, tokens_per_cond, tm):
    m = x.shape[0]
    tf = FFN_TILE_F
    per = tokens_per_cond // tm
    ncond = sc.shape[0]
    cidx = (lambda i: (i // per, 0, 0)) if ncond > 1 else (lambda i: (0, 0, 0))
    const = lambda i: (0, 0)
    once = pl.Buffered(1)
    return pl.pallas_call(
        functools.partial(_ffn_kernel, tf),
        grid=(m // tm,),
        in_specs=[
            pl.BlockSpec((tm, D_MODEL), lambda i: (i, 0)),
            pl.BlockSpec((1, 1, D_MODEL), cidx),
            pl.BlockSpec((1, 1, D_MODEL), cidx),
            pl.BlockSpec((1, 1, D_MODEL), cidx),
            pl.BlockSpec((1, D_MODEL), const),
            pl.BlockSpec((D_MODEL, D_FF), const, pipeline_mode=once),
            pl.BlockSpec((D_MODEL, D_FF), const, pipeline_mode=once),
            pl.BlockSpec((D_FF, D_MODEL), const, pipeline_mode=once),
        ],
        out_specs=pl.BlockSpec((tm, D_MODEL), lambda i: (i, 0)),
        out_shape=jax.ShapeDtypeStruct((m, D_MODEL), F32),
        scratch_shapes=[pltpu.VMEM((tm, D_MODEL), BF16)],
        compiler_params=_params(("arbitrary",)),
        name="ffn",
    )(x, sc, sh, g2, gain, w_gate, w_up, w_down)


def _group_mean_matrix(width, group):
    idx = np.arange(width) // group
    return (idx[:, None] == idx[None, :]).astype(np.float32) / group


def _rotate_half_matrix():
    r = np.zeros((DIFF_HEAD_DIM, DIFF_HEAD_DIM), np.float32)
    for axis in range(2):
        base = axis * ROPE_AXIS_DIM
        for f in range(ROPE_FREQS):
            r[base + ROPE_FREQS + f, base + f] = -1.0
            r[base + f, base + ROPE_FREQS + f] = 1.0
    return np.kron(np.eye(DIFF_QK // DIFF_HEAD_DIM), r).astype(np.float32)


def _rotate_half_perm():
    p = np.zeros((DIFF_HEAD_DIM,), np.int32)
    for axis in range(2):
        base = axis * ROPE_AXIS_DIM
        for f in range(ROPE_FREQS):
            p[base + f] = base + ROPE_FREQS + f
            p[base + ROPE_FREQS + f] = base + f
    return p


def _rope_tables(n_tokens):
    rows = n_tokens // GRID_W
    row = jnp.repeat(jnp.arange(rows), GRID_W).astype(F32)
    col = jnp.tile(jnp.arange(GRID_W), rows).astype(F32)
    inv = ROPE_BASE ** (-jnp.arange(ROPE_FREQS, dtype=F32) * 2.0 / ROPE_AXIS_DIM)
    ang_r = row[:, None] * inv
    ang_c = col[:, None] * inv
    ang = jnp.concatenate([ang_r, ang_r, ang_c, ang_c], axis=-1)
    reps = DIFF_QK // DIFF_HEAD_DIM
    return jnp.tile(jnp.cos(ang), (1, reps)), jnp.tile(jnp.sin(ang), (1, reps))


def _pack_state(s):
    return jnp.transpose(s, (0, 1, 4, 2, 3)).reshape(s.shape[0], 2, GLA_DV, GLA_K)


def _unpack_state(sp):
    s = sp.reshape(sp.shape[0], 2, GLA_DV, GLA_HEADS, GLA_DK)
    return jnp.transpose(s, (0, 1, 3, 4, 2))


def _run_path(x, mods, layers, consts, nb, t, ctx):
    m = nb * t
    tm, tq, gla_bb, dft_rows = _choose_tiles(nb, t)
    x = x.reshape(m, D_MODEL)
    is_latent = ctx is not None
    cos, sin = consts['rope'] if is_latent else (consts['ones'], consts['ones'])
    ct = consts['ct'][t]
    k_out, v_out, s_out = [], [], []
    for l, p in enumerate(layers):
        sh1, sc1, g1, sh2, sc2, g2 = [mods[l][:, None, j * D_MODEL:(j + 1) * D_MODEL] for j in range(6)]
        outs = _in_projection(x, sc1, sh1, p['norm1'], p['w_main'], p['w_a'], p['w2'], p['b2'],
                              p['qk_gains'], consts['grp64'], consts['rot'], cos, sin,
                              nb, t, is_latent, not is_latent, tm)
        proj, g, qn, kn = outs[:4]
        proj3 = proj.reshape(nb, t, PROJ_W)
        lam_init = 0.8 - 0.6 * math.exp(-0.3 * l)
        segments = [(kn, 0, proj3, COL_VD // DIFF_V_HEAD, t)]
        if is_latent:
            ck = ctx[1][:, l].reshape(nb, -1, DIFF_QK).astype(BF16)
            cv = ctx[2][:, l].reshape(nb, -1, DIFF_V).astype(BF16)
            segments.append((ck, 0, cv, 0, ck.shape[1]))
        od = _diff_attention(qn.reshape(nb, t, DIFF_QK), segments, p['diff_lambda'], p['diff_norm'],
                             lam_init, nb, t, tq)

        if is_latent:
            s0 = _pack_state(ctx[0][:, l])
        else:
            s0 = jnp.zeros((nb, 2, GLA_DV, GLA_K), F32)
        o_gla, s_fin = _gla(proj3, g.reshape(nb, t, 2 * GLA_K), s0, consts['gla'], nb, t, gla_bb)

        if t == FFT_N * FFT_N:
            fm = _fourier_fft(proj3, consts['fft'], nb)
        else:
            fm = _fourier(proj3, consts['bcs'], ct, nb, t, dft_rows)

        x = _merge(x, g1, proj, fm.reshape(m, FOU_W), o_gla.reshape(2, m, GLA_V), od.reshape(m, DIFF_V),
                   p['w_fou'], p['w_gla_o'], p['w_diff_o'], p['w_out'], p['gla_norm'], consts['grp128'],
                   t, tm)
        x = _ffn(x, sc2, sh2, g2, p['norm2'], p['w_ff_gate'], p['w_ff_up'], p['w_ff_down'], t, tm)

        if not is_latent:
            k_out.append(outs[4].reshape(nb, t, DIFF_HEADS, 2, DIFF_HEAD_DIM))
            v_out.append(proj3[:, :, COL_VD:COL_VD + DIFF_V].astype(F32).reshape(nb, t, DIFF_HEADS, DIFF_V_HEAD))
            s_out.append(_unpack_state(s_fin))
    return x.reshape(nb, t, D_MODEL), k_out, v_out, s_out


def kernel(x_prompt, x_sample, c, cache_diff_k, cache_diff_v, state_gla, c_ctx, w_mod, b_mod, norm1, norm2, w_in, w_gla_a2, b_gla_a, gla_norm, diff_qk_norm, diff_lambda, diff_norm, w_fou, w_gla_o, w_diff_o, w_out, w_ff_gate, w_ff_up, w_ff_down):
    nb_ctx, t_ctx, _ = x_prompt.shape
    nb_lat, t_lat, _ = x_sample.shape

    n_cond = 1 + nb_lat
    r_pad = -(-n_cond // 8) * 8
    cond = jnp.zeros((r_pad, D_MODEL), F32).at[0].set(c_ctx).at[1:n_cond].set(c)
    mods = _modulation(cond, w_mod, b_mod)
    mods_ctx = [mods[l, 0:1] for l in range(DEPTH)]
    mods_lat = [mods[l, 1:n_cond] for l in range(DEPTH)]

    a0 = 2048
    a1 = a0 + 2 * GATE_RANK
    pieces = {
        'u_f': (0, 512), 'q_g': (512, 768), 'k_g': (768, 1024), 'v_g': (1024, 1536), 'r_g': (1536, 2048),
        'q_d': (a1, a1 + 512), 'k_d': (a1 + 512, a1 + 1024), 'v_d': (a1 + 1024, a1 + 1536),
        'gates': (a1 + 1536, a1 + 1536 + 3072),
    }
    order = ['gates', 'u_f', 'v_g', 'r_g', 'v_d', 'q_g', 'k_g', 'q_d', 'k_d']
    w_main = jnp.concatenate([w_in[:, :, pieces[n][0]:pieces[n][1]] for n in order], axis=-1).astype(BF16)
    w_a = jnp.pad(w_in[:, :, a0:a1], ((0, 0), (0, 0), (0, A_PAD - 2 * GATE_RANK))).astype(BF16)
    w2 = jnp.zeros((DEPTH, A_PAD, 2 * GLA_K), F32)
    w2 = w2.at[:, 0:GATE_RANK, 0:GLA_K].set(w_gla_a2[:, 0])
    w2 = w2.at[:, GATE_RANK:2 * GATE_RANK, GLA_K:].set(w_gla_a2[:, 1])
    b2 = b_gla_a.reshape(DEPTH, 1, 2 * GLA_K)

    perm = _rotate_half_perm()
    reps = DIFF_QK // DIFF_HEAD_DIM
    layers = []
    for l in range(DEPTH):
        gq = diff_qk_norm[l, 0]
        gk = diff_qk_norm[l, 1]
        layers.append({
            'norm1': norm1[l][None, :], 'norm2': norm2[l][None, :],
            'w_main': w_main[l], 'w_a': w_a[l], 'w2': w2[l], 'b2': b2[l],
            'qk_gains': tuple(jnp.tile(v, reps)[None, :] for v in (gq, gk, gq[perm], gk[perm])),
            'diff_lambda': diff_lambda[l], 'diff_norm': diff_norm[l][None, :],
            'gla_norm': jnp.tile(gla_norm[l], GLA_HEADS)[None, :],
            'w_fou': w_fou[l].astype(BF16), 'w_gla_o': w_gla_o[l].astype(BF16),
            'w_diff_o': w_diff_o[l].astype(BF16), 'w_out': w_out[l].astype(BF16),
            'w_ff_gate': w_ff_gate[l].astype(BF16), 'w_ff_up': w_ff_up[l].astype(BF16),
            'w_ff_down': w_ff_down[l].astype(BF16),
        })

    mst, hmy, xq, pm, vbd = _gla_tables()
    consts = {
        'grp64': jnp.asarray(_group_mean_matrix(DIFF_QK, DIFF_HEAD_DIM), BF16),
        'grp128': jnp.asarray(_group_mean_matrix(GLA_V, GLA_DV), BF16),
        'rot': jnp.asarray(_rotate_half_matrix(), BF16),
        'rope': _rope_tables(t_lat),
        'ones': jnp.ones((t_ctx, DIFF_QK), F32),
        'bcs': jnp.asarray(_channel_dft_table(), BF16),
        'ct': {tt: (None if tt == FFT_N * FFT_N else _dft_tables(tt)) for tt in {t_ctx, t_lat}},
        'fft': _fft_tables(),
        'gla': (jnp.asarray(mst, BF16), jnp.asarray(hmy, BF16), jnp.asarray(xq, BF16), jnp.asarray(pm),
                jnp.asarray(vbd, BF16)),
    }

    y_prompt, k_list, v_list, s_list = _run_path(x_prompt, mods_ctx, layers, consts, nb_ctx, t_ctx, None)
    y_sample, _, _, _ = _run_path(x_sample, mods_lat, layers, consts, nb_lat, t_lat,
                                  (state_gla, cache_diff_k, cache_diff_v))
    return (y_prompt, y_sample, jnp.stack(k_list, axis=1), jnp.stack(v_list, axis=1),
            jnp.stack(s_list, axis=1))
```

```python
import functools
import math

import numpy as np
import jax
import jax.numpy as jnp
from jax import lax
from jax.experimental import pallas as pl
from jax.experimental.pallas import tpu as pltpu

F32 = jnp.float32
BF16 = jnp.bfloat16

D_MODEL = 1024
DEPTH = 4
GRID_W = 64
FOU_GROUPS = 4
FOU_GROUP_W = 128
FOU_W = 512
GLA_HEADS = 4
GLA_DK = 64
GLA_DV = 128
GLA_K = 256
GLA_V = 512
GATE_RANK = 16
GATE_TEMP = 16.0
DIFF_HEADS = 4
DIFF_HEAD_DIM = 64
DIFF_V_HEAD = 128
DIFF_QK = 512
DIFF_V = 512
ROPE_AXIS_DIM = 32
ROPE_FREQS = 16
ROPE_BASE = 10000.0
N_BRANCH = 3
D_FF = 2816
EPS = 1e-6

COL_GATES = 0
COL_UF = 3072
COL_VG = 3584
COL_RG = 4096
COL_VD = 4608
COL_QG = 5120
COL_KG = 5376
PROJ_W = 5632
W_COLS = PROJ_W + 2 * DIFF_QK
A_PAD = 128

SUBLANES = 8
LANES = 128
VMEM_LIMIT = 48 * 1024 * 1024
VMEM_LIMIT_BIG = 56 * 1024 * 1024

GLA_C = 64
GLA_LEVELS = (32, 16, 8, 4, 2, 1)
GLA_COARSE = tuple(s for s in GLA_LEVELS if 2 * s >= SUBLANES)
GLA_SHARED = -(-(1 + sum(GLA_C // (2 * s) for s in GLA_COARSE)) // SUBLANES) * SUBLANES
GLA_MROWS = GLA_C * (1 + len(GLA_LEVELS) - len(GLA_COARSE)) + GLA_SHARED
GLA_BB = 8
ROW_TILE = 512
ATTN_TQ = 256
PROJ_TILE_N = 512
FFN_TILE_F = 256
MOD_TILE_N = 768
DFT_ROWS = 256
LOG2_E = math.log2(math.e)


def _choose_tiles(nb, t):
    tm = ROW_TILE if t % ROW_TILE == 0 else t
    tq = ATTN_TQ if t > ATTN_TQ else t // 2
    return tm, tq, math.gcd(nb, GLA_BB), min(t, DFT_ROWS)


def _params(sem, vmem=VMEM_LIMIT):
    return pltpu.CompilerParams(dimension_semantics=sem, vmem_limit_bytes=vmem)


def _dot(a, b):
    return jnp.dot(a, b, preferred_element_type=F32)


def _dot_nt(a, b):
    return lax.dot_general(a, b, (((1,), (1,)), ((), ())), preferred_element_type=F32)


def _dot_tn(a, b):
    return lax.dot_general(a, b, (((0,), (0,)), ((), ())), preferred_element_type=F32)


def _split2(x):
    x1 = x.astype(BF16)
    return x1, (x - x1.astype(F32)).astype(BF16)


def _split3(x):
    x1 = x.astype(BF16)
    r1 = x - x1.astype(F32)
    x2 = r1.astype(BF16)
    x3 = (r1 - x2.astype(F32)).astype(BF16)
    return x1, x2, x3


def _mod_kernel(c_ref, w_ref, b_ref, o_ref):
    c = c_ref[...]
    s = c * jax.nn.sigmoid(c)
    s1, s2, s3 = _split3(s)
    w = w_ref[0]
    w1, w2, w3 = _split3(w)
    acc = _dot(s1, w1) + (_dot(s1, w2) + _dot(s2, w1)) + (_dot(s2, w2) + _dot(s1, w3) + _dot(s3, w1))
    o_ref[0] = acc + b_ref[0]


def _modulation(cond, w_mod, b_mod):
    r = cond.shape[0]
    tn = MOD_TILE_N
    n = 6 * D_MODEL
    return pl.pallas_call(
        _mod_kernel,
        grid=(DEPTH, n // tn),
        in_specs=[
            pl.BlockSpec((r, D_MODEL), lambda l, j: (0, 0)),
            pl.BlockSpec((1, D_MODEL, tn), lambda l, j: (l, 0, j)),
            pl.BlockSpec((1, 1, tn), lambda l, j: (l, 0, j)),
        ],
        out_specs=pl.BlockSpec((1, r, tn), lambda l, j: (l, 0, j)),
        out_shape=jax.ShapeDtypeStruct((DEPTH, r, n), F32),
        compiler_params=_params(("arbitrary", "arbitrary")),
        name="modulation",
    )(cond, w_mod, b_mod.reshape(DEPTH, 1, n))


def _inproj_kernel(tn, use_rope, want_f32, *refs):
    (x_ref, sc_ref, sh_ref, gain_ref, w_ref, wa_ref, w2_ref, b2_ref,
     gq_ref, gk_ref, gqp_ref, gkp_ref, grp_ref, rot_ref, cos_ref, sin_ref) = refs[:16]
    proj_ref, g_ref, qn_ref, kn_ref = refs[16:20]
    kf_ref = refs[20] if want_f32 else None
    h_scr = refs[-1]

    x = x_ref[...]
    ms = jnp.mean(x * x, axis=-1, keepdims=True)
    scale = gain_ref[...] * (1.0 + sc_ref[0])
    hb = (x * lax.rsqrt(ms + EPS) * scale + sh_ref[0]).astype(BF16)
    h_scr[...] = hb

    def tile(j):
        proj_ref[:, j * tn:(j + 1) * tn] = _dot(h_scr[...], w_ref[:, j * tn:(j + 1) * tn]).astype(BF16)

    a = _dot(hb, wa_ref[...])
    y_q = _dot(h_scr[...], w_ref[:, PROJ_W:PROJ_W + DIFF_QK])
    y_k = _dot(h_scr[...], w_ref[:, PROJ_W + DIFF_QK:PROJ_W + 2 * DIFF_QK])
    tile(0)
    a1, a2 = _split2(a)
    v1, v2 = _split2(w2_ref[...])
    yb_q, yb_k = y_q.astype(BF16), y_k.astype(BF16)
    sq_q, sq_k = (y_q * y_q).astype(BF16), (y_k * y_k).astype(BF16)
    z = _dot(a1, v1) + (_dot(a1, v2) + _dot(a2, v1)) + b2_ref[...]
    ms_q = _dot(sq_q, grp_ref[...])
    ms_k = _dot(sq_k, grp_ref[...])
    if use_rope:
        rot_q = _dot(yb_q, rot_ref[...])
        rot_k = _dot(yb_k, rot_ref[...])
    tile(1)
    tile(2)
    logsig = jnp.minimum(z, 0.0) - jnp.log1p(jnp.exp(-jnp.abs(z)))
    g_ref[...] = logsig * (1.0 / GATE_TEMP)

    def normed(y, msq, rot, g_ref_, gp_ref_, out_scale):
        r = lax.rsqrt(msq + EPS)
        out = y * r * g_ref_[...]
        if use_rope:
            out = out * cos_ref[...] + rot * r * gp_ref_[...] * sin_ref[...]
        return out * out_scale

    tile(3)
    qn_ref[...] = normed(y_q, ms_q, rot_q if use_rope else None, gq_ref, gqp_ref,
                         DIFF_HEAD_DIM ** -0.5 * LOG2_E).astype(BF16)
    tile(4)
    kn = normed(y_k, ms_k, rot_k if use_rope else None, gk_ref, gkp_ref, 1.0)
    kn_ref[0] = kn.astype(BF16)
    if want_f32:
        kf_ref[...] = kn
    for j in range(5, PROJ_W // tn):
        tile(j)


def _in_projection(x, sc, sh, gain, w_main, w_a, w2, b2, qk_gains, grp, rot, cos, sin,
                   nb, t, use_rope, want_f32, tm):
    m = x.shape[0]
    tn = PROJ_TILE_N
    per = t // tm
    ncond = sc.shape[0]
    cidx = (lambda i: (i // per, 0, 0)) if ncond > 1 else (lambda i: (0, 0, 0))
    const = lambda i: (0, 0)
    once = pl.Buffered(1)
    vec = pl.BlockSpec((1, DIFF_QK), const)
    mat = pl.BlockSpec((DIFF_QK, DIFF_QK), const, pipeline_mode=once)
    tab = pl.BlockSpec((tm, DIFF_QK), lambda i: (i % per, 0))
    out_specs = [
        pl.BlockSpec((tm, PROJ_W), lambda i: (i, 0)),
        pl.BlockSpec((tm, 2 * GLA_K), lambda i: (i, 0)),
        pl.BlockSpec((tm, DIFF_QK), lambda i: (i, 0)),
        pl.BlockSpec((1, tm, DIFF_QK), lambda i: (i // per, i % per, 0)),
    ]
    out_shape = [
        jax.ShapeDtypeStruct((m, PROJ_W), BF16),
        jax.ShapeDtypeStruct((m, 2 * GLA_K), F32),
        jax.ShapeDtypeStruct((m, DIFF_QK), BF16),
        jax.ShapeDtypeStruct((nb, t, DIFF_QK), BF16),
    ]
    if want_f32:
        out_specs.append(pl.BlockSpec((tm, DIFF_QK), lambda i: (i, 0)))
        out_shape.append(jax.ShapeDtypeStruct((m, DIFF_QK), F32))
    gq, gk, gqp, gkp = qk_gains
    return pl.pallas_call(
        functools.partial(_inproj_kernel, tn, use_rope, want_f32),
        grid=(m // tm,),
        in_specs=[
            pl.BlockSpec((tm, D_MODEL), lambda i: (i, 0)),
            pl.BlockSpec((1, 1, D_MODEL), cidx),
            pl.BlockSpec((1, 1, D_MODEL), cidx),
            pl.BlockSpec((1, D_MODEL), const),
            pl.BlockSpec((D_MODEL, W_COLS), const, pipeline_mode=once),
            pl.BlockSpec((D_MODEL, A_PAD), const, pipeline_mode=once),
            pl.BlockSpec((A_PAD, 2 * GLA_K), const, pipeline_mode=once),
            pl.BlockSpec((1, 2 * GLA_K), const),
            vec, vec, vec, vec, mat, mat, tab, tab,
        ],
        out_specs=out_specs,
        out_shape=out_shape,
        scratch_shapes=[pltpu.VMEM((tm, D_MODEL), BF16)],
        compiler_params=_params(("arbitrary",), VMEM_LIMIT_BIG),
        name="in_projection",
    )(x, sc, sh, gain, w_main, w_a, w2, b2, gq, gk, gqp, gkp, grp, rot, cos, sin)


def _diffattn_kernel(seg_len, lam_init, tq, *refs):
    n_seg = len(seg_len)
    q_ref = refs[0]
    kv_refs = refs[1:1 + 2 * n_seg]
    lam_ref, gain_ref, o_ref = refs[1 + 2 * n_seg:4 + 2 * n_seg]
    kall, vaug, s_a, s_b, m_a, m_b = refs[4 + 2 * n_seg:]
    nq = q_ref.shape[1] // tq

    off = 0
    for s, tk in enumerate(seg_len):
        kall[off:off + tk, :] = kv_refs[2 * s][0]
        v = kv_refs[2 * s + 1][0]
        vaug[off:off + tk, 0:DIFF_V_HEAD] = v
        vaug[off:off + tk, DIFF_V_HEAD:] = jnp.ones_like(v)
        off += tk

    lp = lam_ref[...]
    lam = (jnp.exp(jnp.sum(lp[0:1] * lp[1:2], axis=-1, keepdims=True))
           - jnp.exp(jnp.sum(lp[2:3] * lp[3:4], axis=-1, keepdims=True)) + lam_init)

    def rows(tile):
        return pl.ds(pl.multiple_of(tile * tq, tq), tq)

    def score(tile, s_write, m_write):
        q = q_ref[0, rows(tile), :]
        lane = lax.broadcasted_iota(jnp.int32, q.shape, 1)
        zero = jnp.zeros_like(q)
        q2 = jnp.concatenate([jnp.where(lane < DIFF_HEAD_DIM, q, zero),
                              jnp.where(lane >= DIFF_HEAD_DIM, q, zero)], axis=0)
        sc = _dot_nt(q2, kall[...])
        s_write[...] = sc.reshape(s_write.shape)
        m_write[...] = jnp.broadcast_to(sc.max(axis=-1, keepdims=True), (2 * tq, m_write.shape[2])).reshape(m_write.shape)

    def finish(tile, s_read, m_read):
        outs = []
        for w in range(2):
            p = jnp.exp2(s_read[w] - m_read[w][:, 0:1]).astype(BF16)
            acc = _dot(p, vaug[...])
            outs.append(acc[:, 0:DIFF_V_HEAD] / acc[:, DIFF_V_HEAD:])
        o = outs[0] - lam * outs[1]
        ms = jnp.mean(o * o, axis=-1, keepdims=True)
        o = o * lax.rsqrt(ms + EPS) * gain_ref[...] * (1.0 - lam_init)
        o_ref[0, rows(tile), :] = o.astype(BF16)

    score(0, s_a, m_a)
    n_pairs = (nq - 1) // 2

    def pair(j, carry):
        score(2 * j + 1, s_b, m_b)
        finish(2 * j, s_a, m_a)
        score(2 * j + 2, s_a, m_a)
        finish(2 * j + 1, s_b, m_b)
        return carry

    lax.fori_loop(0, n_pairs, pair, 0)
    done = 2 * n_pairs
    if (nq - 1) % 2 == 1:
        score(done + 1, s_b, m_b)
        finish(done, s_a, m_a)
        finish(done + 1, s_b, m_b)
    else:
        finish(done, s_a, m_a)


def _diff_attention(qn, segments, lam_p, gain, lam_init, nb, tq_total, tq):
    hd = 2 * DIFF_HEAD_DIM
    in_specs = [pl.BlockSpec((1, tq_total, hd), lambda b, h: (b, 0, h))]
    args = [qn]
    seg_len = tuple(seg[4] for seg in segments)
    tk_all = sum(seg_len)
    for (ka, kc, va, vc, tk) in segments:
        in_specs.append(pl.BlockSpec((1, tk, hd), lambda b, h, kc=kc: (b, 0, kc + h)))
        in_specs.append(pl.BlockSpec((1, tk, DIFF_V_HEAD), lambda b, h, vc=vc: (b, 0, vc + h)))
        args += [ka, va]
    in_specs.append(pl.BlockSpec((4, DIFF_HEAD_DIM), lambda b, h: (0, 0)))
    in_specs.append(pl.BlockSpec((1, DIFF_V_HEAD), lambda b, h: (0, 0)))
    args += [lam_p, gain]
    scratch = [
        pltpu.VMEM((tk_all, hd), BF16),
        pltpu.VMEM((tk_all, 2 * DIFF_V_HEAD), BF16),
        pltpu.VMEM((2, tq, tk_all), F32), pltpu.VMEM((2, tq, tk_all), F32),
        pltpu.VMEM((2, tq, hd), F32), pltpu.VMEM((2, tq, hd), F32),
    ]
    return pl.pallas_call(
        functools.partial(_diffattn_kernel, seg_len, lam_init, tq),
        grid=(nb, DIFF_HEADS),
        in_specs=in_specs,
        out_specs=pl.BlockSpec((1, tq_total, DIFF_V_HEAD), lambda b, h: (b, 0, h)),
        out_shape=jax.ShapeDtypeStruct((nb, tq_total, DIFF_V), BF16),
        scratch_shapes=scratch,
        compiler_params=_params(("arbitrary", "arbitrary")),
        name="diff_attention",
    )(*args)


def _gla_tables():
    c = GLA_C
    nl = len(GLA_LEVELS)
    mst = np.zeros((2, GLA_MROWS, c), np.float32)
    xm = np.zeros((2, nl + 1, c), np.float32)
    pm = np.zeros((2, nl + 1, c, c), np.float32)
    idx = np.arange(c)
    for d in range(2):
        ip = idx if d == 0 else c - 1 - idx
        mst[d, 0:c] = (ip[None, :] <= ip[:, None])
        mst[d, c] = 1.0
        shared, full = c + 1, c + GLA_SHARED
        pm[d, 0] = np.eye(c)
        xm[d, 0] = 1.0
        for li, s in enumerate(GLA_LEVELS):
            ref = (ip // (2 * s)) * (2 * s) + s - 1
            if s in GLA_COARSE:
                for first in range(0, c, 2 * s):
                    mst[d, shared] = (ip <= ref[first])
                    shared += 1
            else:
                mst[d, full:full + c] = (ip[None, :] <= ref[:, None])
                full += c
            odd = (ip // s) % 2 == 1
            xm[d, li + 1] = odd
            same = (ip[:, None] // (2 * s)) == (ip[None, :] // (2 * s))
            pm[d, li + 1] = odd[:, None] & (~odd)[None, :] & same
    pm = np.tile(pm, (1, 1, 1, GLA_HEADS))
    rows = np.arange(GLA_HEADS * c)[:, None] // c
    hm = (rows == (np.arange(GLA_K)[None, :] // GLA_DK)).astype(np.float32)
    ykeep = np.concatenate([np.ones((2, 1, c), np.float32), 1.0 - xm[:, 1:]], axis=1)
    hmy = hm[None, None] * np.tile(ykeep, (1, 1, GLA_HEADS))[:, :, :, None]
    xq = np.broadcast_to(xm[:, 1:, :, None], (2, nl, c, GLA_K)).copy()
    vbd = (rows == (np.arange(GLA_V)[None, :] // GLA_DV)).astype(np.float32)
    return mst, hmy, xq, pm, vbd


def _gla_kernel(bb, q_ref, k_ref, v_ref, g_ref, mst_ref, hmy_ref, xq_ref, pm_ref, vbd_ref, s0_ref,
                o_ref, sfin_ref, s_scr):
    c_idx = pl.program_id(2)

    @pl.when(c_idx == 0)
    def _():
        s_scr[...] = s0_ref[:, 0]

    c = GLA_C
    nl = len(GLA_LEVELS)
    m = mst_ref[0]
    seqs = range(bb)

    def stack_keys(yb, li):
        return jnp.concatenate([yb] * GLA_HEADS, axis=0) * hmy_ref[0, li]

    r_all = []
    for b in seqs:
        g1, g2 = _split2(g_ref[b])
        r_all.append(_dot(m, g1) + _dot(m, g2))
    cum = [r[0:c] for r in r_all]
    tot = [r[c:c + 1] for r in r_all]

    def level_ref(r, li):
        s = GLA_LEVELS[li]
        if s in GLA_COARSE:
            first = c + 1 + sum(c // (2 * t) for t in GLA_COARSE if t > s)
            nblk = c // (2 * s)
            return jnp.concatenate([jnp.broadcast_to(r[first + p:first + p + 1], (2 * s, GLA_K))
                                    for p in range(nblk)], axis=0)
        start = c + GLA_SHARED + c * (li - len(GLA_COARSE))
        return r[start:start + c]
    q = [q_ref[b].astype(F32) * (GLA_DK ** -0.5) for b in seqs]
    k = [k_ref[b].astype(F32) for b in seqs]

    o_inter = []
    for b in seqs:
        state = s_scr[b]
        q_dec = stack_keys((q[b] * jnp.exp(cum[b])).astype(BF16), 0)
        o_rows = _dot_nt(q_dec, state.astype(BF16))
        o_inter.append(jnp.concatenate([o_rows[h * c:(h + 1) * c] for h in range(GLA_HEADS)], axis=1))
        k_dec = stack_keys((k[b] * jnp.exp(tot[b] - cum[b])).astype(BF16), 0)
        vf = v_ref[b].astype(F32)
        v_t = jnp.concatenate(
            [jnp.concatenate([vf[:, (2 * p) * GLA_DV:(2 * p + 1) * GLA_DV],
                              vf[:, (2 * p + 1) * GLA_DV:(2 * p + 2) * GLA_DV]], axis=0).T
             for p in range(GLA_HEADS // 2)], axis=1).astype(BF16)
        s_scr[b] = state * jnp.exp(tot[b]) + _dot(v_t, k_dec)

    @pl.when(c_idx == pl.num_programs(2) - 1)
    def _():
        sfin_ref[:, 0] = s_scr[...]

    att = [pm_ref[0, 0] * _dot_nt(q[b].astype(BF16), stack_keys(k_ref[b], 0)) for b in seqs]
    for li in range(nl):
        for b in seqs:
            ref = level_ref(r_all[b], li)
            e = jnp.exp(-jnp.abs(cum[b] - ref))
            xs = (q[b] * e).astype(BF16) * xq_ref[0, li]
            ys = stack_keys((k[b] * e).astype(BF16), li + 1)
            att[b] = att[b] + pm_ref[0, li + 1] * _dot_nt(xs, ys)
    for b in seqs:
        v_bd = jnp.concatenate([v_ref[b]] * GLA_HEADS, axis=0) * vbd_ref[...]
        o_ref[0, b] = o_inter[b] + _dot(att[b].astype(BF16), v_bd)


def _gla(proj3, g3, s0, tables, nb, t, bb):
    mst, hmy, xq, pm, vbd = tables
    nc = t // GLA_C
    nl = len(GLA_LEVELS)

    def chunk(d, c):
        return jnp.where(d == 0, c, nc - 1 - c)

    return pl.pallas_call(
        functools.partial(_gla_kernel, bb),
        grid=(nb // bb, 2, nc),
        in_specs=[
            pl.BlockSpec((bb, GLA_C, GLA_K), lambda b, d, c: (b, chunk(d, c), COL_QG // GLA_K)),
            pl.BlockSpec((bb, GLA_C, GLA_K), lambda b, d, c: (b, chunk(d, c), COL_KG // GLA_K)),
            pl.BlockSpec((bb, GLA_C, GLA_V), lambda b, d, c: (b, chunk(d, c), COL_VG // GLA_V)),
            pl.BlockSpec((bb, GLA_C, GLA_K), lambda b, d, c: (b, chunk(d, c), d)),
            pl.BlockSpec((1, GLA_MROWS, GLA_C), lambda b, d, c: (d, 0, 0)),
            pl.BlockSpec((1, nl + 1, GLA_HEADS * GLA_C, GLA_K), lambda b, d, c: (d, 0, 0, 0)),
            pl.BlockSpec((1, nl, GLA_C, GLA_K), lambda b, d, c: (d, 0, 0, 0)),
            pl.BlockSpec((1, nl + 1, GLA_C, GLA_HEADS * GLA_C), lambda b, d, c: (d, 0, 0, 0)),
            pl.BlockSpec((GLA_HEADS * GLA_C, GLA_V), lambda b, d, c: (0, 0)),
            pl.BlockSpec((bb, 1, GLA_DV, GLA_K), lambda b, d, c: (b, d, 0, 0)),
        ],
        out_specs=[
            pl.BlockSpec((1, bb, GLA_C, GLA_V), lambda b, d, c: (d, b, chunk(d, c), 0)),
            pl.BlockSpec((bb, 1, GLA_DV, GLA_K), lambda b, d, c: (b, d, 0, 0)),
        ],
        out_shape=[
            jax.ShapeDtypeStruct((2, nb, t, GLA_V), F32),
            jax.ShapeDtypeStruct((nb, 2, GLA_DV, GLA_K), F32),
        ],
        scratch_shapes=[pltpu.VMEM((bb, GLA_DV, GLA_K), F32)],
        compiler_params=_params(("arbitrary", "arbitrary", "arbitrary")),
        name="gla",
    )(proj3, proj3, proj3, g3, mst, hmy, xq, pm, vbd, s0)


def _fourier_kernel(t, u_ref, bcs_ref, ct_ref, o_ref, y_scr):
    i = pl.program_id(1)

    @pl.when(i == 0)
    def _():
        rows = min(t, ROW_TILE)
        for r in range(t // rows):
            y = _dot(u_ref[0, r * rows:(r + 1) * rows, :], bcs_ref[...])
            y_scr[r * rows:(r + 1) * rows, :] = y[:, :FOU_W].astype(BF16)
            y_scr[t + r * rows:t + (r + 1) * rows, :] = y[:, FOU_W:].astype(BF16)

    o_ref[0] = _dot(ct_ref[...], y_scr[...]).astype(BF16)


def _fourier(proj3, bcs, ct, nb, t, tm):
    return pl.pallas_call(
        functools.partial(_fourier_kernel, t),
        grid=(nb, t // tm),
        in_specs=[
            pl.BlockSpec((1, t, FOU_W), lambda b, i: (b, 0, COL_UF // FOU_W)),
            pl.BlockSpec((FOU_W, 2 * FOU_W), lambda b, i: (0, 0)),
            pl.BlockSpec((tm, 2 * t), lambda b, i: (i, 0)),
        ],
        out_specs=pl.BlockSpec((1, tm, FOU_W), lambda b, i: (b, i, 0)),
        out_shape=jax.ShapeDtypeStruct((nb, t, FOU_W), BF16),
        scratch_shapes=[pltpu.VMEM((2 * t, FOU_W), BF16)],
        compiler_params=_params(("arbitrary", "arbitrary")),
        name="fourier_mix",
    )(proj3, bcs, ct)


FFT_N = 64
FFT_G = 8
FFT_CW = 256


def _fft_kernel(u_ref, bcs_ref, m1_ref, q2_ref, o_ref, z_scr, a_scr):
    n, gsz, cw = FFT_N, FFT_G, FFT_CW
    rows = n * gsz
    for r in range(n // gsz):
        y = _dot(u_ref[0, r * rows:(r + 1) * rows, :], bcs_ref[...])
        z_scr[gsz * r:gsz * (r + 1)] = y.reshape(gsz, n, 2 * cw)
    for g in range(n // gsz):
        zg = z_scr[:, gsz * g:gsz * (g + 1), :].reshape(rows, 2 * cw)
        rhs = jnp.concatenate([zg[:, :cw], zg[:, cw:]], axis=0).astype(BF16)
        a_scr[g] = _dot(m1_ref[...], rhs).astype(BF16)
    for f in range(n // gsz):
        slabs = [a_scr[g, part * rows + n * f:part * rows + n * (f + 1), :]
                 for part in range(2) for g in range(n // gsz)]
        res = _dot(q2_ref[f], jnp.concatenate(slabs, axis=0))
        o_ref[0, :, gsz * f:gsz * (f + 1), :] = res.reshape(n, gsz, cw)


def _fft_tables():
    n, gsz, t = FFT_N, FFT_G, FFT_N * FFT_N
    k = np.arange(FOU_GROUP_W)
    ang = 2.0 * np.pi * ((k[:, None] * k[None, :]) % FOU_GROUP_W) / FOU_GROUP_W
    eye2 = np.eye(FFT_CW // FOU_GROUP_W)
    bcs = np.concatenate([np.kron(eye2, np.cos(ang)), -np.kron(eye2, np.sin(ang))], axis=1) * FOU_GROUP_W ** -0.5
    i = np.arange(n)
    a1 = 2.0 * np.pi * ((i[:, None] * i[None, :]) % n) / n
    c1 = np.kron(np.cos(a1), np.eye(gsz))
    s1 = np.kron(np.sin(a1), np.eye(gsz))
    m1 = np.block([[c1, s1], [-s1, c1]])
    ngrp = n // gsz
    shape = (ngrp, n * gsz, n * gsz)
    f = lax.broadcasted_iota(jnp.int32, shape, 0)
    row = lax.broadcasted_iota(jnp.int32, shape, 1)
    col = lax.broadcasted_iota(jnp.int32, shape, 2)
    f2, a = row // gsz, row % gsz
    g, ap, j = col // (gsz * gsz), (col // gsz) % gsz, col % gsz
    freq = gsz * f + a + n * f2
    theta = (((gsz * g + j) * freq) % t).astype(F32) * (2.0 * math.pi / t)
    keep = (a == ap)
    scale = t ** -0.5
    qc = jnp.where(keep, jnp.cos(theta) * scale, 0.0)
    qs = jnp.where(keep, jnp.sin(theta) * scale, 0.0)
    q2 = jnp.concatenate([qc, qs], axis=2).astype(BF16)
    return jnp.asarray(bcs, BF16), jnp.asarray(m1, BF16), q2


def _fourier_fft(proj3, tables, nb):
    bcs, m1, q2 = tables
    n, gsz, cw = FFT_N, FFT_G, FFT_CW
    t = n * n
    once = pl.Buffered(1)
    out = pl.pallas_call(
        _fft_kernel,
        grid=(nb, FOU_W // cw),
        in_specs=[
            pl.BlockSpec((1, t, cw), lambda b, h: (b, 0, COL_UF // cw + h)),
            pl.BlockSpec((cw, 2 * cw), lambda b, h: (0, 0), pipeline_mode=once),
            pl.BlockSpec((2 * n * gsz, 2 * n * gsz), lambda b, h: (0, 0), pipeline_mode=once),
            pl.BlockSpec((n // gsz, n * gsz, 2 * n * gsz), lambda b, h: (0, 0, 0), pipeline_mode=once),
        ],
        out_specs=pl.BlockSpec((1, n, n, cw), lambda b, h: (b, 0, 0, h)),
        out_shape=jax.ShapeDtypeStruct((nb, n, n, FOU_W), F32),
        scratch_shapes=[pltpu.VMEM((n, n, 2 * cw), F32), pltpu.VMEM((n // gsz, 2 * n * gsz, cw), BF16)],
        compiler_params=_params(("arbitrary", "arbitrary")),
        name="fourier_fft",
    )(proj3, bcs, m1, q2)
    return out.reshape(nb, t, FOU_W)


def _dft_tables(t):
    lo = min(t, 64)
    hi = t // lo
    f = jnp.arange(t, dtype=jnp.int32)[None, :]
    unit = 2.0 * math.pi / t
    a = ((lo * jnp.arange(hi, dtype=jnp.int32)[:, None] * f) % t).astype(F32) * unit
    b = ((jnp.arange(lo, dtype=jnp.int32)[:, None] * f) % t).astype(F32) * unit
    ca, sa = jnp.cos(a)[:, None, :], jnp.sin(a)[:, None, :]
    cb, sb = jnp.cos(b)[None, :, :], jnp.sin(b)[None, :, :]
    scale = t ** -0.5
    cos = ((ca * cb - sa * sb) * scale).reshape(t, t)
    msin = ((sa * cb + ca * sb) * -scale).reshape(t, t)
    return jnp.concatenate([cos, msin], axis=1).astype(BF16)


def _channel_dft_table():
    n = np.arange(FOU_GROUP_W)
    ang = 2.0 * np.pi * ((n[:, None] * n[None, :]) % FOU_GROUP_W) / FOU_GROUP_W
    eye = np.eye(FOU_GROUPS)
    bc = np.kron(eye, np.cos(ang)) * FOU_GROUP_W ** -0.5
    bs = np.kron(eye, np.sin(ang)) * FOU_GROUP_W ** -0.5
    return np.concatenate([bc, bs], axis=1).astype(np.float32)


def _merge_kernel(x_ref, g1_ref, fm_ref, of_ref, ob_ref, r_ref, od_ref, gt_ref,
                  wf_ref, wg_ref, wd_ref, wo_ref, gn_ref, grp_ref, o_ref):
    y_f = _dot(fm_ref[...].astype(BF16), wf_ref[...])
    og = of_ref[0] + ob_ref[0]
    ms = _dot((og * og).astype(BF16), grp_ref[...])
    r = r_ref[...].astype(F32)
    og = og * lax.rsqrt(ms + EPS) * gn_ref[...] * (r * jax.nn.sigmoid(r))
    y_g = _dot(og.astype(BF16), wg_ref[...])
    y_d = _dot(od_ref[...], wd_ref[...])
    gates = jax.nn.sigmoid(gt_ref[...].astype(F32))
    merged = (gates[:, 0:D_MODEL] * y_f + gates[:, D_MODEL:2 * D_MODEL] * y_g
              + gates[:, 2 * D_MODEL:3 * D_MODEL] * y_d)
    o_ref[...] = x_ref[...] + g1_ref[0] * _dot(merged.astype(BF16), wo_ref[...])


def _merge(x, g1, proj, fm, o_gla, od, w_fou, w_gla_o, w_diff_o, w_out, gn, grp, tokens_per_cond, tm):
    m = x.shape[0]
    per = tokens_per_cond // tm
    ncond = g1.shape[0]
    cidx = (lambda i: (i // per, 0, 0)) if ncond > 1 else (lambda i: (0, 0, 0))
    once = pl.Buffered(1)
    half = pl.BlockSpec((FOU_W, D_MODEL), lambda i: (0, 0), pipeline_mode=once)
    return pl.pallas_call(
        _merge_kernel,
        grid=(m // tm,),
        in_specs=[
            pl.BlockSpec((tm, D_MODEL), lambda i: (i, 0)),
            pl.BlockSpec((1, 1, D_MODEL), cidx),
            pl.BlockSpec((tm, FOU_W), lambda i: (i, 0)),
            pl.BlockSpec((1, tm, GLA_V), lambda i: (0, i, 0)),
            pl.BlockSpec((1, tm, GLA_V), lambda i: (1, i, 0)),
            pl.BlockSpec((tm, GLA_V), lambda i: (i, COL_RG // GLA_V)),
            pl.BlockSpec((tm, DIFF_V), lambda i: (i, 0)),
            pl.BlockSpec((tm, N_BRANCH * D_MODEL), lambda i: (i, COL_GATES)),
            half, half, half,
            pl.BlockSpec((D_MODEL, D_MODEL), lambda i: (0, 0), pipeline_mode=once),
            pl.BlockSpec((1, GLA_V), lambda i: (0, 0)),
            pl.BlockSpec((GLA_V, GLA_V), lambda i: (0, 0)),
        ],
        out_specs=pl.BlockSpec((tm, D_MODEL), lambda i: (i, 0)),
        out_shape=jax.ShapeDtypeStruct((m, D_MODEL), F32),
        compiler_params=_params(("arbitrary",)),
        name="merge",
    )(x, g1, fm, o_gla, o_gla, proj, od, proj, w_fou, w_gla_o, w_diff_o, w_out, gn, grp)


def _ffn_kernel(tf, x_ref, sc_ref, sh_ref, g2_ref, gain_ref, wg_ref, wu_ref, wd_ref, o_ref, h_scr):
    x = x_ref[...]
    ms = jnp.mean(x * x, axis=-1, keepdims=True)
    scale = gain_ref[...] * (1.0 + sc_ref[0])
    h_scr[...] = (x * lax.rsqrt(ms + EPS) * scale + sh_ref[0]).astype(BF16)
    acc = None
    for f in range(D_FF // tf):
        cols = slice(f * tf, (f + 1) * tf)
        gate = _dot(h_scr[...], wg_ref[:, cols])
        up = _dot(h_scr[...], wu_ref[:, cols])
        a = (gate * jax.nn.sigmoid(gate) * up).astype(BF16)
        part = _dot(a, wd_ref[cols, :])
        acc = part if acc is None else acc + part
    o_ref[...] = x_ref[...] + g2_ref[0] * acc


def _ffn(x, sc, sh, g2, gain, w_gate, w_up, w_down, tokens_per_cond, tm):
    m = x.shape[0]
    tf = FFN_TILE_F
    per = tokens_per_cond // tm
    ncond = sc.shape[0]
    cidx = (lambda i: (i // per, 0, 0)) if ncond > 1 else (lambda i: (0, 0, 0))
    const = lambda i: (0, 0)
    once = pl.Buffered(1)
    return pl.pallas_call(
        functools.partial(_ffn_kernel, tf),
        grid=(m // tm,),
        in_specs=[
            pl.BlockSpec((tm, D_MODEL), lambda i: (i, 0)),
            pl.BlockSpec((1, 1, D_MODEL), cidx),
            pl.BlockSpec((1, 1, D_MODEL), cidx),
            pl.BlockSpec((1, 1, D_MODEL), cidx),
            pl.BlockSpec((1, D_MODEL), const),
            pl.BlockSpec((D_MODEL, D_FF), const, pipeline_mode=once),
            pl.BlockSpec((D_MODEL, D_FF), const, pipeline_mode=once),
            pl.BlockSpec((D_FF, D_MODEL), const, pipeline_mode=once),
        ],
        out_specs=pl.BlockSpec((tm, D_MODEL), lambda i: (i, 0)),
        out_shape=jax.ShapeDtypeStruct((m, D_MODEL), F32),
        scratch_shapes=[pltpu.VMEM((tm, D_MODEL), BF16)],
        compiler_params=_params(("arbitrary",)),
        name="ffn",
    )(x, sc, sh, g2, gain, w_gate, w_up, w_down)


def _group_mean_matrix(width, group):
    idx = np.arange(width) // group
    return (idx[:, None] == idx[None, :]).astype(np.float32) / group


def _rotate_half_matrix():
    r = np.zeros((DIFF_HEAD_DIM, DIFF_HEAD_DIM), np.float32)
    for axis in range(2):
        base = axis * ROPE_AXIS_DIM
        for f in range(ROPE_FREQS):
            r[base + ROPE_FREQS + f, base + f] = -1.0
            r[base + f, base + ROPE_FREQS + f] = 1.0
    return np.kron(np.eye(DIFF_QK // DIFF_HEAD_DIM), r).astype(np.float32)


def _rotate_half_perm():
    p = np.zeros((DIFF_HEAD_DIM,), np.int32)
    for axis in range(2):
        base = axis * ROPE_AXIS_DIM
        for f in range(ROPE_FREQS):
            p[base + f] = base + ROPE_FREQS + f
            p[base + ROPE_FREQS + f] = base + f
    return p


def _rope_tables(n_tokens):
    rows = n_tokens // GRID_W
    row = jnp.repeat(jnp.arange(rows), GRID_W).astype(F32)
    col = jnp.tile(jnp.arange(GRID_W), rows).astype(F32)
    inv = ROPE_BASE ** (-jnp.arange(ROPE_FREQS, dtype=F32) * 2.0 / ROPE_AXIS_DIM)
    ang_r = row[:, None] * inv
    ang_c = col[:, None] * inv
    ang = jnp.concatenate([ang_r, ang_r, ang_c, ang_c], axis=-1)
    reps = DIFF_QK // DIFF_HEAD_DIM
    return jnp.tile(jnp.cos(ang), (1, reps)), jnp.tile(jnp.sin(ang), (1, reps))


def _pack_state(s):
    return jnp.transpose(s, (0, 1, 4, 2, 3)).reshape(s.shape[0], 2, GLA_DV, GLA_K)


def _unpack_state(sp):
    s = sp.reshape(sp.shape[0], 2, GLA_DV, GLA_HEADS, GLA_DK)
    return jnp.transpose(s, (0, 1, 3, 4, 2))


def _run_path(x, mods, layers, consts, nb, t, ctx):
    m = nb * t
    tm, tq, gla_bb, dft_rows = _choose_tiles(nb, t)
    x = x.reshape(m, D_MODEL)
    is_latent = ctx is not None
    cos, sin = consts['rope'] if is_latent else (consts['ones'], consts['ones'])
    ct = consts['ct'][t]
    k_out, v_out, s_out = [], [], []
    for l, p in enumerate(layers):
        sh1, sc1, g1, sh2, sc2, g2 = [mods[l][:, None, j * D_MODEL:(j + 1) * D_MODEL] for j in range(6)]
        outs = _in_projection(x, sc1, sh1, p['norm1'], p['w_main'], p['w_a'], p['w2'], p['b2'],
                              p['qk_gains'], consts['grp64'], consts['rot'], cos, sin,
                              nb, t, is_latent, not is_latent, tm)
        proj, g, qn, kn = outs[:4]
        proj3 = proj.reshape(nb, t, PROJ_W)
        lam_init = 0.8 - 0.6 * math.exp(-0.3 * l)
        segments = [(kn, 0, proj3, COL_VD // DIFF_V_HEAD, t)]
        if is_latent:
            ck = ctx[1][:, l].reshape(nb, -1, DIFF_QK).astype(BF16)
            cv = ctx[2][:, l].reshape(nb, -1, DIFF_V).astype(BF16)
            segments.append((ck, 0, cv, 0, ck.shape[1]))
        od = _diff_attention(qn.reshape(nb, t, DIFF_QK), segments, p['diff_lambda'], p['diff_norm'],
                             lam_init, nb, t, tq)

        if is_latent:
            s0 = _pack_state(ctx[0][:, l])
        else:
            s0 = jnp.zeros((nb, 2, GLA_DV, GLA_K), F32)
        o_gla, s_fin = _gla(proj3, g.reshape(nb, t, 2 * GLA_K), s0, consts['gla'], nb, t, gla_bb)

        if t == FFT_N * FFT_N:
            fm = _fourier_fft(proj3, consts['fft'], nb)
        else:
            fm = _fourier(proj3, consts['bcs'], ct, nb, t, dft_rows)

        x = _merge(x, g1, proj, fm.reshape(m, FOU_W), o_gla.reshape(2, m, GLA_V), od.reshape(m, DIFF_V),
                   p['w_fou'], p['w_gla_o'], p['w_diff_o'], p['w_out'], p['gla_norm'], consts['grp128'],
                   t, tm)
        x = _ffn(x, sc2, sh2, g2, p['norm2'], p['w_ff_gate'], p['w_ff_up'], p['w_ff_down'], t, tm)

        if not is_latent:
            k_out.append(outs[4].reshape(nb, t, DIFF_HEADS, 2, DIFF_HEAD_DIM))
            v_out.append(proj3[:, :, COL_VD:COL_VD + DIFF_V].astype(F32).reshape(nb, t, DIFF_HEADS, DIFF_V_HEAD))
            s_out.append(_unpack_state(s_fin))
    return x.reshape(nb, t, D_MODEL), k_out, v_out, s_out


def kernel(x_prompt, x_sample, c, cache_diff_k, cache_diff_v, state_gla, c_ctx, w_mod, b_mod, norm1, norm2, w_in, w_gla_a2, b_gla_a, gla_norm, diff_qk_norm, diff_lambda, diff_norm, w_fou, w_gla_o, w_diff_o, w_out, w_ff_gate, w_ff_up, w_ff_down):
    nb_ctx, t_ctx, _ = x_prompt.shape
    nb_lat, t_lat, _ = x_sample.shape

    n_cond = 1 + nb_lat
    r_pad = -(-n_cond // 8) * 8
    cond = jnp.zeros((r_pad, D_MODEL), F32).at[0].set(c_ctx).at[1:n_cond].set(c)
    mods = _modulation(cond, w_mod, b_mod)
    mods_ctx = [mods[l, 0:1] for l in range(DEPTH)]
    mods_lat = [mods[l, 1:n_cond] for l in range(DEPTH)]

    a0 = 2048
    a1 = a0 + 2 * GATE_RANK
    pieces = {
        'u_f': (0, 512), 'q_g': (512, 768), 'k_g': (768, 1024), 'v_g': (1024, 1536), 'r_g': (1536, 2048),
        'q_d': (a1, a1 + 512), 'k_d': (a1 + 512, a1 + 1024), 'v_d': (a1 + 1024, a1 + 1536),
        'gates': (a1 + 1536, a1 + 1536 + 3072),
    }
    order = ['gates', 'u_f', 'v_g', 'r_g', 'v_d', 'q_g', 'k_g', 'q_d', 'k_d']
    w_main = jnp.concatenate([w_in[:, :, pieces[n][0]:pieces[n][1]] for n in order], axis=-1).astype(BF16)
    w_a = jnp.pad(w_in[:, :, a0:a1], ((0, 0), (0, 0), (0, A_PAD - 2 * GATE_RANK))).astype(BF16)
    w2 = jnp.zeros((DEPTH, A_PAD, 2 * GLA_K), F32)
    w2 = w2.at[:, 0:GATE_RANK, 0:GLA_K].set(w_gla_a2[:, 0])
    w2 = w2.at[:, GATE_RANK:2 * GATE_RANK, GLA_K:].set(w_gla_a2[:, 1])
    b2 = b_gla_a.reshape(DEPTH, 1, 2 * GLA_K)

    perm = _rotate_half_perm()
    reps = DIFF_QK // DIFF_HEAD_DIM
    layers = []
    for l in range(DEPTH):
        gq = diff_qk_norm[l, 0]
        gk = diff_qk_norm[l, 1]
        layers.append({
            'norm1': norm1[l][None, :], 'norm2': norm2[l][None, :],
            'w_main': w_main[l], 'w_a': w_a[l], 'w2': w2[l], 'b2': b2[l],
            'qk_gains': tuple(jnp.tile(v, reps)[None, :] for v in (gq, gk, gq[perm], gk[perm])),
            'diff_lambda': diff_lambda[l], 'diff_norm': diff_norm[l][None, :],
            'gla_norm': jnp.tile(gla_norm[l], GLA_HEADS)[None, :],
            'w_fou': w_fou[l].astype(BF16), 'w_gla_o': w_gla_o[l].astype(BF16),
            'w_diff_o': w_diff_o[l].astype(BF16), 'w_out': w_out[l].astype(BF16),
            'w_ff_gate': w_ff_gate[l].astype(BF16), 'w_ff_up': w_ff_up[l].astype(BF16),
            'w_ff_down': w_ff_down[l].astype(BF16),
        })

    mst, hmy, xq, pm, vbd = _gla_tables()
    consts = {
        'grp64': jnp.asarray(_group_mean_matrix(DIFF_QK, DIFF_HEAD_DIM), BF16),
        'grp128': jnp.asarray(_group_mean_matrix(GLA_V, GLA_DV), BF16),
        'rot': jnp.asarray(_rotate_half_matrix(), BF16),
        'rope': _rope_tables(t_lat),
        'ones': jnp.ones((t_ctx, DIFF_QK), F32),
        'bcs': jnp.asarray(_channel_dft_table(), BF16),
        'ct': {tt: (None if tt == FFT_N * FFT_N else _dft_tables(tt)) for tt in {t_ctx, t_lat}},
        'fft': _fft_tables(),
        'gla': (jnp.asarray(mst, BF16), jnp.asarray(hmy, BF16), jnp.asarray(xq, BF16), jnp.asarray(pm),
                jnp.asarray(vbd, BF16)),
    }

    y_prompt, k_list, v_list, s_list = _run_path(x_prompt, mods_ctx, layers, consts, nb_ctx, t_ctx, None)
    y_sample, _, _, _ = _run_path(x_sample, mods_lat, layers, consts, nb_lat, t_lat,
                                  (state_gla, cache_diff_k, cache_diff_v))
    return (y_prompt, y_sample, jnp.stack(k_list, axis=1), jnp.stack(v_list, axis=1),
            jnp.stack(s_list, axis=1))
```

```python
import functools
import math

import numpy as np
import jax
import jax.numpy as jnp
from jax import lax
from jax.experimental import pallas as pl
from jax.experimental.pallas import tpu as pltpu

F32 = jnp.float32
BF16 = jnp.bfloat16

D_MODEL = 1024
DEPTH = 4
GRID_W = 64
FOU_GROUPS = 4
FOU_GROUP_W = 128
FOU_W = 512
GLA_HEADS = 4
GLA_DK = 64
GLA_DV = 128
GLA_K = 256
GLA_V = 512
GATE_RANK = 16
GATE_TEMP = 16.0
DIFF_HEADS = 4
DIFF_HEAD_DIM = 64
DIFF_V_HEAD = 128
DIFF_QK = 512
DIFF_V = 512
ROPE_AXIS_DIM = 32
ROPE_FREQS = 16
ROPE_BASE = 10000.0
N_BRANCH = 3
D_FF = 2816
EPS = 1e-6

COL_GATES = 0
COL_UF = 3072
COL_VG = 3584
COL_RG = 4096
COL_VD = 4608
COL_QG = 5120
COL_KG = 5376
PROJ_W = 5632
W_COLS = PROJ_W + 2 * DIFF_QK
A_PAD = 128

SUBLANES = 8
LANES = 128
VMEM_LIMIT = 48 * 1024 * 1024
VMEM_LIMIT_BIG = 56 * 1024 * 1024

GLA_C = 64
GLA_LEVELS = (32, 16, 8, 4, 2, 1)
GLA_COARSE = tuple(s for s in GLA_LEVELS if 2 * s >= SUBLANES)
GLA_SHARED = -(-(1 + sum(GLA_C // (2 * s) for s in GLA_COARSE)) // SUBLANES) * SUBLANES
GLA_MROWS = GLA_C * (1 + len(GLA_LEVELS) - len(GLA_COARSE)) + GLA_SHARED
GLA_BB = 8
ROW_TILE = 512
ATTN_TQ = 256
PROJ_TILE_N = 512
FFN_TILE_F = 256
MOD_TILE_N = 768
DFT_ROWS = 256
LOG2_E = math.log2(math.e)


def _choose_tiles(nb, t):
    tm = ROW_TILE if t % ROW_TILE == 0 else t
    tq = ATTN_TQ if t > ATTN_TQ else t // 2
    return tm, tq, math.gcd(nb, GLA_BB), min(t, DFT_ROWS)


def _params(sem, vmem=VMEM_LIMIT):
    return pltpu.CompilerParams(dimension_semantics=sem, vmem_limit_bytes=vmem)


def _dot(a, b):
    return jnp.dot(a, b, preferred_element_type=F32)


def _dot_nt(a, b):
    return lax.dot_general(a, b, (((1,), (1,)), ((), ())), preferred_element_type=F32)


def _dot_tn(a, b):
    return lax.dot_general(a, b, (((0,), (0,)), ((), ())), preferred_element_type=F32)


def _split2(x):
    x1 = x.astype(BF16)
    return x1, (x - x1.astype(F32)).astype(BF16)


def _split3(x):
    x1 = x.astype(BF16)
    r1 = x - x1.astype(F32)
    x2 = r1.astype(BF16)
    x3 = (r1 - x2.astype(F32)).astype(BF16)
    return x1, x2, x3


def _mod_kernel(c_ref, w_ref, b_ref, o_ref):
    c = c_ref[...]
    s = c * jax.nn.sigmoid(c)
    s1, s2, s3 = _split3(s)
    w = w_ref[0]
    w1, w2, w3 = _split3(w)
    acc = _dot(s1, w1) + (_dot(s1, w2) + _dot(s2, w1)) + (_dot(s2, w2) + _dot(s1, w3) + _dot(s3, w1))
    o_ref[0] = acc + b_ref[0]


def _modulation(cond, w_mod, b_mod):
    r = cond.shape[0]
    tn = MOD_TILE_N
    n = 6 * D_MODEL
    return pl.pallas_call(
        _mod_kernel,
        grid=(DEPTH, n // tn),
        in_specs=[
            pl.BlockSpec((r, D_MODEL), lambda l, j: (0, 0)),
            pl.BlockSpec((1, D_MODEL, tn), lambda l, j: (l, 0, j)),
            pl.BlockSpec((1, 1, tn), lambda l, j: (l, 0, j)),
        ],
        out_specs=pl.BlockSpec((1, r, tn), lambda l, j: (l, 0, j)),
        out_shape=jax.ShapeDtypeStruct((DEPTH, r, n), F32),
        compiler_params=_params(("arbitrary", "arbitrary")),
        name="modulation",
    )(cond, w_mod, b_mod.reshape(DEPTH, 1, n))


def _inproj_kernel(tn, use_rope, want_f32, *refs):
    (x_ref, sc_ref, sh_ref, gain_ref, w_ref, wa_ref, w2_ref, b2_ref,
     gq_ref, gk_ref, gqp_ref, gkp_ref, grp_ref, rot_ref, cos_ref, sin_ref) = refs[:16]
    proj_ref, g_ref, qn_ref, kn_ref = refs[16:20]
    kf_ref = refs[20] if want_f32 else None
    h_scr = refs[-1]

    x = x_ref[...]
    ms = jnp.mean(x * x, axis=-1, keepdims=True)
    scale = gain_ref[...] * (1.0 + sc_ref[0])
    hb = (x * lax.rsqrt(ms + EPS) * scale + sh_ref[0]).astype(BF16)
    h_scr[...] = hb

    def tile(j):
        proj_ref[:, j * tn:(j + 1) * tn] = _dot(h_scr[...], w_ref[:, j * tn:(j + 1) * tn]).astype(BF16)

    a = _dot(hb, wa_ref[...])
    y_q = _dot(h_scr[...], w_ref[:, PROJ_W:PROJ_W + DIFF_QK])
    y_k = _dot(h_scr[...], w_ref[:, PROJ_W + DIFF_QK:PROJ_W + 2 * DIFF_QK])
    tile(0)
    a1, a2 = _split2(a)
    v1, v2 = _split2(w2_ref[...])
    yb_q, yb_k = y_q.astype(BF16), y_k.astype(BF16)
    sq_q, sq_k = (y_q * y_q).astype(BF16), (y_k * y_k).astype(BF16)
    z = _dot(a1, v1) + (_dot(a1, v2) + _dot(a2, v1)) + b2_ref[...]
    ms_q = _dot(sq_q, grp_ref[...])
    ms_k = _dot(sq_k, grp_ref[...])
    if use_rope:
        rot_q = _dot(yb_q, rot_ref[...])
        rot_k = _dot(yb_k, rot_ref[...])
    tile(1)
    tile(2)
    logsig = jnp.minimum(z, 0.0) - jnp.log1p(jnp.exp(-jnp.abs(z)))
    g_ref[...] = logsig * (1.0 / GATE_TEMP)

    def normed(y, msq, rot, g_ref_, gp_ref_, out_scale):
        r = lax.rsqrt(msq + EPS)
        out = y * r * g_ref_[...]
        if use_rope:
            out = out * cos_ref[...] + rot * r * gp_ref_[...] * sin_ref[...]
        return out * out_scale

    tile(3)
    qn_ref[...] = normed(y_q, ms_q, rot_q if use_rope else None, gq_ref, gqp_ref,
                         DIFF_HEAD_DIM ** -0.5 * LOG2_E).astype(BF16)
    tile(4)
    kn = normed(y_k, ms_k, rot_k if use_rope else None, gk_ref, gkp_ref, 1.0)
    kn_ref[0] = kn.astype(BF16)
    if want_f32:
        kf_ref[...] = kn
    for j in range(5, PROJ_W // tn):
        tile(j)


def _in_projection(x, sc, sh, gain, w_main, w_a, w2, b2, qk_gains, grp, rot, cos, sin,
                   nb, t, use_rope, want_f32, tm):
    m = x.shape[0]
    tn = PROJ_TILE_N
    per = t // tm
    ncond = sc.shape[0]
    cidx = (lambda i: (i // per, 0, 0)) if ncond > 1 else (lambda i: (0, 0, 0))
    const = lambda i: (0, 0)
    once = pl.Buffered(1)
    vec = pl.BlockSpec((1, DIFF_QK), const)
    mat = pl.BlockSpec((DIFF_QK, DIFF_QK), const, pipeline_mode=once)
    tab = pl.BlockSpec((tm, DIFF_QK), lambda i: (i % per, 0))
    out_specs = [
        pl.BlockSpec((tm, PROJ_W), lambda i: (i, 0)),
        pl.BlockSpec((tm, 2 * GLA_K), lambda i: (i, 0)),
        pl.BlockSpec((tm, DIFF_QK), lambda i: (i, 0)),
        pl.BlockSpec((1, tm, DIFF_QK), lambda i: (i // per, i % per, 0)),
    ]
    out_shape = [
        jax.ShapeDtypeStruct((m, PROJ_W), BF16),
        jax.ShapeDtypeStruct((m, 2 * GLA_K), F32),
        jax.ShapeDtypeStruct((m, DIFF_QK), BF16),
        jax.ShapeDtypeStruct((nb, t, DIFF_QK), BF16),
    ]
    if want_f32:
        out_specs.append(pl.BlockSpec((tm, DIFF_QK), lambda i: (i, 0)))
        out_shape.append(jax.ShapeDtypeStruct((m, DIFF_QK), F32))
    gq, gk, gqp, gkp = qk_gains
    return pl.pallas_call(
        functools.partial(_inproj_kernel, tn, use_rope, want_f32),
        grid=(m // tm,),
        in_specs=[
            pl.BlockSpec((tm, D_MODEL), lambda i: (i, 0)),
            pl.BlockSpec((1, 1, D_MODEL), cidx),
            pl.BlockSpec((1, 1, D_MODEL), cidx),
            pl.BlockSpec((1, D_MODEL), const),
            pl.BlockSpec((D_MODEL, W_COLS), const, pipeline_mode=once),
            pl.BlockSpec((D_MODEL, A_PAD), const, pipeline_mode=once),
            pl.BlockSpec((A_PAD, 2 * GLA_K), const, pipeline_mode=once),
            pl.BlockSpec((1, 2 * GLA_K), const),
            vec, vec, vec, vec, mat, mat, tab, tab,
        ],
        out_specs=out_specs,
        out_shape=out_shape,
        scratch_shapes=[pltpu.VMEM((tm, D_MODEL), BF16)],
        compiler_params=_params(("arbitrary",), VMEM_LIMIT_BIG),
        name="in_projection",
    )(x, sc, sh, gain, w_main, w_a, w2, b2, gq, gk, gqp, gkp, grp, rot, cos, sin)


def _diffattn_kernel(seg_len, lam_init, tq, *refs):
    n_seg = len(seg_len)
    q_ref = refs[0]
    kv_refs = refs[1:1 + 2 * n_seg]
    lam_ref, gain_ref, o_ref = refs[1 + 2 * n_seg:4 + 2 * n_seg]
    kall, vt, s_a, s_b, m_a, m_b = refs[4 + 2 * n_seg:]
    nq = q_ref.shape[1] // tq

    off = 0
    for s, tk in enumerate(seg_len):
        kall[off:off + tk, :] = kv_refs[2 * s][0]
        vt[0:DIFF_V_HEAD, off:off + tk] = kv_refs[2 * s + 1][0].astype(F32).T.astype(BF16)
        off += tk
    pad = vt.shape[0] - DIFF_V_HEAD
    first = lax.broadcasted_iota(jnp.int32, (pad, vt.shape[1]), 0) == 0
    vt[DIFF_V_HEAD:, :] = jnp.where(first, 1.0, 0.0).astype(BF16)

    lp = lam_ref[...]
    lam = (jnp.exp(jnp.sum(lp[0:1] * lp[1:2], axis=-1, keepdims=True))
           - jnp.exp(jnp.sum(lp[2:3] * lp[3:4], axis=-1, keepdims=True)) + lam_init)

    def rows(tile):
        return pl.ds(pl.multiple_of(tile * tq, tq), tq)

    def score(tile, s_write, m_write):
        q = q_ref[0, rows(tile), :]
        lane = lax.broadcasted_iota(jnp.int32, q.shape, 1)
        zero = jnp.zeros_like(q)
        q2 = jnp.concatenate([jnp.where(lane < DIFF_HEAD_DIM, q, zero),
                              jnp.where(lane >= DIFF_HEAD_DIM, q, zero)], axis=0)
        sc = _dot_nt(kall[...], q2)
        s_write[...] = sc
        m_write[...] = jnp.broadcast_to(sc.max(axis=0, keepdims=True), m_write.shape)

    def finish(tile, s_read, m_read):
        p = jnp.exp2(s_read[...] - m_read[0:1, :]).astype(BF16)
        acc = _dot(vt[...], p)
        o2 = acc[0:DIFF_V_HEAD] / acc[DIFF_V_HEAD:DIFF_V_HEAD + 1]
        o = (o2[:, 0:tq] - lam * o2[:, tq:]).T
        ms = jnp.mean(o * o, axis=-1, keepdims=True)
        o = o * lax.rsqrt(ms + EPS) * gain_ref[...] * (1.0 - lam_init)
        o_ref[0, rows(tile), :] = o.astype(BF16)

    score(0, s_a, m_a)
    n_pairs = (nq - 1) // 2

    def pair(j, carry):
        score(2 * j + 1, s_b, m_b)
        finish(2 * j, s_a, m_a)
        score(2 * j + 2, s_a, m_a)
        finish(2 * j + 1, s_b, m_b)
        return carry

    lax.fori_loop(0, n_pairs, pair, 0)
    done = 2 * n_pairs
    if (nq - 1) % 2 == 1:
        score(done + 1, s_b, m_b)
        finish(done, s_a, m_a)
        finish(done + 1, s_b, m_b)
    else:
        finish(done, s_a, m_a)


def _diff_attention(qn, segments, lam_p, gain, lam_init, nb, tq_total, tq):
    hd = 2 * DIFF_HEAD_DIM
    in_specs = [pl.BlockSpec((1, tq_total, hd), lambda b, h: (b, 0, h))]
    args = [qn]
    seg_len = tuple(seg[4] for seg in segments)
    tk_all = sum(seg_len)
    for (ka, kc, va, vc, tk) in segments:
        in_specs.append(pl.BlockSpec((1, tk, hd), lambda b, h, kc=kc: (b, 0, kc + h)))
        in_specs.append(pl.BlockSpec((1, tk, DIFF_V_HEAD), lambda b, h, vc=vc: (b, 0, vc + h)))
        args += [ka, va]
    in_specs.append(pl.BlockSpec((4, DIFF_HEAD_DIM), lambda b, h: (0, 0)))
    in_specs.append(pl.BlockSpec((1, DIFF_V_HEAD), lambda b, h: (0, 0)))
    args += [lam_p, gain]
    scratch = [
        pltpu.VMEM((tk_all, hd), BF16),
        pltpu.VMEM((DIFF_V_HEAD + 2 * SUBLANES, tk_all), BF16),
        pltpu.VMEM((tk_all, 2 * tq), F32), pltpu.VMEM((tk_all, 2 * tq), F32),
        pltpu.VMEM((SUBLANES, 2 * tq), F32), pltpu.VMEM((SUBLANES, 2 * tq), F32),
    ]
    return pl.pallas_call(
        functools.partial(_diffattn_kernel, seg_len, lam_init, tq),
        grid=(nb, DIFF_HEADS),
        in_specs=in_specs,
        out_specs=pl.BlockSpec((1, tq_total, DIFF_V_HEAD), lambda b, h: (b, 0, h)),
        out_shape=jax.ShapeDtypeStruct((nb, tq_total, DIFF_V), BF16),
        scratch_shapes=scratch,
        compiler_params=_params(("arbitrary", "arbitrary")),
        name="diff_attention",
    )(*args)


def _gla_tables():
    c = GLA_C
    nl = len(GLA_LEVELS)
    mst = np.zeros((2, GLA_MROWS, c), np.float32)
    xm = np.zeros((2, nl + 1, c), np.float32)
    pm = np.zeros((2, nl + 1, c, c), np.float32)
    idx = np.arange(c)
    for d in range(2):
        ip = idx if d == 0 else c - 1 - idx
        mst[d, 0:c] = (ip[None, :] <= ip[:, None])
        mst[d, c] = 1.0
        shared, full = c + 1, c + GLA_SHARED
        pm[d, 0] = np.eye(c)
        xm[d, 0] = 1.0
        for li, s in enumerate(GLA_LEVELS):
            ref = (ip // (2 * s)) * (2 * s) + s - 1
            if s in GLA_COARSE:
                for first in range(0, c, 2 * s):
                    mst[d, shared] = (ip <= ref[first])
                    shared += 1
            else:
                mst[d, full:full + c] = (ip[None, :] <= ref[:, None])
                full += c
            odd = (ip // s) % 2 == 1
            xm[d, li + 1] = odd
            same = (ip[:, None] // (2 * s)) == (ip[None, :] // (2 * s))
            pm[d, li + 1] = odd[:, None] & (~odd)[None, :] & same
    pm = np.tile(pm, (1, 1, 1, GLA_HEADS))
    rows = np.arange(GLA_HEADS * c)[:, None] // c
    hm = (rows == (np.arange(GLA_K)[None, :] // GLA_DK)).astype(np.float32)
    ykeep = np.concatenate([np.ones((2, 1, c), np.float32), 1.0 - xm[:, 1:]], axis=1)
    hmy = hm[None, None] * np.tile(ykeep, (1, 1, GLA_HEADS))[:, :, :, None]
    xq = np.broadcast_to(xm[:, 1:, :, None], (2, nl, c, GLA_K)).copy()
    vbd = (rows == (np.arange(GLA_V)[None, :] // GLA_DV)).astype(np.float32)
    return mst, hmy, xq, pm, vbd


def _gla_kernel(bb, q_ref, k_ref, v_ref, g_ref, mst_ref, hmy_ref, xq_ref, pm_ref, vbd_ref, s0_ref,
                o_ref, sfin_ref, s_scr):
    c_idx = pl.program_id(2)

    @pl.when(c_idx == 0)
    def _():
        s_scr[...] = s0_ref[:, 0]

    c = GLA_C
    nl = len(GLA_LEVELS)
    m = mst_ref[0]
    seqs = range(bb)

    def stack_keys(yb, li):
        return jnp.concatenate([yb] * GLA_HEADS, axis=0) * hmy_ref[0, li]

    r_all = []
    for b in seqs:
        g1, g2 = _split2(g_ref[b])
        r_all.append(_dot(m, g1) + _dot(m, g2))
    cum = [r[0:c] for r in r_all]
    tot = [r[c:c + 1] for r in r_all]

    def level_ref(r, li):
        s = GLA_LEVELS[li]
        if s in GLA_COARSE:
            first = c + 1 + sum(c // (2 * t) for t in GLA_COARSE if t > s)
            nblk = c // (2 * s)
            return jnp.concatenate([jnp.broadcast_to(r[first + p:first + p + 1], (2 * s, GLA_K))
                                    for p in range(nblk)], axis=0)
        start = c + GLA_SHARED + c * (li - len(GLA_COARSE))
        return r[start:start + c]
    q = [q_ref[b].astype(F32) * (GLA_DK ** -0.5) for b in seqs]
    k = [k_ref[b].astype(F32) for b in seqs]

    o_inter = []
    for b in seqs:
        state = s_scr[b]
        q_dec = stack_keys((q[b] * jnp.exp(cum[b])).astype(BF16), 0)
        o_rows = _dot_nt(q_dec, state.astype(BF16))
        o_inter.append(jnp.concatenate([o_rows[h * c:(h + 1) * c] for h in range(GLA_HEADS)], axis=1))
        k_dec = stack_keys((k[b] * jnp.exp(tot[b] - cum[b])).astype(BF16), 0)
        vf = v_ref[b].astype(F32)
        v_t = jnp.concatenate(
            [jnp.concatenate([vf[:, (2 * p) * GLA_DV:(2 * p + 1) * GLA_DV],
                              vf[:, (2 * p + 1) * GLA_DV:(2 * p + 2) * GLA_DV]], axis=0).T
             for p in range(GLA_HEADS // 2)], axis=1).astype(BF16)
        s_scr[b] = state * jnp.exp(tot[b]) + _dot(v_t, k_dec)

    @pl.when(c_idx == pl.num_programs(2) - 1)
    def _():
        sfin_ref[:, 0] = s_scr[...]

    att = [pm_ref[0, 0] * _dot_nt(q[b].astype(BF16), stack_keys(k_ref[b], 0)) for b in seqs]
    for li in range(nl):
        for b in seqs:
            ref = level_ref(r_all[b], li)
            e = jnp.exp(-jnp.abs(cum[b] - ref))
            xs = (q[b] * e).astype(BF16) * xq_ref[0, li]
            ys = stack_keys((k[b] * e).astype(BF16), li + 1)
            att[b] = att[b] + pm_ref[0, li + 1] * _dot_nt(xs, ys)
    for b in seqs:
        v_bd = jnp.concatenate([v_ref[b]] * GLA_HEADS, axis=0) * vbd_ref[...]
        o_ref[0, b] = o_inter[b] + _dot(att[b].astype(BF16), v_bd)


def _gla(proj3, g3, s0, tables, nb, t, bb):
    mst, hmy, xq, pm, vbd = tables
    nc = t // GLA_C
    nl = len(GLA_LEVELS)

    def chunk(d, c):
        return jnp.where(d == 0, c, nc - 1 - c)

    return pl.pallas_call(
        functools.partial(_gla_kernel, bb),
        grid=(nb // bb, 2, nc),
        in_specs=[
            pl.BlockSpec((bb, GLA_C, GLA_K), lambda b, d, c: (b, chunk(d, c), COL_QG // GLA_K)),
            pl.BlockSpec((bb, GLA_C, GLA_K), lambda b, d, c: (b, chunk(d, c), COL_KG // GLA_K)),
            pl.BlockSpec((bb, GLA_C, GLA_V), lambda b, d, c: (b, chunk(d, c), COL_VG // GLA_V)),
            pl.BlockSpec((bb, GLA_C, GLA_K), lambda b, d, c: (b, chunk(d, c), d)),
            pl.BlockSpec((1, GLA_MROWS, GLA_C), lambda b, d, c: (d, 0, 0)),
            pl.BlockSpec((1, nl + 1, GLA_HEADS * GLA_C, GLA_K), lambda b, d, c: (d, 0, 0, 0)),
            pl.BlockSpec((1, nl, GLA_C, GLA_K), lambda b, d, c: (d, 0, 0, 0)),
            pl.BlockSpec((1, nl + 1, GLA_C, GLA_HEADS * GLA_C), lambda b, d, c: (d, 0, 0, 0)),
            pl.BlockSpec((GLA_HEADS * GLA_C, GLA_V), lambda b, d, c: (0, 0)),
            pl.BlockSpec((bb, 1, GLA_DV, GLA_K), lambda b, d, c: (b, d, 0, 0)),
        ],
        out_specs=[
            pl.BlockSpec((1, bb, GLA_C, GLA_V), lambda b, d, c: (d, b, chunk(d, c), 0)),
            pl.BlockSpec((bb, 1, GLA_DV, GLA_K), lambda b, d, c: (b, d, 0, 0)),
        ],
        out_shape=[
            jax.ShapeDtypeStruct((2, nb, t, GLA_V), F32),
            jax.ShapeDtypeStruct((nb, 2, GLA_DV, GLA_K), F32),
        ],
        scratch_shapes=[pltpu.VMEM((bb, GLA_DV, GLA_K), F32)],
        compiler_params=_params(("arbitrary", "arbitrary", "arbitrary")),
        name="gla",
    )(proj3, proj3, proj3, g3, mst, hmy, xq, pm, vbd, s0)


def _fourier_kernel(t, u_ref, bcs_ref, ct_ref, o_ref, y_scr):
    i = pl.program_id(1)

    @pl.when(i == 0)
    def _():
        rows = min(t, ROW_TILE)
        for r in range(t // rows):
            y = _dot(u_ref[0, r * rows:(r + 1) * rows, :], bcs_ref[...])
            y_scr[r * rows:(r + 1) * rows, :] = y[:, :FOU_W].astype(BF16)
            y_scr[t + r * rows:t + (r + 1) * rows, :] = y[:, FOU_W:].astype(BF16)

    o_ref[0] = _dot(ct_ref[...], y_scr[...]).astype(BF16)


def _fourier(proj3, bcs, ct, nb, t, tm):
    return pl.pallas_call(
        functools.partial(_fourier_kernel, t),
        grid=(nb, t // tm),
        in_specs=[
            pl.BlockSpec((1, t, FOU_W), lambda b, i: (b, 0, COL_UF // FOU_W)),
            pl.BlockSpec((FOU_W, 2 * FOU_W), lambda b, i: (0, 0)),
            pl.BlockSpec((tm, 2 * t), lambda b, i: (i, 0)),
        ],
        out_specs=pl.BlockSpec((1, tm, FOU_W), lambda b, i: (b, i, 0)),
        out_shape=jax.ShapeDtypeStruct((nb, t, FOU_W), BF16),
        scratch_shapes=[pltpu.VMEM((2 * t, FOU_W), BF16)],
        compiler_params=_params(("arbitrary", "arbitrary")),
        name="fourier_mix",
    )(proj3, bcs, ct)


FFT_N = 64
FFT_G = 8
FFT_CW = 256


def _fft_kernel(u_ref, bcs_ref, m1_ref, q2_ref, o_ref, z_scr, a_scr):
    n, gsz, cw = FFT_N, FFT_G, FFT_CW
    rows = n * gsz
    for r in range(n // gsz):
        y = _dot(u_ref[0, r * rows:(r + 1) * rows, :], bcs_ref[...])
        z_scr[gsz * r:gsz * (r + 1)] = y.reshape(gsz, n, 2 * cw)
    for g in range(n // gsz):
        zg = z_scr[:, gsz * g:gsz * (g + 1), :].reshape(rows, 2 * cw)
        rhs = jnp.concatenate([zg[:, :cw], zg[:, cw:]], axis=0).astype(BF16)
        a_scr[g] = _dot(m1_ref[...], rhs).astype(BF16)
    for f in range(n // gsz):
        slabs = [a_scr[g, part * rows + n * f:part * rows + n * (f + 1), :]
                 for part in range(2) for g in range(n // gsz)]
        res = _dot(q2_ref[f], jnp.concatenate(slabs, axis=0))
        o_ref[0, :, gsz * f:gsz * (f + 1), :] = res.reshape(n, gsz, cw)


def _fft_tables():
    n, gsz, t = FFT_N, FFT_G, FFT_N * FFT_N
    k = np.arange(FOU_GROUP_W)
    ang = 2.0 * np.pi * ((k[:, None] * k[None, :]) % FOU_GROUP_W) / FOU_GROUP_W
    eye2 = np.eye(FFT_CW // FOU_GROUP_W)
    bcs = np.concatenate([np.kron(eye2, np.cos(ang)), -np.kron(eye2, np.sin(ang))], axis=1) * FOU_GROUP_W ** -0.5
    i = np.arange(n)
    a1 = 2.0 * np.pi * ((i[:, None] * i[None, :]) % n) / n
    c1 = np.kron(np.cos(a1), np.eye(gsz))
    s1 = np.kron(np.sin(a1), np.eye(gsz))
    m1 = np.block([[c1, s1], [-s1, c1]])
    ngrp = n // gsz
    shape = (ngrp, n * gsz, n * gsz)
    f = lax.broadcasted_iota(jnp.int32, shape, 0)
    row = lax.broadcasted_iota(jnp.int32, shape, 1)
    col = lax.broadcasted_iota(jnp.int32, shape, 2)
    f2, a = row // gsz, row % gsz
    g, ap, j = col // (gsz * gsz), (col // gsz) % gsz, col % gsz
    freq = gsz * f + a + n * f2
    theta = (((gsz * g + j) * freq) % t).astype(F32) * (2.0 * math.pi / t)
    keep = (a == ap)
    scale = t ** -0.5
    qc = jnp.where(keep, jnp.cos(theta) * scale, 0.0)
    qs = jnp.where(keep, jnp.sin(theta) * scale, 0.0)
    q2 = jnp.concatenate([qc, qs], axis=2).astype(BF16)
    return jnp.asarray(bcs, BF16), jnp.asarray(m1, BF16), q2


def _fourier_fft(proj3, tables, nb):
    bcs, m1, q2 = tables
    n, gsz, cw = FFT_N, FFT_G, FFT_CW
    t = n * n
    once = pl.Buffered(1)
    out = pl.pallas_call(
        _fft_kernel,
        grid=(nb, FOU_W // cw),
        in_specs=[
            pl.BlockSpec((1, t, cw), lambda b, h: (b, 0, COL_UF // cw + h)),
            pl.BlockSpec((cw, 2 * cw), lambda b, h: (0, 0), pipeline_mode=once),
            pl.BlockSpec((2 * n * gsz, 2 * n * gsz), lambda b, h: (0, 0), pipeline_mode=once),
            pl.BlockSpec((n // gsz, n * gsz, 2 * n * gsz), lambda b, h: (0, 0, 0), pipeline_mode=once),
        ],
        out_specs=pl.BlockSpec((1, n, n, cw), lambda b, h: (b, 0, 0, h)),
        out_shape=jax.ShapeDtypeStruct((nb, n, n, FOU_W), F32),
        scratch_shapes=[pltpu.VMEM((n, n, 2 * cw), F32), pltpu.VMEM((n // gsz, 2 * n * gsz, cw), BF16)],
        compiler_params=_params(("arbitrary", "arbitrary")),
        name="fourier_fft",
    )(proj3, bcs, m1, q2)
    return out.reshape(nb, t, FOU_W)


def _dft_tables(t):
    lo = min(t, 64)
    hi = t // lo
    f = jnp.arange(t, dtype=jnp.int32)[None, :]
    unit = 2.0 * math.pi / t
    a = ((lo * jnp.arange(hi, dtype=jnp.int32)[:, None] * f) % t).astype(F32) * unit
    b = ((jnp.arange(lo, dtype=jnp.int32)[:, None] * f) % t).astype(F32) * unit
    ca, sa = jnp.cos(a)[:, None, :], jnp.sin(a)[:, None, :]
    cb, sb = jnp.cos(b)[None, :, :], jnp.sin(b)[None, :, :]
    scale = t ** -0.5
    cos = ((ca * cb - sa * sb) * scale).reshape(t, t)
    msin = ((sa * cb + ca * sb) * -scale).reshape(t, t)
    return jnp.concatenate([cos, msin], axis=1).astype(BF16)


def _channel_dft_table():
    n = np.arange(FOU_GROUP_W)
    ang = 2.0 * np.pi * ((n[:, None] * n[None, :]) % FOU_GROUP_W) / FOU_GROUP_W
    eye = np.eye(FOU_GROUPS)
    bc = np.kron(eye, np.cos(ang)) * FOU_GROUP_W ** -0.5
    bs = np.kron(eye, np.sin(ang)) * FOU_GROUP_W ** -0.5
    return np.concatenate([bc, bs], axis=1).astype(np.float32)


def _merge_kernel(x_ref, g1_ref, fm_ref, of_ref, ob_ref, r_ref, od_ref, gt_ref,
                  wf_ref, wg_ref, wd_ref, wo_ref, gn_ref, grp_ref, o_ref):
    y_f = _dot(fm_ref[...].astype(BF16), wf_ref[...])
    og = of_ref[0] + ob_ref[0]
    ms = _dot((og * og).astype(BF16), grp_ref[...])
    r = r_ref[...].astype(F32)
    og = og * lax.rsqrt(ms + EPS) * gn_ref[...] * (r * jax.nn.sigmoid(r))
    y_g = _dot(og.astype(BF16), wg_ref[...])
    y_d = _dot(od_ref[...], wd_ref[...])
    gates = jax.nn.sigmoid(gt_ref[...].astype(F32))
    merged = (gates[:, 0:D_MODEL] * y_f + gates[:, D_MODEL:2 * D_MODEL] * y_g
              + gates[:, 2 * D_MODEL:3 * D_MODEL] * y_d)
    o_ref[...] = x_ref[...] + g1_ref[0] * _dot(merged.astype(BF16), wo_ref[...])


def _merge(x, g1, proj, fm, o_gla, od, w_fou, w_gla_o, w_diff_o, w_out, gn, grp, tokens_per_cond, tm):
    m = x.shape[0]
    per = tokens_per_cond // tm
    ncond = g1.shape[0]
    cidx = (lambda i: (i // per, 0, 0)) if ncond > 1 else (lambda i: (0, 0, 0))
    once = pl.Buffered(1)
    half = pl.BlockSpec((FOU_W, D_MODEL), lambda i: (0, 0), pipeline_mode=once)
    return pl.pallas_call(
        _merge_kernel,
        grid=(m // tm,),
        in_specs=[
            pl.BlockSpec((tm, D_MODEL), lambda i: (i, 0)),
            pl.BlockSpec((1, 1, D_MODEL), cidx),
            pl.BlockSpec((tm, FOU_W), lambda i: (i, 0)),
            pl.BlockSpec((1, tm, GLA_V), lambda i: (0, i, 0)),
            pl.BlockSpec((1, tm, GLA_V), lambda i: (1, i, 0)),
            pl.BlockSpec((tm, GLA_V), lambda i: (i, COL_RG // GLA_V)),
            pl.BlockSpec((tm, DIFF_V), lambda i: (i, 0)),
            pl.BlockSpec((tm, N_BRANCH * D_MODEL), lambda i: (i, COL_GATES)),
            half, half, half,
            pl.BlockSpec((D_MODEL, D_MODEL), lambda i: (0, 0), pipeline_mode=once),
            pl.BlockSpec((1, GLA_V), lambda i: (0, 0)),
            pl.BlockSpec((GLA_V, GLA_V), lambda i: (0, 0)),
        ],
        out_specs=pl.BlockSpec((tm, D_MODEL), lambda i: (i, 0)),
        out_shape=jax.ShapeDtypeStruct((m, D_MODEL), F32),
        compiler_params=_params(("arbitrary",)),
        name="merge",
    )(x, g1, fm, o_gla, o_gla, proj, od, proj, w_fou, w_gla_o, w_diff_o, w_out, gn, grp)


def _ffn_kernel(tf, x_ref, sc_ref, sh_ref, g2_ref, gain_ref, wg_ref, wu_ref, wd_ref, o_ref, h_scr):
    x = x_ref[...]
    ms = jnp.mean(x * x, axis=-1, keepdims=True)
    scale = gain_ref[...] * (1.0 + sc_ref[0])
    h_scr[...] = (x * lax.rsqrt(ms + EPS) * scale + sh_ref[0]).astype(BF16)
    acc = None
    for f in range(D_FF // tf):
        cols = slice(f * tf, (f + 1) * tf)
        gate = _dot(h_scr[...], wg_ref[:, cols])
        up = _dot(h_scr[...], wu_ref[:, cols])
        a = (gate * jax.nn.sigmoid(gate) * up).astype(BF16)
        part = _dot(a, wd_ref[cols, :])
        acc = part if acc is None else acc + part
    o_ref[...] = x_ref[...] + g2_ref[0] * acc


def _ffn(x, sc, sh, g2, gain, w_gate, w_up, w_down, tokens_per_cond, tm):
    m = x.shape[0]
    tf = FFN_TILE_F
    per = tokens_per_cond // tm
    ncond = sc.shape[0]
    cidx = (lambda i: (i // per, 0, 0)) if ncond > 1 else (lambda i: (0, 0, 0))
    const = lambda i: (0, 0)
    once = pl.Buffered(1)
    return pl.pallas_call(
        functools.partial(_ffn_kernel, tf),
        grid=(m // tm,),
        in_specs=[
            pl.BlockSpec((tm, D_MODEL), lambda i: (i, 0)),
            pl.BlockSpec((1, 1, D_MODEL), cidx),
            pl.BlockSpec((1, 1, D_MODEL), cidx),
            pl.BlockSpec((1, 1, D_MODEL), cidx),
            pl.BlockSpec((1, D_MODEL), const),
            pl.BlockSpec((D_MODEL, D_FF), const, pipeline_mode=once),
            pl.BlockSpec((D_MODEL, D_FF), const, pipeline_mode=once),
            pl.BlockSpec((D_FF, D_MODEL), const, pipeline_mode=once),
        ],
        out_specs=pl.BlockSpec((tm, D_MODEL), lambda i: (i, 0)),
        out_shape=jax.ShapeDtypeStruct((m, D_MODEL), F32),
        scratch_shapes=[pltpu.VMEM((tm, D_MODEL), BF16)],
        compiler_params=_params(("arbitrary",)),
        name="ffn",
    )(x, sc, sh, g2, gain, w_gate, w_up, w_down)


def _group_mean_matrix(width, group):
    idx = np.arange(width) // group
    return (idx[:, None] == idx[None, :]).astype(np.float32) / group


def _rotate_half_matrix():
    r = np.zeros((DIFF_HEAD_DIM, DIFF_HEAD_DIM), np.float32)
    for axis in range(2):
        base = axis * ROPE_AXIS_DIM
        for f in range(ROPE_FREQS):
            r[base + ROPE_FREQS + f, base + f] = -1.0
            r[base + f, base + ROPE_FREQS + f] = 1.0
    return np.kron(np.eye(DIFF_QK // DIFF_HEAD_DIM), r).astype(np.float32)


def _rotate_half_perm():
    p = np.zeros((DIFF_HEAD_DIM,), np.int32)
    for axis in range(2):
        base = axis * ROPE_AXIS_DIM
        for f in range(ROPE_FREQS):
            p[base + f] = base + ROPE_FREQS + f
            p[base + ROPE_FREQS + f] = base + f
    return p


def _rope_tables(n_tokens):
    rows = n_tokens // GRID_W
    row = jnp.repeat(jnp.arange(rows), GRID_W).astype(F32)
    col = jnp.tile(jnp.arange(GRID_W), rows).astype(F32)
    inv = ROPE_BASE ** (-jnp.arange(ROPE_FREQS, dtype=F32) * 2.0 / ROPE_AXIS_DIM)
    ang_r = row[:, None] * inv
    ang_c = col[:, None] * inv
    ang = jnp.concatenate([ang_r, ang_r, ang_c, ang_c], axis=-1)
    reps = DIFF_QK // DIFF_HEAD_DIM
    return jnp.tile(jnp.cos(ang), (1, reps)), jnp.tile(jnp.sin(ang), (1, reps))


def _pack_state(s):
    return jnp.transpose(s, (0, 1, 4, 2, 3)).reshape(s.shape[0], 2, GLA_DV, GLA_K)


def _unpack_state(sp):
    s = sp.reshape(sp.shape[0], 2, GLA_DV, GLA_HEADS, GLA_DK)
    return jnp.transpose(s, (0, 1, 3, 4, 2))


def _run_path(x, mods, layers, consts, nb, t, ctx):
    m = nb * t
    tm, tq, gla_bb, dft_rows = _choose_tiles(nb, t)
    x = x.reshape(m, D_MODEL)
    is_latent = ctx is not None
    cos, sin = consts['rope'] if is_latent else (consts['ones'], consts['ones'])
    ct = consts['ct'][t]
    k_out, v_out, s_out = [], [], []
    for l, p in enumerate(layers):
        sh1, sc1, g1, sh2, sc2, g2 = [mods[l][:, None, j * D_MODEL:(j + 1) * D_MODEL] for j in range(6)]
        outs = _in_projection(x, sc1, sh1, p['norm1'], p['w_main'], p['w_a'], p['w2'], p['b2'],
                              p['qk_gains'], consts['grp64'], consts['rot'], cos, sin,
                              nb, t, is_latent, not is_latent, tm)
        proj, g, qn, kn = outs[:4]
        proj3 = proj.reshape(nb, t, PROJ_W)
        lam_init = 0.8 - 0.6 * math.exp(-0.3 * l)
        segments = [(kn, 0, proj3, COL_VD // DIFF_V_HEAD, t)]
        if is_latent:
            ck = ctx[1][:, l].reshape(nb, -1, DIFF_QK).astype(BF16)
            cv = ctx[2][:, l].reshape(nb, -1, DIFF_V).astype(BF16)
            segments.append((ck, 0, cv, 0, ck.shape[1]))
        od = _diff_attention(qn.reshape(nb, t, DIFF_QK), segments, p['diff_lambda'], p['diff_norm'],
                             lam_init, nb, t, tq)

        if is_latent:
            s0 = _pack_state(ctx[0][:, l])
        else:
            s0 = jnp.zeros((nb, 2, GLA_DV, GLA_K), F32)
        o_gla, s_fin = _gla(proj3, g.reshape(nb, t, 2 * GLA_K), s0, consts['gla'], nb, t, gla_bb)

        if t == FFT_N * FFT_N:
            fm = _fourier_fft(proj3, consts['fft'], nb)
        else:
            fm = _fourier(proj3, consts['bcs'], ct, nb, t, dft_rows)

        x = _merge(x, g1, proj, fm.reshape(m, FOU_W), o_gla.reshape(2, m, GLA_V), od.reshape(m, DIFF_V),
                   p['w_fou'], p['w_gla_o'], p['w_diff_o'], p['w_out'], p['gla_norm'], consts['grp128'],
                   t, tm)
        x = _ffn(x, sc2, sh2, g2, p['norm2'], p['w_ff_gate'], p['w_ff_up'], p['w_ff_down'], t, tm)

        if not is_latent:
            k_out.append(outs[4].reshape(nb, t, DIFF_HEADS, 2, DIFF_HEAD_DIM))
            v_out.append(proj3[:, :, COL_VD:COL_VD + DIFF_V].astype(F32).reshape(nb, t, DIFF_HEADS, DIFF_V_HEAD))
            s_out.append(_unpack_state(s_fin))
    return x.reshape(nb, t, D_MODEL), k_out, v_out, s_out


def kernel(x_prompt, x_sample, c, cache_diff_k, cache_diff_v, state_gla, c_ctx, w_mod, b_mod, norm1, norm2, w_in, w_gla_a2, b_gla_a, gla_norm, diff_qk_norm, diff_lambda, diff_norm, w_fou, w_gla_o, w_diff_o, w_out, w_ff_gate, w_ff_up, w_ff_down):
    nb_ctx, t_ctx, _ = x_prompt.shape
    nb_lat, t_lat, _ = x_sample.shape

    n_cond = 1 + nb_lat
    r_pad = -(-n_cond // 8) * 8
    cond = jnp.zeros((r_pad, D_MODEL), F32).at[0].set(c_ctx).at[1:n_cond].set(c)
    mods = _modulation(cond, w_mod, b_mod)
    mods_ctx = [mods[l, 0:1] for l in range(DEPTH)]
    mods_lat = [mods[l, 1:n_cond] for l in range(DEPTH)]

    a0 = 2048
    a1 = a0 + 2 * GATE_RANK
    pieces = {
        'u_f': (0, 512), 'q_g': (512, 768), 'k_g': (768, 1024), 'v_g': (1024, 1536), 'r_g': (1536, 2048),
        'q_d': (a1, a1 + 512), 'k_d': (a1 + 512, a1 + 1024), 'v_d': (a1 + 1024, a1 + 1536),
        'gates': (a1 + 1536, a1 + 1536 + 3072),
    }
    order = ['gates', 'u_f', 'v_g', 'r_g', 'v_d', 'q_g', 'k_g', 'q_d', 'k_d']
    w_main = jnp.concatenate([w_in[:, :, pieces[n][0]:pieces[n][1]] for n in order], axis=-1).astype(BF16)
    w_a = jnp.pad(w_in[:, :, a0:a1], ((0, 0), (0, 0), (0, A_PAD - 2 * GATE_RANK))).astype(BF16)
    w2 = jnp.zeros((DEPTH, A_PAD, 2 * GLA_K), F32)
    w2 = w2.at[:, 0:GATE_RANK, 0:GLA_K].set(w_gla_a2[:, 0])
    w2 = w2.at[:, GATE_RANK:2 * GATE_RANK, GLA_K:].set(w_gla_a2[:, 1])
    b2 = b_gla_a.reshape(DEPTH, 1, 2 * GLA_K)

    perm = _rotate_half_perm()
    reps = DIFF_QK // DIFF_HEAD_DIM
    layers = []
    for l in range(DEPTH):
        gq = diff_qk_norm[l, 0]
        gk = diff_qk_norm[l, 1]
        layers.append({
            'norm1': norm1[l][None, :], 'norm2': norm2[l][None, :],
            'w_main': w_main[l], 'w_a': w_a[l], 'w2': w2[l], 'b2': b2[l],
            'qk_gains': tuple(jnp.tile(v, reps)[None, :] for v in (gq, gk, gq[perm], gk[perm])),
            'diff_lambda': diff_lambda[l], 'diff_norm': diff_norm[l][None, :],
            'gla_norm': jnp.tile(gla_norm[l], GLA_HEADS)[None, :],
            'w_fou': w_fou[l].astype(BF16), 'w_gla_o': w_gla_o[l].astype(BF16),
            'w_diff_o': w_diff_o[l].astype(BF16), 'w_out': w_out[l].astype(BF16),
            'w_ff_gate': w_ff_gate[l].astype(BF16), 'w_ff_up': w_ff_up[l].astype(BF16),
            'w_ff_down': w_ff_down[l].astype(BF16),
        })

    mst, hmy, xq, pm, vbd = _gla_tables()
    consts = {
        'grp64': jnp.asarray(_group_mean_matrix(DIFF_QK, DIFF_HEAD_DIM), BF16),
        'grp128': jnp.asarray(_group_mean_matrix(GLA_V, GLA_DV), BF16),
        'rot': jnp.asarray(_rotate_half_matrix(), BF16),
        'rope': _rope_tables(t_lat),
        'ones': jnp.ones((t_ctx, DIFF_QK), F32),
        'bcs': jnp.asarray(_channel_dft_table(), BF16),
        'ct': {tt: (None if tt == FFT_N * FFT_N else _dft_tables(tt)) for tt in {t_ctx, t_lat}},
        'fft': _fft_tables(),
        'gla': (jnp.asarray(mst, BF16), jnp.asarray(hmy, BF16), jnp.asarray(xq, BF16), jnp.asarray(pm),
                jnp.asarray(vbd, BF16)),
    }

    y_prompt, k_list, v_list, s_list = _run_path(x_prompt, mods_ctx, layers, consts, nb_ctx, t_ctx, None)
    y_sample, _, _, _ = _run_path(x_sample, mods_lat, layers, consts, nb_lat, t_lat,
                                  (state_gla, cache_diff_k, cache_diff_v))
    return (y_prompt, y_sample, jnp.stack(k_list, axis=1), jnp.stack(v_list, axis=1),
            jnp.stack(s_list, axis=1))
```

```python
import functools
import math

import numpy as np
import jax
import jax.numpy as jnp
from jax import lax
from jax.experimental import pallas as pl
from jax.experimental.pallas import tpu as pltpu

F32 = jnp.float32
BF16 = jnp.bfloat16

D_MODEL = 1024
DEPTH = 4
GRID_W = 64
FOU_GROUPS = 4
FOU_GROUP_W = 128
FOU_W = 512
GLA_HEADS = 4
GLA_DK = 64
GLA_DV = 128
GLA_K = 256
GLA_V = 512
GATE_RANK = 16
GATE_TEMP = 16.0
DIFF_HEADS = 4
DIFF_HEAD_DIM = 64
DIFF_V_HEAD = 128
DIFF_QK = 512
DIFF_V = 512
ROPE_AXIS_DIM = 32
ROPE_FREQS = 16
ROPE_BASE = 10000.0
N_BRANCH = 3
D_FF = 2816
EPS = 1e-6

COL_GATES = 0
COL_UF = 3072
COL_VG = 3584
COL_RG = 4096
COL_VD = 4608
COL_QG = 5120
COL_KG = 5376
PROJ_W = 5632
W_COLS = PROJ_W + 2 * DIFF_QK
A_PAD = 128

SUBLANES = 8
VMEM_LIMIT = 48 * 1024 * 1024
VMEM_LIMIT_BIG = 56 * 1024 * 1024

GLA_C = 64
GLA_LEVELS = (32, 16, 8, 4, 2, 1)
GLA_COARSE = tuple(s for s in GLA_LEVELS if 2 * s >= SUBLANES)
GLA_SHARED = -(-(1 + sum(GLA_C // (2 * s) for s in GLA_COARSE)) // SUBLANES) * SUBLANES
GLA_MROWS = GLA_C * (1 + len(GLA_LEVELS) - len(GLA_COARSE)) + GLA_SHARED
GLA_BB = 8
ROW_TILE = 512
ATTN_TQ = 256
PROJ_TILE_N = 512
FFN_TILE_F = 256
MOD_TILE_N = 768
DFT_ROWS = 256
LOG2_E = math.log2(math.e)


def _choose_tiles(nb, t):
    tm = ROW_TILE if t % ROW_TILE == 0 else t
    tq = ATTN_TQ if t > ATTN_TQ else t // 2
    return tm, tq, math.gcd(nb, GLA_BB), min(t, DFT_ROWS)


def _params(sem, vmem=VMEM_LIMIT):
    return pltpu.CompilerParams(dimension_semantics=sem, vmem_limit_bytes=vmem)


def _dot(a, b):
    return jnp.dot(a, b, preferred_element_type=F32)


def _dot_nt(a, b):
    return lax.dot_general(a, b, (((1,), (1,)), ((), ())), preferred_element_type=F32)


def _split2(x):
    x1 = x.astype(BF16)
    return x1, (x - x1.astype(F32)).astype(BF16)


def _split3(x):
    x1 = x.astype(BF16)
    r1 = x - x1.astype(F32)
    x2 = r1.astype(BF16)
    x3 = (r1 - x2.astype(F32)).astype(BF16)
    return x1, x2, x3


def _mod_kernel(c_ref, w_ref, b_ref, o_ref):
    c = c_ref[...]
    s = c * jax.nn.sigmoid(c)
    s1, s2, s3 = _split3(s)
    w = w_ref[0]
    w1, w2, w3 = _split3(w)
    acc = _dot(s1, w1) + (_dot(s1, w2) + _dot(s2, w1)) + (_dot(s2, w2) + _dot(s1, w3) + _dot(s3, w1))
    o_ref[0] = acc + b_ref[0]


def _modulation(cond, w_mod, b_mod):
    r = cond.shape[0]
    tn = MOD_TILE_N
    n = 6 * D_MODEL
    return pl.pallas_call(
        _mod_kernel,
        grid=(DEPTH, n // tn),
        in_specs=[
            pl.BlockSpec((r, D_MODEL), lambda l, j: (0, 0)),
            pl.BlockSpec((1, D_MODEL, tn), lambda l, j: (l, 0, j)),
            pl.BlockSpec((1, 1, tn), lambda l, j: (l, 0, j)),
        ],
        out_specs=pl.BlockSpec((1, r, tn), lambda l, j: (l, 0, j)),
        out_shape=jax.ShapeDtypeStruct((DEPTH, r, n), F32),
        compiler_params=_params(("arbitrary", "arbitrary")),
        name="modulation",
    )(cond, w_mod, b_mod.reshape(DEPTH, 1, n))


def _inproj_kernel(tn, use_rope, want_f32, *refs):
    (x_ref, sc_ref, sh_ref, gain_ref, w_ref, wa_ref, w2_ref, b2_ref,
     gq_ref, gk_ref, gqp_ref, gkp_ref, grp_ref, rot_ref, cos_ref, sin_ref) = refs[:16]
    proj_ref, g_ref, qn_ref, kn_ref = refs[16:20]
    kf_ref = refs[20] if want_f32 else None
    h_scr = refs[-1]

    x = x_ref[...]
    ms = jnp.mean(x * x, axis=-1, keepdims=True)
    scale = gain_ref[...] * (1.0 + sc_ref[0])
    hb = (x * lax.rsqrt(ms + EPS) * scale + sh_ref[0]).astype(BF16)
    h_scr[...] = hb

    def tile(j):
        proj_ref[:, j * tn:(j + 1) * tn] = _dot(h_scr[...], w_ref[:, j * tn:(j + 1) * tn]).astype(BF16)

    a = _dot(hb, wa_ref[...])
    y_q = _dot(h_scr[...], w_ref[:, PROJ_W:PROJ_W + DIFF_QK])
    y_k = _dot(h_scr[...], w_ref[:, PROJ_W + DIFF_QK:PROJ_W + 2 * DIFF_QK])
    tile(0)
    a1, a2 = _split2(a)
    v1, v2 = _split2(w2_ref[...])
    yb_q, yb_k = y_q.astype(BF16), y_k.astype(BF16)
    sq_q, sq_k = (y_q * y_q).astype(BF16), (y_k * y_k).astype(BF16)
    z = _dot(a1, v1) + (_dot(a1, v2) + _dot(a2, v1)) + b2_ref[...]
    ms_q = _dot(sq_q, grp_ref[...])
    ms_k = _dot(sq_k, grp_ref[...])
    if use_rope:
        rot_q = _dot(yb_q, rot_ref[...])
        rot_k = _dot(yb_k, rot_ref[...])
    tile(1)
    tile(2)
    logsig = jnp.minimum(z, 0.0) - jnp.log1p(jnp.exp(-jnp.abs(z)))
    g_ref[...] = logsig * (1.0 / GATE_TEMP)

    def normed(y, msq, rot, g_ref_, gp_ref_, out_scale):
        r = lax.rsqrt(msq + EPS)
        out = y * r * g_ref_[...]
        if use_rope:
            out = out * cos_ref[...] + rot * r * gp_ref_[...] * sin_ref[...]
        return out * out_scale

    tile(3)
    qn_ref[...] = normed(y_q, ms_q, rot_q if use_rope else None, gq_ref, gqp_ref,
                         DIFF_HEAD_DIM ** -0.5 * LOG2_E).astype(BF16)
    tile(4)
    kn = normed(y_k, ms_k, rot_k if use_rope else None, gk_ref, gkp_ref, 1.0)
    kn_ref[0] = kn.astype(BF16)
    if want_f32:
        kf_ref[...] = kn
    for j in range(5, PROJ_W // tn):
        tile(j)


def _in_projection(x, sc, sh, gain, w_main, w_a, w2, b2, qk_gains, grp, rot, cos, sin,
                   nb, t, use_rope, want_f32, tm):
    m = x.shape[0]
    tn = PROJ_TILE_N
    per = t // tm
    ncond = sc.shape[0]
    cidx = (lambda i: (i // per, 0, 0)) if ncond > 1 else (lambda i: (0, 0, 0))
    const = lambda i: (0, 0)
    once = pl.Buffered(1)
    vec = pl.BlockSpec((1, DIFF_QK), const)
    mat = pl.BlockSpec((DIFF_QK, DIFF_QK), const, pipeline_mode=once)
    tab = pl.BlockSpec((tm, DIFF_QK), lambda i: (i % per, 0))
    out_specs = [
        pl.BlockSpec((tm, PROJ_W), lambda i: (i, 0)),
        pl.BlockSpec((tm, 2 * GLA_K), lambda i: (i, 0)),
        pl.BlockSpec((tm, DIFF_QK), lambda i: (i, 0)),
        pl.BlockSpec((1, tm, DIFF_QK), lambda i: (i // per, i % per, 0)),
    ]
    out_shape = [
        jax.ShapeDtypeStruct((m, PROJ_W), BF16),
        jax.ShapeDtypeStruct((m, 2 * GLA_K), F32),
        jax.ShapeDtypeStruct((m, DIFF_QK), BF16),
        jax.ShapeDtypeStruct((nb, t, DIFF_QK), BF16),
    ]
    if want_f32:
        out_specs.append(pl.BlockSpec((tm, DIFF_QK), lambda i: (i, 0)))
        out_shape.append(jax.ShapeDtypeStruct((m, DIFF_QK), F32))
    gq, gk, gqp, gkp = qk_gains
    return pl.pallas_call(
        functools.partial(_inproj_kernel, tn, use_rope, want_f32),
        grid=(m // tm,),
        in_specs=[
            pl.BlockSpec((tm, D_MODEL), lambda i: (i, 0)),
            pl.BlockSpec((1, 1, D_MODEL), cidx),
            pl.BlockSpec((1, 1, D_MODEL), cidx),
            pl.BlockSpec((1, D_MODEL), const),
            pl.BlockSpec((D_MODEL, W_COLS), const, pipeline_mode=once),
            pl.BlockSpec((D_MODEL, A_PAD), const, pipeline_mode=once),
            pl.BlockSpec((A_PAD, 2 * GLA_K), const, pipeline_mode=once),
            pl.BlockSpec((1, 2 * GLA_K), const),
            vec, vec, vec, vec, mat, mat, tab, tab,
        ],
        out_specs=out_specs,
        out_shape=out_shape,
        scratch_shapes=[pltpu.VMEM((tm, D_MODEL), BF16)],
        compiler_params=_params(("arbitrary",), VMEM_LIMIT_BIG),
        name="in_projection",
    )(x, sc, sh, gain, w_main, w_a, w2, b2, gq, gk, gqp, gkp, grp, rot, cos, sin)


def _diffattn_kernel(seg_len, lam_init, tq, *refs):
    n_seg = len(seg_len)
    q_ref = refs[0]
    kv_refs = refs[1:1 + 2 * n_seg]
    lam_ref, gain_ref, o_ref = refs[1 + 2 * n_seg:4 + 2 * n_seg]
    kall, vt, s_a, s_b, m_a, m_b = refs[4 + 2 * n_seg:]
    nq = q_ref.shape[1] // tq

    off = 0
    for s, tk in enumerate(seg_len):
        kall[off:off + tk, :] = kv_refs[2 * s][0]
        vt[0:DIFF_V_HEAD, off:off + tk] = kv_refs[2 * s + 1][0].astype(F32).T.astype(BF16)
        off += tk
    pad = vt.shape[0] - DIFF_V_HEAD
    first = lax.broadcasted_iota(jnp.int32, (pad, vt.shape[1]), 0) == 0
    vt[DIFF_V_HEAD:, :] = jnp.where(first, 1.0, 0.0).astype(BF16)

    lp = lam_ref[...]
    lam = (jnp.exp(jnp.sum(lp[0:1] * lp[1:2], axis=-1, keepdims=True))
           - jnp.exp(jnp.sum(lp[2:3] * lp[3:4], axis=-1, keepdims=True)) + lam_init)

    def rows(tile):
        return pl.ds(pl.multiple_of(tile * tq, tq), tq)

    def score(tile, s_write, m_write):
        q = q_ref[0, rows(tile), :]
        lane = lax.broadcasted_iota(jnp.int32, q.shape, 1)
        zero = jnp.zeros_like(q)
        q2 = jnp.concatenate([jnp.where(lane < DIFF_HEAD_DIM, q, zero),
                              jnp.where(lane >= DIFF_HEAD_DIM, q, zero)], axis=0)
        sc = _dot_nt(kall[...], q2)
        s_write[...] = sc
        m_write[...] = jnp.broadcast_to(sc.max(axis=0, keepdims=True), m_write.shape)

    def finish(tile, s_read, m_read):
        p = jnp.exp2(s_read[...] - m_read[0:1, :]).astype(BF16)
        acc = _dot(vt[...], p)
        o2 = acc[0:DIFF_V_HEAD] / acc[DIFF_V_HEAD:DIFF_V_HEAD + 1]
        o = (o2[:, 0:tq] - lam * o2[:, tq:]).T
        ms = jnp.mean(o * o, axis=-1, keepdims=True)
        o = o * lax.rsqrt(ms + EPS) * gain_ref[...] * (1.0 - lam_init)
        o_ref[0, rows(tile), :] = o.astype(BF16)

    score(0, s_a, m_a)
    n_pairs = (nq - 1) // 2

    def pair(j, carry):
        score(2 * j + 1, s_b, m_b)
        finish(2 * j, s_a, m_a)
        score(2 * j + 2, s_a, m_a)
        finish(2 * j + 1, s_b, m_b)
        return carry

    lax.fori_loop(0, n_pairs, pair, 0)
    done = 2 * n_pairs
    if (nq - 1) % 2 == 1:
        score(done + 1, s_b, m_b)
        finish(done, s_a, m_a)
        finish(done + 1, s_b, m_b)
    else:
        finish(done, s_a, m_a)


def _diff_attention(qn, segments, lam_p, gain, lam_init, nb, tq_total, tq):
    hd = 2 * DIFF_HEAD_DIM
    in_specs = [pl.BlockSpec((1, tq_total, hd), lambda b, h: (b, 0, h))]
    args = [qn]
    seg_len = tuple(seg[4] for seg in segments)
    tk_all = sum(seg_len)
    for (ka, kc, va, vc, tk) in segments:
        in_specs.append(pl.BlockSpec((1, tk, hd), lambda b, h, kc=kc: (b, 0, kc + h)))
        in_specs.append(pl.BlockSpec((1, tk, DIFF_V_HEAD), lambda b, h, vc=vc: (b, 0, vc + h)))
        args += [ka, va]
    in_specs.append(pl.BlockSpec((4, DIFF_HEAD_DIM), lambda b, h: (0, 0)))
    in_specs.append(pl.BlockSpec((1, DIFF_V_HEAD), lambda b, h: (0, 0)))
    args += [lam_p, gain]
    scratch = [
        pltpu.VMEM((tk_all, hd), BF16),
        pltpu.VMEM((DIFF_V_HEAD + 2 * SUBLANES, tk_all), BF16),
        pltpu.VMEM((tk_all, 2 * tq), F32), pltpu.VMEM((tk_all, 2 * tq), F32),
        pltpu.VMEM((SUBLANES, 2 * tq), F32), pltpu.VMEM((SUBLANES, 2 * tq), F32),
    ]
    return pl.pallas_call(
        functools.partial(_diffattn_kernel, seg_len, lam_init, tq),
        grid=(nb, DIFF_HEADS),
        in_specs=in_specs,
        out_specs=pl.BlockSpec((1, tq_total, DIFF_V_HEAD), lambda b, h: (b, 0, h)),
        out_shape=jax.ShapeDtypeStruct((nb, tq_total, DIFF_V), BF16),
        scratch_shapes=scratch,
        compiler_params=_params(("arbitrary", "arbitrary")),
        name="diff_attention",
    )(*args)


def _gla_tables():
    c = GLA_C
    nl = len(GLA_LEVELS)
    mst = np.zeros((2, GLA_MROWS, c), np.float32)
    xm = np.zeros((2, nl + 1, c), np.float32)
    pm = np.zeros((2, nl + 1, c, c), np.float32)
    idx = np.arange(c)
    for d in range(2):
        ip = idx if d == 0 else c - 1 - idx
        mst[d, 0:c] = (ip[None, :] <= ip[:, None])
        mst[d, c] = 1.0
        shared, full = c + 1, c + GLA_SHARED
        pm[d, 0] = np.eye(c)
        xm[d, 0] = 1.0
        for li, s in enumerate(GLA_LEVELS):
            ref = (ip // (2 * s)) * (2 * s) + s - 1
            if s in GLA_COARSE:
                for first in range(0, c, 2 * s):
                    mst[d, shared] = (ip <= ref[first])
                    shared += 1
            else:
                mst[d, full:full + c] = (ip[None, :] <= ref[:, None])
                full += c
            odd = (ip // s) % 2 == 1
            xm[d, li + 1] = odd
            same = (ip[:, None] // (2 * s)) == (ip[None, :] // (2 * s))
            pm[d, li + 1] = odd[:, None] & (~odd)[None, :] & same
    pm = np.tile(pm, (1, 1, 1, GLA_HEADS))
    rows = np.arange(GLA_HEADS * c)[:, None] // c
    hm = (rows == (np.arange(GLA_K)[None, :] // GLA_DK)).astype(np.float32)
    ykeep = np.concatenate([np.ones((2, 1, c), np.float32), 1.0 - xm[:, 1:]], axis=1)
    hmy = hm[None, None] * np.tile(ykeep, (1, 1, GLA_HEADS))[:, :, :, None]
    xq = np.broadcast_to(xm[:, 1:, :, None], (2, nl, c, GLA_K)).copy()
    vbd = (rows == (np.arange(GLA_V)[None, :] // GLA_DV)).astype(np.float32)
    return mst, hmy, xq, pm, vbd


def _gla_kernel(bb, q_ref, k_ref, v_ref, g_ref, mst_ref, hmy_ref, xq_ref, pm_ref, vbd_ref, s0_ref,
                o_ref, sfin_ref, s_scr):
    c_idx = pl.program_id(2)

    @pl.when(c_idx == 0)
    def _():
        s_scr[...] = s0_ref[:, 0]

    c = GLA_C
    nl = len(GLA_LEVELS)
    m = mst_ref[0]
    seqs = range(bb)

    def stack_keys(yb, li):
        return jnp.concatenate([yb] * GLA_HEADS, axis=0) * hmy_ref[0, li]

    r_all = []
    for b in seqs:
        g1, g2 = _split2(g_ref[b])
        r_all.append(_dot(m, g1) + _dot(m, g2))
    cum = [r[0:c] for r in r_all]
    tot = [r[c:c + 1] for r in r_all]

    def level_ref(r, li):
        s = GLA_LEVELS[li]
        if s in GLA_COARSE:
            first = c + 1 + sum(c // (2 * t) for t in GLA_COARSE if t > s)
            nblk = c // (2 * s)
            return jnp.concatenate([jnp.broadcast_to(r[first + p:first + p + 1], (2 * s, GLA_K))
                                    for p in range(nblk)], axis=0)
        start = c + GLA_SHARED + c * (li - len(GLA_COARSE))
        return r[start:start + c]
    q = [q_ref[b].astype(F32) * (GLA_DK ** -0.5) for b in seqs]
    k = [k_ref[b].astype(F32) for b in seqs]

    o_inter = []
    for b in seqs:
        state = s_scr[b]
        q_dec = stack_keys((q[b] * jnp.exp(cum[b])).astype(BF16), 0)
        o_rows = _dot_nt(q_dec, state.astype(BF16))
        o_inter.append(jnp.concatenate([o_rows[h * c:(h + 1) * c] for h in range(GLA_HEADS)], axis=1))
        k_dec = stack_keys((k[b] * jnp.exp(tot[b] - cum[b])).astype(BF16), 0)
        vf = v_ref[b].astype(F32)
        v_t = jnp.concatenate(
            [jnp.concatenate([vf[:, (2 * p) * GLA_DV:(2 * p + 1) * GLA_DV],
                              vf[:, (2 * p + 1) * GLA_DV:(2 * p + 2) * GLA_DV]], axis=0).T
             for p in range(GLA_HEADS // 2)], axis=1).astype(BF16)
        s_scr[b] = state * jnp.exp(tot[b]) + _dot(v_t, k_dec)

    @pl.when(c_idx == pl.num_programs(2) - 1)
    def _():
        sfin_ref[:, 0] = s_scr[...]

    att = [pm_ref[0, 0] * _dot_nt(q[b].astype(BF16), stack_keys(k_ref[b], 0)) for b in seqs]
    for li in range(nl):
        for b in seqs:
            ref = level_ref(r_all[b], li)
            e = jnp.exp(-jnp.abs(cum[b] - ref))
            xs = (q[b] * e).astype(BF16) * xq_ref[0, li]
            ys = stack_keys((k[b] * e).astype(BF16), li + 1)
            att[b] = att[b] + pm_ref[0, li + 1] * _dot_nt(xs, ys)
    for b in seqs:
        v_bd = jnp.concatenate([v_ref[b]] * GLA_HEADS, axis=0) * vbd_ref[...]
        o_ref[0, b] = o_inter[b] + _dot(att[b].astype(BF16), v_bd)


def _gla(proj3, g3, s0, tables, nb, t, bb):
    mst, hmy, xq, pm, vbd = tables
    nc = t // GLA_C
    nl = len(GLA_LEVELS)

    def chunk(d, c):
        return jnp.where(d == 0, c, nc - 1 - c)

    return pl.pallas_call(
        functools.partial(_gla_kernel, bb),
        grid=(nb // bb, 2, nc),
        in_specs=[
            pl.BlockSpec((bb, GLA_C, GLA_K), lambda b, d, c: (b, chunk(d, c), COL_QG // GLA_K)),
            pl.BlockSpec((bb, GLA_C, GLA_K), lambda b, d, c: (b, chunk(d, c), COL_KG // GLA_K)),
            pl.BlockSpec((bb, GLA_C, GLA_V), lambda b, d, c: (b, chunk(d, c), COL_VG // GLA_V)),
            pl.BlockSpec((bb, GLA_C, GLA_K), lambda b, d, c: (b, chunk(d, c), d)),
            pl.BlockSpec((1, GLA_MROWS, GLA_C), lambda b, d, c: (d, 0, 0)),
            pl.BlockSpec((1, nl + 1, GLA_HEADS * GLA_C, GLA_K), lambda b, d, c: (d, 0, 0, 0)),
            pl.BlockSpec((1, nl, GLA_C, GLA_K), lambda b, d, c: (d, 0, 0, 0)),
            pl.BlockSpec((1, nl + 1, GLA_C, GLA_HEADS * GLA_C), lambda b, d, c: (d, 0, 0, 0)),
            pl.BlockSpec((GLA_HEADS * GLA_C, GLA_V), lambda b, d, c: (0, 0)),
            pl.BlockSpec((bb, 1, GLA_DV, GLA_K), lambda b, d, c: (b, d, 0, 0)),
        ],
        out_specs=[
            pl.BlockSpec((1, bb, GLA_C, GLA_V), lambda b, d, c: (d, b, chunk(d, c), 0)),
            pl.BlockSpec((bb, 1, GLA_DV, GLA_K), lambda b, d, c: (b, d, 0, 0)),
        ],
        out_shape=[
            jax.ShapeDtypeStruct((2, nb, t, GLA_V), F32),
            jax.ShapeDtypeStruct((nb, 2, GLA_DV, GLA_K), F32),
        ],
        scratch_shapes=[pltpu.VMEM((bb, GLA_DV, GLA_K), F32)],
        compiler_params=_params(("arbitrary", "arbitrary", "arbitrary")),
        name="gla",
    )(proj3, proj3, proj3, g3, mst, hmy, xq, pm, vbd, s0)


def _fourier_kernel(t, u_ref, bcs_ref, ct_ref, o_ref, y_scr):
    i = pl.program_id(1)

    @pl.when(i == 0)
    def _():
        rows = min(t, ROW_TILE)
        for r in range(t // rows):
            y = _dot(u_ref[0, r * rows:(r + 1) * rows, :], bcs_ref[...])
            y_scr[r * rows:(r + 1) * rows, :] = y[:, :FOU_W].astype(BF16)
            y_scr[t + r * rows:t + (r + 1) * rows, :] = y[:, FOU_W:].astype(BF16)

    o_ref[0] = _dot(ct_ref[...], y_scr[...]).astype(BF16)


def _fourier(proj3, bcs, ct, nb, t, tm):
    return pl.pallas_call(
        functools.partial(_fourier_kernel, t),
        grid=(nb, t // tm),
        in_specs=[
            pl.BlockSpec((1, t, FOU_W), lambda b, i: (b, 0, COL_UF // FOU_W)),
            pl.BlockSpec((FOU_W, 2 * FOU_W), lambda b, i: (0, 0)),
            pl.BlockSpec((tm, 2 * t), lambda b, i: (i, 0)),
        ],
        out_specs=pl.BlockSpec((1, tm, FOU_W), lambda b, i: (b, i, 0)),
        out_shape=jax.ShapeDtypeStruct((nb, t, FOU_W), BF16),
        scratch_shapes=[pltpu.VMEM((2 * t, FOU_W), BF16)],
        compiler_params=_params(("arbitrary", "arbitrary")),
        name="fourier_mix",
    )(proj3, bcs, ct)


FFT_N = 64
FFT_G = 8
FFT_CW = 256


def _fft_kernel(u_ref, bcs_ref, m1_ref, q2_ref, o_ref, z_scr, a_scr):
    n, gsz, cw = FFT_N, FFT_G, FFT_CW
    rows = n * gsz
    for r in range(n // gsz):
        y = _dot(u_ref[0, r * rows:(r + 1) * rows, :], bcs_ref[...])
        z_scr[gsz * r:gsz * (r + 1)] = y.reshape(gsz, n, 2 * cw)
    for g in range(n // gsz):
        zg = z_scr[:, gsz * g:gsz * (g + 1), :].reshape(rows, 2 * cw)
        rhs = jnp.concatenate([zg[:, :cw], zg[:, cw:]], axis=0).astype(BF16)
        a_scr[g] = _dot(m1_ref[...], rhs).astype(BF16)
    for f in range(n // gsz):
        slabs = [a_scr[g, part * rows + n * f:part * rows + n * (f + 1), :]
                 for part in range(2) for g in range(n // gsz)]
        res = _dot(q2_ref[f], jnp.concatenate(slabs, axis=0))
        o_ref[0, :, gsz * f:gsz * (f + 1), :] = res.reshape(n, gsz, cw)


def _fft_tables():
    n, gsz, t = FFT_N, FFT_G, FFT_N * FFT_N
    k = np.arange(FOU_GROUP_W)
    ang = 2.0 * np.pi * ((k[:, None] * k[None, :]) % FOU_GROUP_W) / FOU_GROUP_W
    eye2 = np.eye(FFT_CW // FOU_GROUP_W)
    bcs = np.concatenate([np.kron(eye2, np.cos(ang)), -np.kron(eye2, np.sin(ang))], axis=1) * FOU_GROUP_W ** -0.5
    i = np.arange(n)
    a1 = 2.0 * np.pi * ((i[:, None] * i[None, :]) % n) / n
    c1 = np.kron(np.cos(a1), np.eye(gsz))
    s1 = np.kron(np.sin(a1), np.eye(gsz))
    m1 = np.block([[c1, s1], [-s1, c1]])
    ngrp = n // gsz
    shape = (ngrp, n * gsz, n * gsz)
    f = lax.broadcasted_iota(jnp.int32, shape, 0)
    row = lax.broadcasted_iota(jnp.int32, shape, 1)
    col = lax.broadcasted_iota(jnp.int32, shape, 2)
    f2, a = row // gsz, row % gsz
    g, ap, j = col // (gsz * gsz), (col // gsz) % gsz, col % gsz
    freq = gsz * f + a + n * f2
    theta = (((gsz * g + j) * freq) % t).astype(F32) * (2.0 * math.pi / t)
    keep = (a == ap)
    scale = t ** -0.5
    qc = jnp.where(keep, jnp.cos(theta) * scale, 0.0)
    qs = jnp.where(keep, jnp.sin(theta) * scale, 0.0)
    q2 = jnp.concatenate([qc, qs], axis=2).astype(BF16)
    return jnp.asarray(bcs, BF16), jnp.asarray(m1, BF16), q2


def _fourier_fft(proj3, tables, nb):
    bcs, m1, q2 = tables
    n, gsz, cw = FFT_N, FFT_G, FFT_CW
    t = n * n
    once = pl.Buffered(1)
    out = pl.pallas_call(
        _fft_kernel,
        grid=(nb, FOU_W // cw),
        in_specs=[
            pl.BlockSpec((1, t, cw), lambda b, h: (b, 0, COL_UF // cw + h)),
            pl.BlockSpec((cw, 2 * cw), lambda b, h: (0, 0), pipeline_mode=once),
            pl.BlockSpec((2 * n * gsz, 2 * n * gsz), lambda b, h: (0, 0), pipeline_mode=once),
            pl.BlockSpec((n // gsz, n * gsz, 2 * n * gsz), lambda b, h: (0, 0, 0), pipeline_mode=once),
        ],
        out_specs=pl.BlockSpec((1, n, n, cw), lambda b, h: (b, 0, 0, h)),
        out_shape=jax.ShapeDtypeStruct((nb, n, n, FOU_W), F32),
        scratch_shapes=[pltpu.VMEM((n, n, 2 * cw), F32), pltpu.VMEM((n // gsz, 2 * n * gsz, cw), BF16)],
        compiler_params=_params(("arbitrary", "arbitrary")),
        name="fourier_fft",
    )(proj3, bcs, m1, q2)
    return out.reshape(nb, t, FOU_W)


def _dft_tables(t):
    lo = min(t, 64)
    hi = t // lo
    f = jnp.arange(t, dtype=jnp.int32)[None, :]
    unit = 2.0 * math.pi / t
    a = ((lo * jnp.arange(hi, dtype=jnp.int32)[:, None] * f) % t).astype(F32) * unit
    b = ((jnp.arange(lo, dtype=jnp.int32)[:, None] * f) % t).astype(F32) * unit
    ca, sa = jnp.cos(a)[:, None, :], jnp.sin(a)[:, None, :]
    cb, sb = jnp.cos(b)[None, :, :], jnp.sin(b)[None, :, :]
    scale = t ** -0.5
    cos = ((ca * cb - sa * sb) * scale).reshape(t, t)
    msin = ((sa * cb + ca * sb) * -scale).reshape(t, t)
    return jnp.concatenate([cos, msin], axis=1).astype(BF16)


def _channel_dft_table():
    n = np.arange(FOU_GROUP_W)
    ang = 2.0 * np.pi * ((n[:, None] * n[None, :]) % FOU_GROUP_W) / FOU_GROUP_W
    eye = np.eye(FOU_GROUPS)
    bc = np.kron(eye, np.cos(ang)) * FOU_GROUP_W ** -0.5
    bs = np.kron(eye, np.sin(ang)) * FOU_GROUP_W ** -0.5
    return np.concatenate([bc, bs], axis=1).astype(np.float32)


def _merge_kernel(x_ref, g1_ref, fm_ref, of_ref, ob_ref, r_ref, od_ref, gt_ref,
                  wf_ref, wg_ref, wd_ref, wo_ref, gn_ref, grp_ref, o_ref):
    y_f = _dot(fm_ref[...].astype(BF16), wf_ref[...])
    og = of_ref[0] + ob_ref[0]
    ms = _dot((og * og).astype(BF16), grp_ref[...])
    r = r_ref[...].astype(F32)
    og = og * lax.rsqrt(ms + EPS) * gn_ref[...] * (r * jax.nn.sigmoid(r))
    y_g = _dot(og.astype(BF16), wg_ref[...])
    y_d = _dot(od_ref[...], wd_ref[...])
    gates = jax.nn.sigmoid(gt_ref[...].astype(F32))
    merged = (gates[:, 0:D_MODEL] * y_f + gates[:, D_MODEL:2 * D_MODEL] * y_g
              + gates[:, 2 * D_MODEL:3 * D_MODEL] * y_d)
    o_ref[...] = x_ref[...] + g1_ref[0] * _dot(merged.astype(BF16), wo_ref[...])


def _merge(x, g1, proj, fm, o_gla, od, w_fou, w_gla_o, w_diff_o, w_out, gn, grp, tokens_per_cond, tm):
    m = x.shape[0]
    per = tokens_per_cond // tm
    ncond = g1.shape[0]
    cidx = (lambda i: (i // per, 0, 0)) if ncond > 1 else (lambda i: (0, 0, 0))
    once = pl.Buffered(1)
    half = pl.BlockSpec((FOU_W, D_MODEL), lambda i: (0, 0), pipeline_mode=once)
    return pl.pallas_call(
        _merge_kernel,
        grid=(m // tm,),
        in_specs=[
            pl.BlockSpec((tm, D_MODEL), lambda i: (i, 0)),
            pl.BlockSpec((1, 1, D_MODEL), cidx),
            pl.BlockSpec((tm, FOU_W), lambda i: (i, 0)),
            pl.BlockSpec((1, tm, GLA_V), lambda i: (0, i, 0)),
            pl.BlockSpec((1, tm, GLA_V), lambda i: (1, i, 0)),
            pl.BlockSpec((tm, GLA_V), lambda i: (i, COL_RG // GLA_V)),
            pl.BlockSpec((tm, DIFF_V), lambda i: (i, 0)),
            pl.BlockSpec((tm, N_BRANCH * D_MODEL), lambda i: (i, COL_GATES)),
            half, half, half,
            pl.BlockSpec((D_MODEL, D_MODEL), lambda i: (0, 0), pipeline_mode=once),
            pl.BlockSpec((1, GLA_V), lambda i: (0, 0)),
            pl.BlockSpec((GLA_V, GLA_V), lambda i: (0, 0)),
        ],
        out_specs=pl.BlockSpec((tm, D_MODEL), lambda i: (i, 0)),
        out_shape=jax.ShapeDtypeStruct((m, D_MODEL), F32),
        compiler_params=_params(("arbitrary",)),
        name="merge",
    )(x, g1, fm, o_gla, o_gla, proj, od, proj, w_fou, w_gla_o, w_diff_o, w_out, gn, grp)


def _ffn_kernel(tf, x_ref, sc_ref, sh_ref, g2_ref, gain_ref, wg_ref, wu_ref, wd_ref, o_ref, h_scr):
    x = x_ref[...]
    ms = jnp.mean(x * x, axis=-1, keepdims=True)
    scale = gain_ref[...] * (1.0 + sc_ref[0])
    h_scr[...] = (x * lax.rsqrt(ms + EPS) * scale + sh_ref[0]).astype(BF16)
    acc = None
    for f in range(D_FF // tf):
        cols = slice(f * tf, (f + 1) * tf)
        gate = _dot(h_scr[...], wg_ref[:, cols])
        up = _dot(h_scr[...], wu_ref[:, cols])
        a = (gate * jax.nn.sigmoid(gate) * up).astype(BF16)
        part = _dot(a, wd_ref[cols, :])
        acc = part if acc is None else acc + part
    o_ref[...] = x_ref[...] + g2_ref[0] * acc


def _ffn(x, sc, sh, g2, gain, w_gate, w_up, w_down, tokens_per_cond, tm):
    m = x.shape[0]
    tf = FFN_TILE_F
    per = tokens_per_cond // tm
    ncond = sc.shape[0]
    cidx = (lambda i: (i // per, 0, 0)) if ncond > 1 else (lambda i: (0, 0, 0))
    const = lambda i: (0, 0)
    once = pl.Buffered(1)
    return pl.pallas_call(
        functools.partial(_ffn_kernel, tf),
        grid=(m // tm,),
        in_specs=[
            pl.BlockSpec((tm, D_MODEL), lambda i: (i, 0)),
            pl.BlockSpec((1, 1, D_MODEL), cidx),
            pl.BlockSpec((1, 1, D_MODEL), cidx),
            pl.BlockSpec((1, 1, D_MODEL), cidx),
            pl.BlockSpec((1, D_MODEL), const),
            pl.BlockSpec((D_MODEL, D_FF), const, pipeline_mode=once),
            pl.BlockSpec((D_MODEL, D_FF), const, pipeline_mode=once),
            pl.BlockSpec((D_FF, D_MODEL), const, pipeline_mode=once),
        ],
        out_specs=pl.BlockSpec((tm, D_MODEL), lambda i: (i, 0)),
        out_shape=jax.ShapeDtypeStruct((m, D_MODEL), F32),
        scratch_shapes=[pltpu.VMEM((tm, D_MODEL), BF16)],
        compiler_params=_params(("arbitrary",)),
        name="ffn",
    )(x, sc, sh, g2, gain, w_gate, w_up, w_down)


def _group_mean_matrix(width, group):
    idx = np.arange(width) // group
    return (idx[:, None] == idx[None, :]).astype(np.float32) / group


def _rotate_half_matrix():
    r = np.zeros((DIFF_HEAD_DIM, DIFF_HEAD_DIM), np.float32)
    for axis in range(2):
        base = axis * ROPE_AXIS_DIM
        for f in range(ROPE_FREQS):
            r[base + ROPE_FREQS + f, base + f] = -1.0
            r[base + f, base + ROPE_FREQS + f] = 1.0
    return np.kron(np.eye(DIFF_QK // DIFF_HEAD_DIM), r).astype(np.float32)


def _rotate_half_perm():
    p = np.zeros((DIFF_HEAD_DIM,), np.int32)
    for axis in range(2):
        base = axis * ROPE_AXIS_DIM
        for f in range(ROPE_FREQS):
            p[base + f] = base + ROPE_FREQS + f
            p[base + ROPE_FREQS + f] = base + f
    return p


def _rope_tables(n_tokens):
    rows = n_tokens // GRID_W
    row = jnp.repeat(jnp.arange(rows), GRID_W).astype(F32)
    col = jnp.tile(jnp.arange(GRID_W), rows).astype(F32)
    inv = ROPE_BASE ** (-jnp.arange(ROPE_FREQS, dtype=F32) * 2.0 / ROPE_AXIS_DIM)
    ang_r = row[:, None] * inv
    ang_c = col[:, None] * inv
    ang = jnp.concatenate([ang_r, ang_r, ang_c, ang_c], axis=-1)
    reps = DIFF_QK // DIFF_HEAD_DIM
    return jnp.tile(jnp.cos(ang), (1, reps)), jnp.tile(jnp.sin(ang), (1, reps))


def _pack_state(s):
    return jnp.transpose(s, (0, 1, 4, 2, 3)).reshape(s.shape[0], 2, GLA_DV, GLA_K)


def _unpack_state(sp):
    s = sp.reshape(sp.shape[0], 2, GLA_DV, GLA_HEADS, GLA_DK)
    return jnp.transpose(s, (0, 1, 3, 4, 2))


def _run_path(x, mods, layers, consts, nb, t, ctx):
    m = nb * t
    tm, tq, gla_bb, dft_rows = _choose_tiles(nb, t)
    x = x.reshape(m, D_MODEL)
    is_latent = ctx is not None
    cos, sin = consts['rope'] if is_latent else (consts['ones'], consts['ones'])
    ct = consts['ct'][t]
    k_out, v_out, s_out = [], [], []
    for l, p in enumerate(layers):
        sh1, sc1, g1, sh2, sc2, g2 = [mods[l][:, None, j * D_MODEL:(j + 1) * D_MODEL] for j in range(6)]
        outs = _in_projection(x, sc1, sh1, p['norm1'], p['w_main'], p['w_a'], p['w2'], p['b2'],
                              p['qk_gains'], consts['grp64'], consts['rot'], cos, sin,
                              nb, t, is_latent, not is_latent, tm)
        proj, g, qn, kn = outs[:4]
        proj3 = proj.reshape(nb, t, PROJ_W)
        lam_init = 0.8 - 0.6 * math.exp(-0.3 * l)
        segments = [(kn, 0, proj3, COL_VD // DIFF_V_HEAD, t)]
        if is_latent:
            ck = ctx[1][:, l].reshape(nb, -1, DIFF_QK).astype(BF16)
            cv = ctx[2][:, l].reshape(nb, -1, DIFF_V).astype(BF16)
            segments.append((ck, 0, cv, 0, ck.shape[1]))
        od = _diff_attention(qn.reshape(nb, t, DIFF_QK), segments, p['diff_lambda'], p['diff_norm'],
                             lam_init, nb, t, tq)

        if is_latent:
            s0 = _pack_state(ctx[0][:, l])
        else:
            s0 = jnp.zeros((nb, 2, GLA_DV, GLA_K), F32)
        o_gla, s_fin = _gla(proj3, g.reshape(nb, t, 2 * GLA_K), s0, consts['gla'], nb, t, gla_bb)

        if t == FFT_N * FFT_N:
            fm = _fourier_fft(proj3, consts['fft'], nb)
        else:
            fm = _fourier(proj3, consts['bcs'], ct, nb, t, dft_rows)

        x = _merge(x, g1, proj, fm.reshape(m, FOU_W), o_gla.reshape(2, m, GLA_V), od.reshape(m, DIFF_V),
                   p['w_fou'], p['w_gla_o'], p['w_diff_o'], p['w_out'], p['gla_norm'], consts['grp128'],
                   t, tm)
        x = _ffn(x, sc2, sh2, g2, p['norm2'], p['w_ff_gate'], p['w_ff_up'], p['w_ff_down'], t, tm)

        if not is_latent:
            k_out.append(outs[4].reshape(nb, t, DIFF_HEADS, 2, DIFF_HEAD_DIM))
            v_out.append(proj3[:, :, COL_VD:COL_VD + DIFF_V].astype(F32).reshape(nb, t, DIFF_HEADS, DIFF_V_HEAD))
            s_out.append(_unpack_state(s_fin))
    return x.reshape(nb, t, D_MODEL), k_out, v_out, s_out


def kernel(x_prompt, x_sample, c, cache_diff_k, cache_diff_v, state_gla, c_ctx, w_mod, b_mod, norm1, norm2, w_in, w_gla_a2, b_gla_a, gla_norm, diff_qk_norm, diff_lambda, diff_norm, w_fou, w_gla_o, w_diff_o, w_out, w_ff_gate, w_ff_up, w_ff_down):
    nb_ctx, t_ctx, _ = x_prompt.shape
    nb_lat, t_lat, _ = x_sample.shape

    n_cond = 1 + nb_lat
    r_pad = -(-n_cond // 8) * 8
    cond = jnp.zeros((r_pad, D_MODEL), F32).at[0].set(c_ctx).at[1:n_cond].set(c)
    mods = _modulation(cond, w_mod, b_mod)
    mods_ctx = [mods[l, 0:1] for l in range(DEPTH)]
    mods_lat = [mods[l, 1:n_cond] for l in range(DEPTH)]

    a0 = 2048
    a1 = a0 + 2 * GATE_RANK
    pieces = {
        'u_f': (0, 512), 'q_g': (512, 768), 'k_g': (768, 1024), 'v_g': (1024, 1536), 'r_g': (1536, 2048),
        'q_d': (a1, a1 + 512), 'k_d': (a1 + 512, a1 + 1024), 'v_d': (a1 + 1024, a1 + 1536),
        'gates': (a1 + 1536, a1 + 1536 + 3072),
    }
    order = ['gates', 'u_f', 'v_g', 'r_g', 'v_d', 'q_g', 'k_g', 'q_d', 'k_d']
    w_main = jnp.concatenate([w_in[:, :, pieces[n][0]:pieces[n][1]] for n in order], axis=-1).astype(BF16)
    w_a = jnp.pad(w_in[:, :, a0:a1], ((0, 0), (0, 0), (0, A_PAD - 2 * GATE_RANK))).astype(BF16)
    w2 = jnp.zeros((DEPTH, A_PAD, 2 * GLA_K), F32)
    w2 = w2.at[:, 0:GATE_RANK, 0:GLA_K].set(w_gla_a2[:, 0])
    w2 = w2.at[:, GATE_RANK:2 * GATE_RANK, GLA_K:].set(w_gla_a2[:, 1])
    b2 = b_gla_a.reshape(DEPTH, 1, 2 * GLA_K)

    perm = _rotate_half_perm()
    reps = DIFF_QK // DIFF_HEAD_DIM
    layers = []
    for l in range(DEPTH):
        gq = diff_qk_norm[l, 0]
        gk = diff_qk_norm[l, 1]
        layers.append({
            'norm1': norm1[l][None, :], 'norm2': norm2[l][None, :],
            'w_main': w_main[l], 'w_a': w_a[l], 'w2': w2[l], 'b2': b2[l],
            'qk_gains': tuple(jnp.tile(v, reps)[None, :] for v in (gq, gk, gq[perm], gk[perm])),
            'diff_lambda': diff_lambda[l], 'diff_norm': diff_norm[l][None, :],
            'gla_norm': jnp.tile(gla_norm[l], GLA_HEADS)[None, :],
            'w_fou': w_fou[l].astype(BF16), 'w_gla_o': w_gla_o[l].astype(BF16),
            'w_diff_o': w_diff_o[l].astype(BF16), 'w_out': w_out[l].astype(BF16),
            'w_ff_gate': w_ff_gate[l].astype(BF16), 'w_ff_up': w_ff_up[l].astype(BF16),
            'w_ff_down': w_ff_down[l].astype(BF16),
        })

    mst, hmy, xq, pm, vbd = _gla_tables()
    consts = {
        'grp64': jnp.asarray(_group_mean_matrix(DIFF_QK, DIFF_HEAD_DIM), BF16),
        'grp128': jnp.asarray(_group_mean_matrix(GLA_V, GLA_DV), BF16),
        'rot': jnp.asarray(_rotate_half_matrix(), BF16),
        'rope': _rope_tables(t_lat),
        'ones': jnp.ones((t_ctx, DIFF_QK), F32),
        'bcs': jnp.asarray(_channel_dft_table(), BF16),
        'ct': {tt: (None if tt == FFT_N * FFT_N else _dft_tables(tt)) for tt in {t_ctx, t_lat}},
        'fft': _fft_tables(),
        'gla': (jnp.asarray(mst, BF16), jnp.asarray(hmy, BF16), jnp.asarray(xq, BF16), jnp.asarray(pm),
                jnp.asarray(vbd, BF16)),
    }

    y_prompt, k_list, v_list, s_list = _run_path(x_prompt, mods_ctx, layers, consts, nb_ctx, t_ctx, None)
    y_sample, _, _, _ = _run_path(x_sample, mods_lat, layers, consts, nb_lat, t_lat,
                                  (state_gla, cache_diff_k, cache_diff_v))
    return (y_prompt, y_sample, jnp.stack(k_list, axis=1), jnp.stack(v_list, axis=1),
            jnp.stack(s_list, axis=1))
```

```python
import functools
import math

import numpy as np
import jax
import jax.numpy as jnp
from jax import lax
from jax.experimental import pallas as pl
from jax.experimental.pallas import tpu as pltpu

F32 = jnp.float32
BF16 = jnp.bfloat16

D_MODEL = 1024
DEPTH = 4
GRID_W = 64
FOU_GROUPS = 4
FOU_GROUP_W = 128
FOU_W = 512
GLA_HEADS = 4
GLA_DK = 64
GLA_DV = 128
GLA_K = 256
GLA_V = 512
GATE_RANK = 16
GATE_TEMP = 16.0
DIFF_HEADS = 4
DIFF_HEAD_DIM = 64
DIFF_V_HEAD = 128
DIFF_QK = 512
DIFF_V = 512
ROPE_AXIS_DIM = 32
ROPE_FREQS = 16
ROPE_BASE = 10000.0
N_BRANCH = 3
D_FF = 2816
EPS = 1e-6

COL_GATES = 0
COL_UF = 3072
COL_VG = 3584
COL_RG = 4096
COL_VD = 4608
COL_QG = 5120
COL_KG = 5376
PROJ_W = 5632
W_COLS = PROJ_W + 2 * DIFF_QK
A_PAD = 128

SUBLANES = 8
VMEM_LIMIT = 48 * 1024 * 1024
VMEM_LIMIT_BIG = 56 * 1024 * 1024

GLA_C = 64
GLA_LEVELS = (32, 16, 8, 4, 2, 1)
GLA_COARSE = tuple(s for s in GLA_LEVELS if 2 * s >= SUBLANES)
GLA_SHARED = -(-(1 + sum(GLA_C // (2 * s) for s in GLA_COARSE)) // SUBLANES) * SUBLANES
GLA_MROWS = GLA_C * (1 + len(GLA_LEVELS) - len(GLA_COARSE)) + GLA_SHARED
GLA_BB = 8
ROW_TILE = 512
ATTN_TQ = 256
PROJ_TILE_N = 512
FFN_TILE_F = 256
MOD_TILE_N = 768
DFT_ROWS = 256
LOG2_E = math.log2(math.e)


def _choose_tiles(nb, t):
    tm = ROW_TILE if t % ROW_TILE == 0 else t
    tq = ATTN_TQ if t > ATTN_TQ else t // 2
    return tm, tq, math.gcd(nb, GLA_BB), min(t, DFT_ROWS)


def _params(sem, vmem=VMEM_LIMIT):
    return pltpu.CompilerParams(dimension_semantics=sem, vmem_limit_bytes=vmem)


def _dot(a, b):
    return jnp.dot(a, b, preferred_element_type=F32)


def _dot_nt(a, b):
    return lax.dot_general(a, b, (((1,), (1,)), ((), ())), preferred_element_type=F32)


def _split2(x):
    x1 = x.astype(BF16)
    return x1, (x - x1.astype(F32)).astype(BF16)


def _split3(x):
    x1 = x.astype(BF16)
    r1 = x - x1.astype(F32)
    x2 = r1.astype(BF16)
    x3 = (r1 - x2.astype(F32)).astype(BF16)
    return x1, x2, x3


def _mod_kernel(c_ref, w_ref, b_ref, o_ref):
    c = c_ref[...]
    s = c * jax.nn.sigmoid(c)
    s1, s2, s3 = _split3(s)
    w = w_ref[0]
    w1, w2, w3 = _split3(w)
    acc = _dot(s1, w1) + (_dot(s1, w2) + _dot(s2, w1)) + (_dot(s2, w2) + _dot(s1, w3) + _dot(s3, w1))
    o_ref[0] = acc + b_ref[0]


def _modulation(cond, w_mod, b_mod):
    r = cond.shape[0]
    tn = MOD_TILE_N
    n = 6 * D_MODEL
    return pl.pallas_call(
        _mod_kernel,
        grid=(DEPTH, n // tn),
        in_specs=[
            pl.BlockSpec((r, D_MODEL), lambda l, j: (0, 0)),
            pl.BlockSpec((1, D_MODEL, tn), lambda l, j: (l, 0, j)),
            pl.BlockSpec((1, 1, tn), lambda l, j: (l, 0, j)),
        ],
        out_specs=pl.BlockSpec((1, r, tn), lambda l, j: (l, 0, j)),
        out_shape=jax.ShapeDtypeStruct((DEPTH, r, n), F32),
        compiler_params=_params(("arbitrary", "arbitrary")),
        name="modulation",
    )(cond, w_mod, b_mod.reshape(DEPTH, 1, n))


def _inproj_kernel(tn, use_rope, want_f32, *refs):
    (x_ref, sc_ref, sh_ref, gain_ref, w_ref, wa_ref, w2_ref, b2_ref,
     gq_ref, gk_ref, gqp_ref, gkp_ref, grp_ref, rot_ref, cos_ref, sin_ref) = refs[:16]
    proj_ref, g_ref, qn_ref, kn_ref = refs[16:20]
    kf_ref = refs[20] if want_f32 else None
    h_scr = refs[-1]

    x = x_ref[...]
    ms = jnp.mean(x * x, axis=-1, keepdims=True)
    scale = gain_ref[...] * (1.0 + sc_ref[0])
    hb = (x * lax.rsqrt(ms + EPS) * scale + sh_ref[0]).astype(BF16)
    h_scr[...] = hb

    def tile(j):
        proj_ref[:, j * tn:(j + 1) * tn] = _dot(h_scr[...], w_ref[:, j * tn:(j + 1) * tn]).astype(BF16)

    a = _dot(hb, wa_ref[...])
    y_q = _dot(h_scr[...], w_ref[:, PROJ_W:PROJ_W + DIFF_QK])
    y_k = _dot(h_scr[...], w_ref[:, PROJ_W + DIFF_QK:PROJ_W + 2 * DIFF_QK])
    tile(0)
    a1, a2 = _split2(a)
    v1, v2 = _split2(w2_ref[...])
    yb_q, yb_k = y_q.astype(BF16), y_k.astype(BF16)
    sq_q, sq_k = (y_q * y_q).astype(BF16), (y_k * y_k).astype(BF16)
    z = _dot(a1, v1) + (_dot(a1, v2) + _dot(a2, v1)) + b2_ref[...]
    ms_q = _dot(sq_q, grp_ref[...])
    ms_k = _dot(sq_k, grp_ref[...])
    if use_rope:
        rot_q = _dot(yb_q, rot_ref[...])
        rot_k = _dot(yb_k, rot_ref[...])
    tile(1)
    tile(2)
    logsig = jnp.minimum(z, 0.0) - jnp.log1p(jnp.exp(-jnp.abs(z)))
    g_ref[...] = logsig * (1.0 / GATE_TEMP)

    def normed(y, msq, rot, g_ref_, gp_ref_, out_scale):
        r = lax.rsqrt(msq + EPS)
        out = y * r * g_ref_[...]
        if use_rope:
            out = out * cos_ref[...] + rot * r * gp_ref_[...] * sin_ref[...]
        return out * out_scale

    tile(3)
    qn_ref[...] = normed(y_q, ms_q, rot_q if use_rope else None, gq_ref, gqp_ref,
                         DIFF_HEAD_DIM ** -0.5 * LOG2_E).astype(BF16)
    tile(4)
    kn = normed(y_k, ms_k, rot_k if use_rope else None, gk_ref, gkp_ref, 1.0)
    kn_ref[0] = kn.astype(BF16)
    if want_f32:
        kf_ref[...] = kn
    for j in range(5, PROJ_W // tn):
        tile(j)


def _in_projection(x, sc, sh, gain, w_main, w_a, w2, b2, qk_gains, grp, rot, cos, sin,
                   nb, t, use_rope, want_f32, tm):
    m = x.shape[0]
    tn = PROJ_TILE_N
    per = t // tm
    ncond = sc.shape[0]
    cidx = (lambda i: (i // per, 0, 0)) if ncond > 1 else (lambda i: (0, 0, 0))
    const = lambda i: (0, 0)
    once = pl.Buffered(1)
    vec = pl.BlockSpec((1, DIFF_QK), const)
    mat = pl.BlockSpec((DIFF_QK, DIFF_QK), const, pipeline_mode=once)
    tab = pl.BlockSpec((tm, DIFF_QK), lambda i: (i % per, 0))
    out_specs = [
        pl.BlockSpec((tm, PROJ_W), lambda i: (i, 0)),
        pl.BlockSpec((tm, 2 * GLA_K), lambda i: (i, 0)),
        pl.BlockSpec((tm, DIFF_QK), lambda i: (i, 0)),
        pl.BlockSpec((1, tm, DIFF_QK), lambda i: (i // per, i % per, 0)),
    ]
    out_shape = [
        jax.ShapeDtypeStruct((m, PROJ_W), BF16),
        jax.ShapeDtypeStruct((m, 2 * GLA_K), F32),
        jax.ShapeDtypeStruct((m, DIFF_QK), BF16),
        jax.ShapeDtypeStruct((nb, t, DIFF_QK), BF16),
    ]
    if want_f32:
        out_specs.append(pl.BlockSpec((tm, DIFF_QK), lambda i: (i, 0)))
        out_shape.append(jax.ShapeDtypeStruct((m, DIFF_QK), F32))
    gq, gk, gqp, gkp = qk_gains
    return pl.pallas_call(
        functools.partial(_inproj_kernel, tn, use_rope, want_f32),
        grid=(m // tm,),
        in_specs=[
            pl.BlockSpec((tm, D_MODEL), lambda i: (i, 0)),
            pl.BlockSpec((1, 1, D_MODEL), cidx),
            pl.BlockSpec((1, 1, D_MODEL), cidx),
            pl.BlockSpec((1, D_MODEL), const),
            pl.BlockSpec((D_MODEL, W_COLS), const, pipeline_mode=once),
            pl.BlockSpec((D_MODEL, A_PAD), const, pipeline_mode=once),
            pl.BlockSpec((A_PAD, 2 * GLA_K), const, pipeline_mode=once),
            pl.BlockSpec((1, 2 * GLA_K), const),
            vec, vec, vec, vec, mat, mat, tab, tab,
        ],
        out_specs=out_specs,
        out_shape=out_shape,
        scratch_shapes=[pltpu.VMEM((tm, D_MODEL), BF16)],
        compiler_params=_params(("arbitrary",), VMEM_LIMIT_BIG),
        name="in_projection",
    )(x, sc, sh, gain, w_main, w_a, w2, b2, gq, gk, gqp, gkp, grp, rot, cos, sin)


def _diffattn_kernel(seg_len, lam_init, tq, *refs):
    n_seg = len(seg_len)
    q_ref = refs[0]
    kv_refs = refs[1:1 + 2 * n_seg]
    lam_ref, gain_ref, o_ref = refs[1 + 2 * n_seg:4 + 2 * n_seg]
    kall, vt, s_a, s_b, m_a, m_b = refs[4 + 2 * n_seg:]
    nq = q_ref.shape[1] // tq

    off = 0
    for s, tk in enumerate(seg_len):
        kall[off:off + tk, :] = kv_refs[2 * s][0]
        vt[0:DIFF_V_HEAD, off:off + tk] = kv_refs[2 * s + 1][0].astype(F32).T.astype(BF16)
        off += tk
    pad = vt.shape[0] - DIFF_V_HEAD
    first = lax.broadcasted_iota(jnp.int32, (pad, vt.shape[1]), 0) == 0
    vt[DIFF_V_HEAD:, :] = jnp.where(first, 1.0, 0.0).astype(BF16)

    lp = lam_ref[...]
    lam = (jnp.exp(jnp.sum(lp[0:1] * lp[1:2], axis=-1, keepdims=True))
           - jnp.exp(jnp.sum(lp[2:3] * lp[3:4], axis=-1, keepdims=True)) + lam_init)

    def rows(tile):
        return pl.ds(pl.multiple_of(tile * tq, tq), tq)

    def score(tile, s_write, m_write):
        q = q_ref[0, rows(tile), :]
        lane = lax.broadcasted_iota(jnp.int32, q.shape, 1)
        zero = jnp.zeros_like(q)
        q2 = jnp.concatenate([jnp.where(lane < DIFF_HEAD_DIM, q, zero),
                              jnp.where(lane >= DIFF_HEAD_DIM, q, zero)], axis=0)
        sc = _dot_nt(kall[...], q2)
        s_write[...] = sc
        m_write[...] = jnp.broadcast_to(sc.max(axis=0, keepdims=True), m_write.shape)

    def finish(tile, s_read, m_read):
        p = jnp.exp2(s_read[...] - m_read[0:1, :]).astype(BF16)
        acc = _dot(vt[...], p)
        o2 = acc[0:DIFF_V_HEAD] / acc[DIFF_V_HEAD:DIFF_V_HEAD + 1]
        o = (o2[:, 0:tq] - lam * o2[:, tq:]).T
        ms = jnp.mean(o * o, axis=-1, keepdims=True)
        o = o * lax.rsqrt(ms + EPS) * gain_ref[...] * (1.0 - lam_init)
        o_ref[0, rows(tile), :] = o.astype(BF16)

    score(0, s_a, m_a)
    n_pairs = (nq - 1) // 2

    def pair(j, carry):
        score(2 * j + 1, s_b, m_b)
        finish(2 * j, s_a, m_a)
        score(2 * j + 2, s_a, m_a)
        finish(2 * j + 1, s_b, m_b)
        return carry

    lax.fori_loop(0, n_pairs, pair, 0)
    done = 2 * n_pairs
    if (nq - 1) % 2 == 1:
        score(done + 1, s_b, m_b)
        finish(done, s_a, m_a)
        finish(done + 1, s_b, m_b)
    else:
        finish(done, s_a, m_a)


def _diff_attention(qn, segments, lam_p, gain, lam_init, nb, tq_total, tq):
    hd = 2 * DIFF_HEAD_DIM
    in_specs = [pl.BlockSpec((1, tq_total, hd), lambda b, h: (b, 0, h))]
    args = [qn]
    seg_len = tuple(seg[4] for seg in segments)
    tk_all = sum(seg_len)
    for (ka, kc, va, vc, tk) in segments:
        in_specs.append(pl.BlockSpec((1, tk, hd), lambda b, h, kc=kc: (b, 0, kc + h)))
        in_specs.append(pl.BlockSpec((1, tk, DIFF_V_HEAD), lambda b, h, vc=vc: (b, 0, vc + h)))
        args += [ka, va]
    in_specs.append(pl.BlockSpec((4, DIFF_HEAD_DIM), lambda b, h: (0, 0)))
    in_specs.append(pl.BlockSpec((1, DIFF_V_HEAD), lambda b, h: (0, 0)))
    args += [lam_p, gain]
    scratch = [
        pltpu.VMEM((tk_all, hd), BF16),
        pltpu.VMEM((DIFF_V_HEAD + 2 * SUBLANES, tk_all), BF16),
        pltpu.VMEM((tk_all, 2 * tq), F32), pltpu.VMEM((tk_all, 2 * tq), F32),
        pltpu.VMEM((SUBLANES, 2 * tq), F32), pltpu.VMEM((SUBLANES, 2 * tq), F32),
    ]
    return pl.pallas_call(
        functools.partial(_diffattn_kernel, seg_len, lam_init, tq),
        grid=(nb, DIFF_HEADS),
        in_specs=in_specs,
        out_specs=pl.BlockSpec((1, tq_total, DIFF_V_HEAD), lambda b, h: (b, 0, h)),
        out_shape=jax.ShapeDtypeStruct((nb, tq_total, DIFF_V), BF16),
        scratch_shapes=scratch,
        compiler_params=_params(("arbitrary", "arbitrary")),
        name="diff_attention",
    )(*args)


def _gla_tables():
    c = GLA_C
    nl = len(GLA_LEVELS)
    mst = np.zeros((2, GLA_MROWS, c), np.float32)
    xm = np.zeros((2, nl + 1, c), np.float32)
    pm = np.zeros((2, nl + 1, c, c), np.float32)
    idx = np.arange(c)
    for d in range(2):
        ip = idx if d == 0 else c - 1 - idx
        mst[d, 0:c] = (ip[None, :] <= ip[:, None])
        mst[d, c] = 1.0
        shared, full = c + 1, c + GLA_SHARED
        pm[d, 0] = np.eye(c)
        xm[d, 0] = 1.0
        for li, s in enumerate(GLA_LEVELS):
            ref = (ip // (2 * s)) * (2 * s) + s - 1
            if s in GLA_COARSE:
                for first in range(0, c, 2 * s):
                    mst[d, shared] = (ip <= ref[first])
                    shared += 1
            else:
                mst[d, full:full + c] = (ip[None, :] <= ref[:, None])
                full += c
            odd = (ip // s) % 2 == 1
            xm[d, li + 1] = odd
            same = (ip[:, None] // (2 * s)) == (ip[None, :] // (2 * s))
            pm[d, li + 1] = odd[:, None] & (~odd)[None, :] & same
    pm = np.tile(pm, (1, 1, 1, GLA_HEADS))
    rows = np.arange(GLA_HEADS * c)[:, None] // c
    hm = (rows == (np.arange(GLA_K)[None, :] // GLA_DK)).astype(np.float32)
    ykeep = np.concatenate([np.ones((2, 1, c), np.float32), 1.0 - xm[:, 1:]], axis=1)
    hmy = hm[None, None] * np.tile(ykeep, (1, 1, GLA_HEADS))[:, :, :, None]
    xq = np.broadcast_to(xm[:, 1:, :, None], (2, nl, c, GLA_K)).copy()
    vbd = (rows == (np.arange(GLA_V)[None, :] // GLA_DV)).astype(np.float32)
    return mst, hmy, xq, pm, vbd


def _gla_kernel(bb, q_ref, k_ref, v_ref, g_ref, mst_ref, hmy_ref, xq_ref, pm_ref, vbd_ref, s0_ref,
                o_ref, sfin_ref, s_scr):
    c_idx = pl.program_id(2)

    @pl.when(c_idx == 0)
    def _():
        s_scr[...] = s0_ref[:, 0]

    c = GLA_C
    nl = len(GLA_LEVELS)
    m = mst_ref[0]
    seqs = range(bb)

    def stack_keys(yb, li):
        return jnp.concatenate([yb] * GLA_HEADS, axis=0) * hmy_ref[0, li]

    r_all = []
    for b in seqs:
        g1, g2 = _split2(g_ref[b])
        r_all.append(_dot(m, g1) + _dot(m, g2))
    cum = [r[0:c] for r in r_all]
    tot = [r[c:c + 1] for r in r_all]

    def level_ref(r, li):
        s = GLA_LEVELS[li]
        if s in GLA_COARSE:
            first = c + 1 + sum(c // (2 * t) for t in GLA_COARSE if t > s)
            nblk = c // (2 * s)
            return jnp.concatenate([jnp.broadcast_to(r[first + p:first + p + 1], (2 * s, GLA_K))
                                    for p in range(nblk)], axis=0)
        start = c + GLA_SHARED + c * (li - len(GLA_COARSE))
        return r[start:start + c]
    q = [q_ref[b].astype(F32) * (GLA_DK ** -0.5) for b in seqs]
    k = [k_ref[b].astype(F32) for b in seqs]

    o_inter = []
    for b in seqs:
        state = s_scr[b]
        q_dec = stack_keys((q[b] * jnp.exp(cum[b])).astype(BF16), 0)
        o_rows = _dot_nt(q_dec, state.astype(BF16))
        o_inter.append(jnp.concatenate([o_rows[h * c:(h + 1) * c] for h in range(GLA_HEADS)], axis=1))
        k_dec = stack_keys((k[b] * jnp.exp(tot[b] - cum[b])).astype(BF16), 0)
        vf = v_ref[b].astype(F32)
        v_t = jnp.concatenate(
            [jnp.concatenate([vf[:, (2 * p) * GLA_DV:(2 * p + 1) * GLA_DV],
                              vf[:, (2 * p + 1) * GLA_DV:(2 * p + 2) * GLA_DV]], axis=0).T
             for p in range(GLA_HEADS // 2)], axis=1).astype(BF16)
        s_scr[b] = state * jnp.exp(tot[b]) + _dot(v_t, k_dec)

    @pl.when(c_idx == pl.num_programs(2) - 1)
    def _():
        sfin_ref[:, 0] = s_scr[...]

    att = [pm_ref[0, 0] * _dot_nt(q[b].astype(BF16), stack_keys(k_ref[b], 0)) for b in seqs]
    for li in range(nl):
        for b in seqs:
            ref = level_ref(r_all[b], li)
            e = jnp.exp(-jnp.abs(cum[b] - ref))
            xs = (q[b] * e).astype(BF16) * xq_ref[0, li]
            ys = stack_keys((k[b] * e).astype(BF16), li + 1)
            att[b] = att[b] + pm_ref[0, li + 1] * _dot_nt(xs, ys)
    for b in seqs:
        v_bd = jnp.concatenate([v_ref[b]] * GLA_HEADS, axis=0) * vbd_ref[...]
        o_ref[0, b] = (o_inter[b] + _dot(att[b].astype(BF16), v_bd)).astype(BF16)


def _gla(proj3, g3, s0, tables, nb, t, bb):
    mst, hmy, xq, pm, vbd = tables
    nc = t // GLA_C
    nl = len(GLA_LEVELS)

    def chunk(d, c):
        return jnp.where(d == 0, c, nc - 1 - c)

    return pl.pallas_call(
        functools.partial(_gla_kernel, bb),
        grid=(nb // bb, 2, nc),
        in_specs=[
            pl.BlockSpec((bb, GLA_C, GLA_K), lambda b, d, c: (b, chunk(d, c), COL_QG // GLA_K)),
            pl.BlockSpec((bb, GLA_C, GLA_K), lambda b, d, c: (b, chunk(d, c), COL_KG // GLA_K)),
            pl.BlockSpec((bb, GLA_C, GLA_V), lambda b, d, c: (b, chunk(d, c), COL_VG // GLA_V)),
            pl.BlockSpec((bb, GLA_C, GLA_K), lambda b, d, c: (b, chunk(d, c), d)),
            pl.BlockSpec((1, GLA_MROWS, GLA_C), lambda b, d, c: (d, 0, 0)),
            pl.BlockSpec((1, nl + 1, GLA_HEADS * GLA_C, GLA_K), lambda b, d, c: (d, 0, 0, 0)),
            pl.BlockSpec((1, nl, GLA_C, GLA_K), lambda b, d, c: (d, 0, 0, 0)),
            pl.BlockSpec((1, nl + 1, GLA_C, GLA_HEADS * GLA_C), lambda b, d, c: (d, 0, 0, 0)),
            pl.BlockSpec((GLA_HEADS * GLA_C, GLA_V), lambda b, d, c: (0, 0)),
            pl.BlockSpec((bb, 1, GLA_DV, GLA_K), lambda b, d, c: (b, d, 0, 0)),
        ],
        out_specs=[
            pl.BlockSpec((1, bb, GLA_C, GLA_V), lambda b, d, c: (d, b, chunk(d, c), 0)),
            pl.BlockSpec((bb, 1, GLA_DV, GLA_K), lambda b, d, c: (b, d, 0, 0)),
        ],
        out_shape=[
            jax.ShapeDtypeStruct((2, nb, t, GLA_V), BF16),
            jax.ShapeDtypeStruct((nb, 2, GLA_DV, GLA_K), F32),
        ],
        scratch_shapes=[pltpu.VMEM((bb, GLA_DV, GLA_K), F32)],
        compiler_params=_params(("arbitrary", "arbitrary", "arbitrary")),
        name="gla",
    )(proj3, proj3, proj3, g3, mst, hmy, xq, pm, vbd, s0)


def _fourier_kernel(t, u_ref, bcs_ref, ct_ref, o_ref, y_scr):
    i = pl.program_id(1)

    @pl.when(i == 0)
    def _():
        rows = min(t, ROW_TILE)
        for r in range(t // rows):
            y = _dot(u_ref[0, r * rows:(r + 1) * rows, :], bcs_ref[...])
            y_scr[r * rows:(r + 1) * rows, :] = y[:, :FOU_W].astype(BF16)
            y_scr[t + r * rows:t + (r + 1) * rows, :] = y[:, FOU_W:].astype(BF16)

    o_ref[0] = _dot(ct_ref[...], y_scr[...]).astype(BF16)


def _fourier(proj3, bcs, ct, nb, t, tm):
    return pl.pallas_call(
        functools.partial(_fourier_kernel, t),
        grid=(nb, t // tm),
        in_specs=[
            pl.BlockSpec((1, t, FOU_W), lambda b, i: (b, 0, COL_UF // FOU_W)),
            pl.BlockSpec((FOU_W, 2 * FOU_W), lambda b, i: (0, 0)),
            pl.BlockSpec((tm, 2 * t), lambda b, i: (i, 0)),
        ],
        out_specs=pl.BlockSpec((1, tm, FOU_W), lambda b, i: (b, i, 0)),
        out_shape=jax.ShapeDtypeStruct((nb, t, FOU_W), BF16),
        scratch_shapes=[pltpu.VMEM((2 * t, FOU_W), BF16)],
        compiler_params=_params(("arbitrary", "arbitrary")),
        name="fourier_mix",
    )(proj3, bcs, ct)


FFT_N = 64
FFT_G = 8
FFT_CW = 256


def _fft_kernel(u_ref, bcs_ref, m1_ref, q2_ref, o_ref, z_scr, a_scr):
    n, gsz, cw = FFT_N, FFT_G, FFT_CW
    rows = n * gsz
    for r in range(n // gsz):
        y = _dot(u_ref[0, r * rows:(r + 1) * rows, :], bcs_ref[...])
        z_scr[gsz * r:gsz * (r + 1)] = y.reshape(gsz, n, 2 * cw)
    for g in range(n // gsz):
        zg = z_scr[:, gsz * g:gsz * (g + 1), :].reshape(rows, 2 * cw)
        rhs = jnp.concatenate([zg[:, :cw], zg[:, cw:]], axis=0).astype(BF16)
        a_scr[g] = _dot(m1_ref[...], rhs).astype(BF16)
    for f in range(0, n // gsz, 2):
        res = []
        for ff in (f, f + 1):
            slabs = [a_scr[g, part * rows + n * ff:part * rows + n * (ff + 1), :]
                     for part in range(2) for g in range(n // gsz)]
            res.append(_dot(q2_ref[ff], jnp.concatenate(slabs, axis=0)).reshape(n, gsz, cw))
        o_ref[0, :, gsz * f:gsz * (f + 2), :] = jnp.concatenate(res, axis=1).astype(BF16)


def _fft_tables():
    n, gsz, t = FFT_N, FFT_G, FFT_N * FFT_N
    k = np.arange(FOU_GROUP_W)
    ang = 2.0 * np.pi * ((k[:, None] * k[None, :]) % FOU_GROUP_W) / FOU_GROUP_W
    eye2 = np.eye(FFT_CW // FOU_GROUP_W)
    bcs = np.concatenate([np.kron(eye2, np.cos(ang)), -np.kron(eye2, np.sin(ang))], axis=1) * FOU_GROUP_W ** -0.5
    i = np.arange(n)
    a1 = 2.0 * np.pi * ((i[:, None] * i[None, :]) % n) / n
    c1 = np.kron(np.cos(a1), np.eye(gsz))
    s1 = np.kron(np.sin(a1), np.eye(gsz))
    m1 = np.block([[c1, s1], [-s1, c1]])
    ngrp = n // gsz
    shape = (ngrp, n * gsz, n * gsz)
    f = lax.broadcasted_iota(jnp.int32, shape, 0)
    row = lax.broadcasted_iota(jnp.int32, shape, 1)
    col = lax.broadcasted_iota(jnp.int32, shape, 2)
    f2, a = row // gsz, row % gsz
    g, ap, j = col // (gsz * gsz), (col // gsz) % gsz, col % gsz
    freq = gsz * f + a + n * f2
    theta = (((gsz * g + j) * freq) % t).astype(F32) * (2.0 * math.pi / t)
    keep = (a == ap)
    scale = t ** -0.5
    qc = jnp.where(keep, jnp.cos(theta) * scale, 0.0)
    qs = jnp.where(keep, jnp.sin(theta) * scale, 0.0)
    q2 = jnp.concatenate([qc, qs], axis=2).astype(BF16)
    return jnp.asarray(bcs, BF16), jnp.asarray(m1, BF16), q2


def _fourier_fft(proj3, tables, nb):
    bcs, m1, q2 = tables
    n, gsz, cw = FFT_N, FFT_G, FFT_CW
    t = n * n
    once = pl.Buffered(1)
    out = pl.pallas_call(
        _fft_kernel,
        grid=(nb, FOU_W // cw),
        in_specs=[
            pl.BlockSpec((1, t, cw), lambda b, h: (b, 0, COL_UF // cw + h)),
            pl.BlockSpec((cw, 2 * cw), lambda b, h: (0, 0), pipeline_mode=once),
            pl.BlockSpec((2 * n * gsz, 2 * n * gsz), lambda b, h: (0, 0), pipeline_mode=once),
            pl.BlockSpec((n // gsz, n * gsz, 2 * n * gsz), lambda b, h: (0, 0, 0), pipeline_mode=once),
        ],
        out_specs=pl.BlockSpec((1, n, n, cw), lambda b, h: (b, 0, 0, h)),
        out_shape=jax.ShapeDtypeStruct((nb, n, n, FOU_W), BF16),
        scratch_shapes=[pltpu.VMEM((n, n, 2 * cw), F32), pltpu.VMEM((n // gsz, 2 * n * gsz, cw), BF16)],
        compiler_params=_params(("arbitrary", "arbitrary")),
        name="fourier_fft",
    )(proj3, bcs, m1, q2)
    return out.reshape(nb, t, FOU_W)


def _dft_tables(t):
    lo = min(t, 64)
    hi = t // lo
    f = jnp.arange(t, dtype=jnp.int32)[None, :]
    unit = 2.0 * math.pi / t
    a = ((lo * jnp.arange(hi, dtype=jnp.int32)[:, None] * f) % t).astype(F32) * unit
    b = ((jnp.arange(lo, dtype=jnp.int32)[:, None] * f) % t).astype(F32) * unit
    ca, sa = jnp.cos(a)[:, None, :], jnp.sin(a)[:, None, :]
    cb, sb = jnp.cos(b)[None, :, :], jnp.sin(b)[None, :, :]
    scale = t ** -0.5
    cos = ((ca * cb - sa * sb) * scale).reshape(t, t)
    msin = ((sa * cb + ca * sb) * -scale).reshape(t, t)
    return jnp.concatenate([cos, msin], axis=1).astype(BF16)


def _channel_dft_table():
    n = np.arange(FOU_GROUP_W)
    ang = 2.0 * np.pi * ((n[:, None] * n[None, :]) % FOU_GROUP_W) / FOU_GROUP_W
    eye = np.eye(FOU_GROUPS)
    bc = np.kron(eye, np.cos(ang)) * FOU_GROUP_W ** -0.5
    bs = np.kron(eye, np.sin(ang)) * FOU_GROUP_W ** -0.5
    return np.concatenate([bc, bs], axis=1).astype(np.float32)


def _merge_kernel(x_ref, g1_ref, fm_ref, of_ref, ob_ref, r_ref, od_ref, gt_ref,
                  wf_ref, wg_ref, wd_ref, wo_ref, gn_ref, grp_ref, o_ref):
    y_f = _dot(fm_ref[...].astype(BF16), wf_ref[...])
    og = of_ref[0].astype(F32) + ob_ref[0].astype(F32)
    ms = _dot((og * og).astype(BF16), grp_ref[...])
    r = r_ref[...].astype(F32)
    og = og * lax.rsqrt(ms + EPS) * gn_ref[...] * (r * jax.nn.sigmoid(r))
    y_g = _dot(og.astype(BF16), wg_ref[...])
    y_d = _dot(od_ref[...], wd_ref[...])
    gates = jax.nn.sigmoid(gt_ref[...].astype(F32))
    merged = (gates[:, 0:D_MODEL] * y_f + gates[:, D_MODEL:2 * D_MODEL] * y_g
              + gates[:, 2 * D_MODEL:3 * D_MODEL] * y_d)
    o_ref[...] = x_ref[...] + g1_ref[0] * _dot(merged.astype(BF16), wo_ref[...])


def _merge(x, g1, proj, fm, o_gla, od, w_fou, w_gla_o, w_diff_o, w_out, gn, grp, tokens_per_cond, tm):
    m = x.shape[0]
    per = tokens_per_cond // tm
    ncond = g1.shape[0]
    cidx = (lambda i: (i // per, 0, 0)) if ncond > 1 else (lambda i: (0, 0, 0))
    once = pl.Buffered(1)
    half = pl.BlockSpec((FOU_W, D_MODEL), lambda i: (0, 0), pipeline_mode=once)
    return pl.pallas_call(
        _merge_kernel,
        grid=(m // tm,),
        in_specs=[
            pl.BlockSpec((tm, D_MODEL), lambda i: (i, 0)),
            pl.BlockSpec((1, 1, D_MODEL), cidx),
            pl.BlockSpec((tm, FOU_W), lambda i: (i, 0)),
            pl.BlockSpec((1, tm, GLA_V), lambda i: (0, i, 0)),
            pl.BlockSpec((1, tm, GLA_V), lambda i: (1, i, 0)),
            pl.BlockSpec((tm, GLA_V), lambda i: (i, COL_RG // GLA_V)),
            pl.BlockSpec((tm, DIFF_V), lambda i: (i, 0)),
            pl.BlockSpec((tm, N_BRANCH * D_MODEL), lambda i: (i, COL_GATES)),
            half, half, half,
            pl.BlockSpec((D_MODEL, D_MODEL), lambda i: (0, 0), pipeline_mode=once),
            pl.BlockSpec((1, GLA_V), lambda i: (0, 0)),
            pl.BlockSpec((GLA_V, GLA_V), lambda i: (0, 0)),
        ],
        out_specs=pl.BlockSpec((tm, D_MODEL), lambda i: (i, 0)),
        out_shape=jax.ShapeDtypeStruct((m, D_MODEL), F32),
        compiler_params=_params(("arbitrary",)),
        name="merge",
    )(x, g1, fm, o_gla, o_gla, proj, od, proj, w_fou, w_gla_o, w_diff_o, w_out, gn, grp)


def _ffn_kernel(tf, x_ref, sc_ref, sh_ref, g2_ref, gain_ref, wg_ref, wu_ref, wd_ref, o_ref, h_scr):
    x = x_ref[...]
    ms = jnp.mean(x * x, axis=-1, keepdims=True)
    scale = gain_ref[...] * (1.0 + sc_ref[0])
    h_scr[...] = (x * lax.rsqrt(ms + EPS) * scale + sh_ref[0]).astype(BF16)
    acc = None
    for f in range(D_FF // tf):
        cols = slice(f * tf, (f + 1) * tf)
        gate = _dot(h_scr[...], wg_ref[:, cols])
        up = _dot(h_scr[...], wu_ref[:, cols])
        a = (gate * jax.nn.sigmoid(gate) * up).astype(BF16)
        part = _dot(a, wd_ref[cols, :])
        acc = part if acc is None else acc + part
    o_ref[...] = x_ref[...] + g2_ref[0] * acc


def _ffn(x, sc, sh, g2, gain, w_gate, w_up, w_down, tokens_per_cond, tm):
    m = x.shape[0]
    tf = FFN_TILE_F
    per = tokens_per_cond // tm
    ncond = sc.shape[0]
    cidx = (lambda i: (i // per, 0, 0)) if ncond > 1 else (lambda i: (0, 0, 0))
    const = lambda i: (0, 0)
    once = pl.Buffered(1)
    return pl.pallas_call(
        functools.partial(_ffn_kernel, tf),
        grid=(m // tm,),
        in_specs=[
            pl.BlockSpec((tm, D_MODEL), lambda i: (i, 0)),
            pl.BlockSpec((1, 1, D_MODEL), cidx),
            pl.BlockSpec((1, 1, D_MODEL), cidx),
            pl.BlockSpec((1, 1, D_MODEL), cidx),
            pl.BlockSpec((1, D_MODEL), const),
            pl.BlockSpec((D_MODEL, D_FF), const, pipeline_mode=once),
            pl.BlockSpec((D_MODEL, D_FF), const, pipeline_mode=once),
            pl.BlockSpec((D_FF, D_MODEL), const, pipeline_mode=once),
        ],
        out_specs=pl.BlockSpec((tm, D_MODEL), lambda i: (i, 0)),
        out_shape=jax.ShapeDtypeStruct((m, D_MODEL), F32),
        scratch_shapes=[pltpu.VMEM((tm, D_MODEL), BF16)],
        compiler_params=_params(("arbitrary",)),
        name="ffn",
    )(x, sc, sh, g2, gain, w_gate, w_up, w_down)


def _group_mean_matrix(width, group):
    idx = np.arange(width) // group
    return (idx[:, None] == idx[None, :]).astype(np.float32) / group


def _rotate_half_matrix():
    r = np.zeros((DIFF_HEAD_DIM, DIFF_HEAD_DIM), np.float32)
    for axis in range(2):
        base = axis * ROPE_AXIS_DIM
        for f in range(ROPE_FREQS):
            r[base + ROPE_FREQS + f, base + f] = -1.0
            r[base + f, base + ROPE_FREQS + f] = 1.0
    return np.kron(np.eye(DIFF_QK // DIFF_HEAD_DIM), r).astype(np.float32)


def _rotate_half_perm():
    p = np.zeros((DIFF_HEAD_DIM,), np.int32)
    for axis in range(2):
        base = axis * ROPE_AXIS_DIM
        for f in range(ROPE_FREQS):
            p[base + f] = base + ROPE_FREQS + f
            p[base + ROPE_FREQS + f] = base + f
    return p


def _rope_tables(n_tokens):
    rows = n_tokens // GRID_W
    row = jnp.repeat(jnp.arange(rows), GRID_W).astype(F32)
    col = jnp.tile(jnp.arange(GRID_W), rows).astype(F32)
    inv = ROPE_BASE ** (-jnp.arange(ROPE_FREQS, dtype=F32) * 2.0 / ROPE_AXIS_DIM)
    ang_r = row[:, None] * inv
    ang_c = col[:, None] * inv
    ang = jnp.concatenate([ang_r, ang_r, ang_c, ang_c], axis=-1)
    reps = DIFF_QK // DIFF_HEAD_DIM
    return jnp.tile(jnp.cos(ang), (1, reps)), jnp.tile(jnp.sin(ang), (1, reps))


def _pack_state(s):
    return jnp.transpose(s, (0, 1, 4, 2, 3)).reshape(s.shape[0], 2, GLA_DV, GLA_K)


def _unpack_state(sp):
    s = sp.reshape(sp.shape[0], 2, GLA_DV, GLA_HEADS, GLA_DK)
    return jnp.transpose(s, (0, 1, 3, 4, 2))


def _run_path(x, mods, layers, consts, nb, t, ctx):
    m = nb * t
    tm, tq, gla_bb, dft_rows = _choose_tiles(nb, t)
    x = x.reshape(m, D_MODEL)
    is_latent = ctx is not None
    cos, sin = consts['rope'] if is_latent else (consts['ones'], consts['ones'])
    ct = consts['ct'][t]
    k_out, v_out, s_out = [], [], []
    for l, p in enumerate(layers):
        sh1, sc1, g1, sh2, sc2, g2 = [mods[l][:, None, j * D_MODEL:(j + 1) * D_MODEL] for j in range(6)]
        outs = _in_projection(x, sc1, sh1, p['norm1'], p['w_main'], p['w_a'], p['w2'], p['b2'],
                              p['qk_gains'], consts['grp64'], consts['rot'], cos, sin,
                              nb, t, is_latent, not is_latent, tm)
        proj, g, qn, kn = outs[:4]
        proj3 = proj.reshape(nb, t, PROJ_W)
        lam_init = 0.8 - 0.6 * math.exp(-0.3 * l)
        segments = [(kn, 0, proj3, COL_VD // DIFF_V_HEAD, t)]
        if is_latent:
            ck = ctx[1][:, l].reshape(nb, -1, DIFF_QK).astype(BF16)
            cv = ctx[2][:, l].reshape(nb, -1, DIFF_V).astype(BF16)
            segments.append((ck, 0, cv, 0, ck.shape[1]))
        od = _diff_attention(qn.reshape(nb, t, DIFF_QK), segments, p['diff_lambda'], p['diff_norm'],
                             lam_init, nb, t, tq)

        if is_latent:
            s0 = _pack_state(ctx[0][:, l])
        else:
            s0 = jnp.zeros((nb, 2, GLA_DV, GLA_K), F32)
        o_gla, s_fin = _gla(proj3, g.reshape(nb, t, 2 * GLA_K), s0, consts['gla'], nb, t, gla_bb)

        if t == FFT_N * FFT_N:
            fm = _fourier_fft(proj3, consts['fft'], nb)
        else:
            fm = _fourier(proj3, consts['bcs'], ct, nb, t, dft_rows)

        x = _merge(x, g1, proj, fm.reshape(m, FOU_W), o_gla.reshape(2, m, GLA_V), od.reshape(m, DIFF_V),
                   p['w_fou'], p['w_gla_o'], p['w_diff_o'], p['w_out'], p['gla_norm'], consts['grp128'],
                   t, tm)
        x = _ffn(x, sc2, sh2, g2, p['norm2'], p['w_ff_gate'], p['w_ff_up'], p['w_ff_down'], t, tm)

        if not is_latent:
            k_out.append(outs[4].reshape(nb, t, DIFF_HEADS, 2, DIFF_HEAD_DIM))
            v_out.append(proj3[:, :, COL_VD:COL_VD + DIFF_V].astype(F32).reshape(nb, t, DIFF_HEADS, DIFF_V_HEAD))
            s_out.append(_unpack_state(s_fin))
    return x.reshape(nb, t, D_MODEL), k_out, v_out, s_out


def kernel(x_prompt, x_sample, c, cache_diff_k, cache_diff_v, state_gla, c_ctx, w_mod, b_mod, norm1, norm2, w_in, w_gla_a2, b_gla_a, gla_norm, diff_qk_norm, diff_lambda, diff_norm, w_fou, w_gla_o, w_diff_o, w_out, w_ff_gate, w_ff_up, w_ff_down):
    nb_ctx, t_ctx, _ = x_prompt.shape
    nb_lat, t_lat, _ = x_sample.shape

    n_cond = 1 + nb_lat
    r_pad = -(-n_cond // 8) * 8
    cond = jnp.zeros((r_pad, D_MODEL), F32).at[0].set(c_ctx).at[1:n_cond].set(c)
    mods = _modulation(cond, w_mod, b_mod)
    mods_ctx = [mods[l, 0:1] for l in range(DEPTH)]
    mods_lat = [mods[l, 1:n_cond] for l in range(DEPTH)]

    a0 = 2048
    a1 = a0 + 2 * GATE_RANK
    pieces = {
        'u_f': (0, 512), 'q_g': (512, 768), 'k_g': (768, 1024), 'v_g': (1024, 1536), 'r_g': (1536, 2048),
        'q_d': (a1, a1 + 512), 'k_d': (a1 + 512, a1 + 1024), 'v_d': (a1 + 1024, a1 + 1536),
        'gates': (a1 + 1536, a1 + 1536 + 3072),
    }
    order = ['gates', 'u_f', 'v_g', 'r_g', 'v_d', 'q_g', 'k_g', 'q_d', 'k_d']
    w_main = jnp.concatenate([w_in[:, :, pieces[n][0]:pieces[n][1]] for n in order], axis=-1).astype(BF16)
    w_a = jnp.pad(w_in[:, :, a0:a1], ((0, 0), (0, 0), (0, A_PAD - 2 * GATE_RANK))).astype(BF16)
    w2 = jnp.zeros((DEPTH, A_PAD, 2 * GLA_K), F32)
    w2 = w2.at[:, 0:GATE_RANK, 0:GLA_K].set(w_gla_a2[:, 0])
    w2 = w2.at[:, GATE_RANK:2 * GATE_RANK, GLA_K:].set(w_gla_a2[:, 1])
    b2 = b_gla_a.reshape(DEPTH, 1, 2 * GLA_K)

    perm = _rotate_half_perm()
    reps = DIFF_QK // DIFF_HEAD_DIM
    layers = []
    for l in range(DEPTH):
        gq = diff_qk_norm[l, 0]
        gk = diff_qk_norm[l, 1]
        layers.append({
            'norm1': norm1[l][None, :], 'norm2': norm2[l][None, :],
            'w_main': w_main[l], 'w_a': w_a[l], 'w2': w2[l], 'b2': b2[l],
            'qk_gains': tuple(jnp.tile(v, reps)[None, :] for v in (gq, gk, gq[perm], gk[perm])),
            'diff_lambda': diff_lambda[l], 'diff_norm': diff_norm[l][None, :],
            'gla_norm': jnp.tile(gla_norm[l], GLA_HEADS)[None, :],
            'w_fou': w_fou[l].astype(BF16), 'w_gla_o': w_gla_o[l].astype(BF16),
            'w_diff_o': w_diff_o[l].astype(BF16), 'w_out': w_out[l].astype(BF16),
            'w_ff_gate': w_ff_gate[l].astype(BF16), 'w_ff_up': w_ff_up[l].astype(BF16),
            'w_ff_down': w_ff_down[l].astype(BF16),
        })

    mst, hmy, xq, pm, vbd = _gla_tables()
    consts = {
        'grp64': jnp.asarray(_group_mean_matrix(DIFF_QK, DIFF_HEAD_DIM), BF16),
        'grp128': jnp.asarray(_group_mean_matrix(GLA_V, GLA_DV), BF16),
        'rot': jnp.asarray(_rotate_half_matrix(), BF16),
        'rope': _rope_tables(t_lat),
        'ones': jnp.ones((t_ctx, DIFF_QK), F32),
        'bcs': jnp.asarray(_channel_dft_table(), BF16),
        'ct': {tt: (None if tt == FFT_N * FFT_N else _dft_tables(tt)) for tt in {t_ctx, t_lat}},
        'fft': _fft_tables(),
        'gla': (jnp.asarray(mst, BF16), jnp.asarray(hmy, BF16), jnp.asarray(xq, BF16), jnp.asarray(pm),
                jnp.asarray(vbd, BF16)),
    }

    y_prompt, k_list, v_list, s_list = _run_path(x_prompt, mods_ctx, layers, consts, nb_ctx, t_ctx, None)
    y_sample, _, _, _ = _run_path(x_sample, mods_lat, layers, consts, nb_lat, t_lat,
                                  (state_gla, cache_diff_k, cache_diff_v))
    return (y_prompt, y_sample, jnp.stack(k_list, axis=1), jnp.stack(v_list, axis=1),
            jnp.stack(s_list, axis=1))
```

```python
import functools
import math

import numpy as np
import jax
import jax.numpy as jnp
from jax import lax
from jax.experimental import pallas as pl
from jax.experimental.pallas import tpu as pltpu

F32 = jnp.float32
BF16 = jnp.bfloat16

D_MODEL = 1024
DEPTH = 4
GRID_W = 64
FOU_GROUPS = 4
FOU_GROUP_W = 128
FOU_W = 512
GLA_HEADS = 4
GLA_DK = 64
GLA_DV = 128
GLA_K = 256
GLA_V = 512
GATE_RANK = 16
GATE_TEMP = 16.0
DIFF_HEADS = 4
DIFF_HEAD_DIM = 64
DIFF_V_HEAD = 128
DIFF_QK = 512
DIFF_V = 512
ROPE_AXIS_DIM = 32
ROPE_FREQS = 16
ROPE_BASE = 10000.0
N_BRANCH = 3
D_FF = 2816
EPS = 1e-6

COL_GATES = 0
COL_UF = 3072
COL_VG = 3584
COL_RG = 4096
COL_VD = 4608
COL_QG = 5120
COL_KG = 5376
PROJ_W = 5632
W_COLS = PROJ_W + 2 * DIFF_QK
A_PAD = 128

SUBLANES = 8
VMEM_LIMIT = 48 * 1024 * 1024
VMEM_LIMIT_BIG = 56 * 1024 * 1024

GLA_C = 64
GLA_LEVELS = (32, 16, 8, 4, 2, 1)
GLA_COARSE = tuple(s for s in GLA_LEVELS if 2 * s >= SUBLANES)
GLA_SHARED = -(-(1 + sum(GLA_C // (2 * s) for s in GLA_COARSE)) // SUBLANES) * SUBLANES
GLA_MROWS = GLA_C * (1 + len(GLA_LEVELS) - len(GLA_COARSE)) + GLA_SHARED
GLA_BB = 8
ROW_TILE = 512
ATTN_TQ = 256
PROJ_TILE_N = 512
FFN_TILE_F = 256
MOD_TILE_N = 768
DFT_ROWS = 256
LOG2_E = math.log2(math.e)


def _choose_tiles(nb, t):
    tm = ROW_TILE if t % ROW_TILE == 0 else t
    tq = ATTN_TQ if t > ATTN_TQ else t // 2
    return tm, tq, math.gcd(nb, GLA_BB), min(t, DFT_ROWS)


def _params(sem, vmem=VMEM_LIMIT):
    return pltpu.CompilerParams(dimension_semantics=sem, vmem_limit_bytes=vmem)


def _dot(a, b):
    return jnp.dot(a, b, preferred_element_type=F32)


def _dot_nt(a, b):
    return lax.dot_general(a, b, (((1,), (1,)), ((), ())), preferred_element_type=F32)


def _split2(x):
    x1 = x.astype(BF16)
    return x1, (x - x1.astype(F32)).astype(BF16)


def _split3(x):
    x1 = x.astype(BF16)
    r1 = x - x1.astype(F32)
    x2 = r1.astype(BF16)
    x3 = (r1 - x2.astype(F32)).astype(BF16)
    return x1, x2, x3


def _mod_kernel(c_ref, w_ref, b_ref, o_ref):
    c = c_ref[...]
    s = c * jax.nn.sigmoid(c)
    s1, s2, s3 = _split3(s)
    w = w_ref[0]
    w1, w2, w3 = _split3(w)
    acc = _dot(s1, w1) + (_dot(s1, w2) + _dot(s2, w1)) + (_dot(s2, w2) + _dot(s1, w3) + _dot(s3, w1))
    o_ref[0] = acc + b_ref[0]


def _modulation(cond, w_mod, b_mod):
    r = cond.shape[0]
    tn = MOD_TILE_N
    n = 6 * D_MODEL
    return pl.pallas_call(
        _mod_kernel,
        grid=(DEPTH, n // tn),
        in_specs=[
            pl.BlockSpec((r, D_MODEL), lambda l, j: (0, 0)),
            pl.BlockSpec((1, D_MODEL, tn), lambda l, j: (l, 0, j)),
            pl.BlockSpec((1, 1, tn), lambda l, j: (l, 0, j)),
        ],
        out_specs=pl.BlockSpec((1, r, tn), lambda l, j: (l, 0, j)),
        out_shape=jax.ShapeDtypeStruct((DEPTH, r, n), F32),
        compiler_params=_params(("arbitrary", "arbitrary")),
        name="modulation",
    )(cond, w_mod, b_mod.reshape(DEPTH, 1, n))


def _inproj_kernel(tn, use_rope, want_f32, *refs):
    (x_ref, sc_ref, sh_ref, gain_ref, w_ref, wa_ref, w2_ref, b2_ref,
     gq_ref, gk_ref, gqp_ref, gkp_ref, grp_ref, rot_ref, cos_ref, sin_ref) = refs[:16]
    proj_ref, g_ref, qn_ref, kn_ref = refs[16:20]
    kf_ref = refs[20] if want_f32 else None
    h_scr = refs[-1]

    x = x_ref[...]
    ms = jnp.mean(x * x, axis=-1, keepdims=True)
    scale = gain_ref[...] * (1.0 + sc_ref[0])
    hb = (x * lax.rsqrt(ms + EPS) * scale + sh_ref[0]).astype(BF16)
    h_scr[...] = hb

    def tile(j):
        proj_ref[:, j * tn:(j + 1) * tn] = _dot(h_scr[...], w_ref[:, j * tn:(j + 1) * tn]).astype(BF16)

    a = _dot(hb, wa_ref[...])
    y_q = _dot(h_scr[...], w_ref[:, PROJ_W:PROJ_W + DIFF_QK])
    y_k = _dot(h_scr[...], w_ref[:, PROJ_W + DIFF_QK:PROJ_W + 2 * DIFF_QK])
    tile(0)
    a1, a2 = _split2(a)
    v1, v2 = _split2(w2_ref[...])
    yb_q, yb_k = y_q.astype(BF16), y_k.astype(BF16)
    sq_q, sq_k = (y_q * y_q).astype(BF16), (y_k * y_k).astype(BF16)
    z = _dot(a1, v1) + (_dot(a1, v2) + _dot(a2, v1)) + b2_ref[...]
    ms_q = _dot(sq_q, grp_ref[...])
    ms_k = _dot(sq_k, grp_ref[...])
    if use_rope:
        rot_q = _dot(yb_q, rot_ref[...])
        rot_k = _dot(yb_k, rot_ref[...])
    tile(1)
    tile(2)
    logsig = jnp.minimum(z, 0.0) - jnp.log1p(jnp.exp(-jnp.abs(z)))
    g_ref[...] = logsig * (1.0 / GATE_TEMP)

    def normed(y, msq, rot, g_ref_, gp_ref_, out_scale):
        r = lax.rsqrt(msq + EPS)
        out = y * r * g_ref_[...]
        if use_rope:
            out = out * cos_ref[...] + rot * r * gp_ref_[...] * sin_ref[...]
        return out * out_scale

    tile(3)
    qn_ref[...] = normed(y_q, ms_q, rot_q if use_rope else None, gq_ref, gqp_ref,
                         DIFF_HEAD_DIM ** -0.5 * LOG2_E).astype(BF16)
    tile(4)
    kn = normed(y_k, ms_k, rot_k if use_rope else None, gk_ref, gkp_ref, 1.0)
    kn_ref[0] = kn.astype(BF16)
    if want_f32:
        kf_ref[...] = kn
    for j in range(5, PROJ_W // tn):
        tile(j)


def _in_projection(x, sc, sh, gain, w_main, w_a, w2, b2, qk_gains, grp, rot, cos, sin,
                   nb, t, use_rope, want_f32, tm):
    m = x.shape[0]
    tn = PROJ_TILE_N
    per = t // tm
    ncond = sc.shape[0]
    cidx = (lambda i: (i // per, 0, 0)) if ncond > 1 else (lambda i: (0, 0, 0))
    const = lambda i: (0, 0)
    once = pl.Buffered(1)
    vec = pl.BlockSpec((1, DIFF_QK), const)
    mat = pl.BlockSpec((DIFF_QK, DIFF_QK), const, pipeline_mode=once)
    tab = pl.BlockSpec((tm, DIFF_QK), lambda i: (i % per, 0))
    out_specs = [
        pl.BlockSpec((tm, PROJ_W), lambda i: (i, 0)),
        pl.BlockSpec((tm, 2 * GLA_K), lambda i: (i, 0)),
        pl.BlockSpec((tm, DIFF_QK), lambda i: (i, 0)),
        pl.BlockSpec((1, tm, DIFF_QK), lambda i: (i // per, i % per, 0)),
    ]
    out_shape = [
        jax.ShapeDtypeStruct((m, PROJ_W), BF16),
        jax.ShapeDtypeStruct((m, 2 * GLA_K), F32),
        jax.ShapeDtypeStruct((m, DIFF_QK), BF16),
        jax.ShapeDtypeStruct((nb, t, DIFF_QK), BF16),
    ]
    if want_f32:
        out_specs.append(pl.BlockSpec((tm, DIFF_QK), lambda i: (i, 0)))
        out_shape.append(jax.ShapeDtypeStruct((m, DIFF_QK), F32))
    gq, gk, gqp, gkp = qk_gains
    return pl.pallas_call(
        functools.partial(_inproj_kernel, tn, use_rope, want_f32),
        grid=(m // tm,),
        in_specs=[
            pl.BlockSpec((tm, D_MODEL), lambda i: (i, 0)),
            pl.BlockSpec((1, 1, D_MODEL), cidx),
            pl.BlockSpec((1, 1, D_MODEL), cidx),
            pl.BlockSpec((1, D_MODEL), const),
            pl.BlockSpec((D_MODEL, W_COLS), const, pipeline_mode=once),
            pl.BlockSpec((D_MODEL, A_PAD), const, pipeline_mode=once),
            pl.BlockSpec((A_PAD, 2 * GLA_K), const, pipeline_mode=once),
            pl.BlockSpec((1, 2 * GLA_K), const),
            vec, vec, vec, vec, mat, mat, tab, tab,
        ],
        out_specs=out_specs,
        out_shape=out_shape,
        scratch_shapes=[pltpu.VMEM((tm, D_MODEL), BF16)],
        compiler_params=_params(("arbitrary",), VMEM_LIMIT_BIG),
        name="in_projection",
    )(x, sc, sh, gain, w_main, w_a, w2, b2, gq, gk, gqp, gkp, grp, rot, cos, sin)


def _diffattn_kernel(seg_len, lam_init, tq, *refs):
    n_seg = len(seg_len)
    q_ref = refs[0]
    kv_refs = refs[1:1 + 2 * n_seg]
    lam_ref, gain_ref, o_ref = refs[1 + 2 * n_seg:4 + 2 * n_seg]
    kall, vt, s_a, s_b, m_a, m_b = refs[4 + 2 * n_seg:]
    nq = q_ref.shape[1] // tq

    off = 0
    for s, tk in enumerate(seg_len):
        kall[off:off + tk, :] = kv_refs[2 * s][0]
        vt[0:DIFF_V_HEAD, off:off + tk] = kv_refs[2 * s + 1][0].astype(F32).T.astype(BF16)
        off += tk
    pad = vt.shape[0] - DIFF_V_HEAD
    first = lax.broadcasted_iota(jnp.int32, (pad, vt.shape[1]), 0) == 0
    vt[DIFF_V_HEAD:, :] = jnp.where(first, 1.0, 0.0).astype(BF16)

    lp = lam_ref[...]
    lam = (jnp.exp(jnp.sum(lp[0:1] * lp[1:2], axis=-1, keepdims=True))
           - jnp.exp(jnp.sum(lp[2:3] * lp[3:4], axis=-1, keepdims=True)) + lam_init)

    def rows(tile):
        return pl.ds(pl.multiple_of(tile * tq, tq), tq)

    def score(tile, s_write, m_write):
        q = q_ref[0, rows(tile), :]
        lane = lax.broadcasted_iota(jnp.int32, q.shape, 1)
        zero = jnp.zeros_like(q)
        q2 = jnp.concatenate([jnp.where(lane < DIFF_HEAD_DIM, q, zero),
                              jnp.where(lane >= DIFF_HEAD_DIM, q, zero)], axis=0)
        sc = _dot_nt(kall[...], q2)
        s_write[...] = sc
        m_write[...] = jnp.broadcast_to(sc.max(axis=0, keepdims=True), m_write.shape)

    def finish(tile, s_read, m_read):
        p = jnp.exp2(s_read[...] - m_read[0:1, :]).astype(BF16)
        acc = _dot(vt[...], p)
        o2 = acc[0:DIFF_V_HEAD] / acc[DIFF_V_HEAD:DIFF_V_HEAD + 1]
        o = (o2[:, 0:tq] - lam * o2[:, tq:]).T
        ms = jnp.mean(o * o, axis=-1, keepdims=True)
        o = o * lax.rsqrt(ms + EPS) * gain_ref[...] * (1.0 - lam_init)
        o_ref[0, rows(tile), :] = o.astype(BF16)

    score(0, s_a, m_a)
    n_pairs = (nq - 1) // 2

    def pair(j, carry):
        score(2 * j + 1, s_b, m_b)
        finish(2 * j, s_a, m_a)
        score(2 * j + 2, s_a, m_a)
        finish(2 * j + 1, s_b, m_b)
        return carry

    lax.fori_loop(0, n_pairs, pair, 0)
    done = 2 * n_pairs
    if (nq - 1) % 2 == 1:
        score(done + 1, s_b, m_b)
        finish(done, s_a, m_a)
        finish(done + 1, s_b, m_b)
    else:
        finish(done, s_a, m_a)


def _diff_attention(qn, segments, lam_p, gain, lam_init, nb, tq_total, tq):
    hd = 2 * DIFF_HEAD_DIM
    in_specs = [pl.BlockSpec((1, tq_total, hd), lambda b, h: (b, 0, h))]
    args = [qn]
    seg_len = tuple(seg[4] for seg in segments)
    tk_all = sum(seg_len)
    for (ka, kc, va, vc, tk) in segments:
        in_specs.append(pl.BlockSpec((1, tk, hd), lambda b, h, kc=kc: (b, 0, kc + h)))
        in_specs.append(pl.BlockSpec((1, tk, DIFF_V_HEAD), lambda b, h, vc=vc: (b, 0, vc + h)))
        args += [ka, va]
    in_specs.append(pl.BlockSpec((4, DIFF_HEAD_DIM), lambda b, h: (0, 0)))
    in_specs.append(pl.BlockSpec((1, DIFF_V_HEAD), lambda b, h: (0, 0)))
    args += [lam_p, gain]
    scratch = [
        pltpu.VMEM((tk_all, hd), BF16),
        pltpu.VMEM((DIFF_V_HEAD + 2 * SUBLANES, tk_all), BF16),
        pltpu.VMEM((tk_all, 2 * tq), F32), pltpu.VMEM((tk_all, 2 * tq), F32),
        pltpu.VMEM((SUBLANES, 2 * tq), F32), pltpu.VMEM((SUBLANES, 2 * tq), F32),
    ]
    return pl.pallas_call(
        functools.partial(_diffattn_kernel, seg_len, lam_init, tq),
        grid=(nb, DIFF_HEADS),
        in_specs=in_specs,
        out_specs=pl.BlockSpec((1, tq_total, DIFF_V_HEAD), lambda b, h: (b, 0, h)),
        out_shape=jax.ShapeDtypeStruct((nb, tq_total, DIFF_V), BF16),
        scratch_shapes=scratch,
        compiler_params=_params(("arbitrary", "arbitrary")),
        name="diff_attention",
    )(*args)


def _gla_tables():
    c = GLA_C
    nl = len(GLA_LEVELS)
    mst = np.zeros((2, GLA_MROWS, c), np.float32)
    xm = np.zeros((2, nl + 1, c), np.float32)
    pm = np.zeros((2, nl + 1, c, c), np.float32)
    idx = np.arange(c)
    for d in range(2):
        ip = idx if d == 0 else c - 1 - idx
        mst[d, 0:c] = (ip[None, :] <= ip[:, None])
        mst[d, c] = 1.0
        shared, full = c + 1, c + GLA_SHARED
        pm[d, 0] = np.eye(c)
        xm[d, 0] = 1.0
        for li, s in enumerate(GLA_LEVELS):
            ref = (ip // (2 * s)) * (2 * s) + s - 1
            if s in GLA_COARSE:
                for first in range(0, c, 2 * s):
                    mst[d, shared] = (ip <= ref[first])
                    shared += 1
            else:
                mst[d, full:full + c] = (ip[None, :] <= ref[:, None])
                full += c
            odd = (ip // s) % 2 == 1
            xm[d, li + 1] = odd
            same = (ip[:, None] // (2 * s)) == (ip[None, :] // (2 * s))
            pm[d, li + 1] = odd[:, None] & (~odd)[None, :] & same
    pm = np.tile(pm, (1, 1, 1, GLA_HEADS))
    rows = np.arange(GLA_HEADS * c)[:, None] // c
    hm = (rows == (np.arange(GLA_K)[None, :] // GLA_DK)).astype(np.float32)
    ykeep = np.concatenate([np.ones((2, 1, c), np.float32), 1.0 - xm[:, 1:]], axis=1)
    hmy = hm[None, None] * np.tile(ykeep, (1, 1, GLA_HEADS))[:, :, :, None]
    xq = np.broadcast_to(xm[:, 1:, :, None], (2, nl, c, GLA_K)).copy()
    vbd = (rows == (np.arange(GLA_V)[None, :] // GLA_DV)).astype(np.float32)
    return mst, hmy, xq, pm, vbd


def _gla_kernel(bb, q_ref, k_ref, v_ref, g_ref, mst_ref, hmy_ref, xq_ref, pm_ref, vbd_ref, s0_ref,
                o_ref, sfin_ref, s_scr):
    c_idx = pl.program_id(2)

    @pl.when(c_idx == 0)
    def _():
        s_scr[...] = s0_ref[:, 0]

    c = GLA_C
    nl = len(GLA_LEVELS)
    m = mst_ref[0]
    seqs = range(bb)

    def stack_keys(yb, li):
        return jnp.concatenate([yb] * GLA_HEADS, axis=0) * hmy_ref[0, li]

    r_all = []
    for b in seqs:
        g1, g2 = _split2(g_ref[b])
        r_all.append(_dot(m, g1) + _dot(m, g2))
    cum = [r[0:c] for r in r_all]
    tot = [r[c:c + 1] for r in r_all]

    def level_ref(r, li):
        s = GLA_LEVELS[li]
        if s in GLA_COARSE:
            first = c + 1 + sum(c // (2 * t) for t in GLA_COARSE if t > s)
            nblk = c // (2 * s)
            return jnp.concatenate([jnp.broadcast_to(r[first + p:first + p + 1], (2 * s, GLA_K))
                                    for p in range(nblk)], axis=0)
        start = c + GLA_SHARED + c * (li - len(GLA_COARSE))
        return r[start:start + c]
    q = [q_ref[b].astype(F32) * (GLA_DK ** -0.5) for b in seqs]
    k = [k_ref[b].astype(F32) for b in seqs]

    o_inter = []
    for b in seqs:
        state = s_scr[b]
        q_dec = stack_keys((q[b] * jnp.exp(cum[b])).astype(BF16), 0)
        o_rows = _dot_nt(q_dec, state.astype(BF16))
        o_inter.append(jnp.concatenate([o_rows[h * c:(h + 1) * c] for h in range(GLA_HEADS)], axis=1))
        k_dec = stack_keys((k[b] * jnp.exp(tot[b] - cum[b])).astype(BF16), 0)
        vf = v_ref[b].astype(F32)
        v_t = jnp.concatenate(
            [jnp.concatenate([vf[:, (2 * p) * GLA_DV:(2 * p + 1) * GLA_DV],
                              vf[:, (2 * p + 1) * GLA_DV:(2 * p + 2) * GLA_DV]], axis=0).T
             for p in range(GLA_HEADS // 2)], axis=1).astype(BF16)
        s_scr[b] = state * jnp.exp(tot[b]) + _dot(v_t, k_dec)

    @pl.when(c_idx == pl.num_programs(2) - 1)
    def _():
        sfin_ref[:, 0] = s_scr[...]

    att = [pm_ref[0, 0] * _dot_nt(q[b].astype(BF16), stack_keys(k_ref[b], 0)) for b in seqs]
    for li in range(nl):
        for b in seqs:
            ref = level_ref(r_all[b], li)
            e = jnp.exp(-jnp.abs(cum[b] - ref))
            xs = (q[b] * e).astype(BF16) * xq_ref[0, li]
            ys = stack_keys((k[b] * e).astype(BF16), li + 1)
            att[b] = att[b] + pm_ref[0, li + 1] * _dot_nt(xs, ys)
    for b in seqs:
        v_bd = jnp.concatenate([v_ref[b]] * GLA_HEADS, axis=0) * vbd_ref[...]
        o_ref[0, b] = (o_inter[b] + _dot(att[b].astype(BF16), v_bd)).astype(BF16)


def _gla(proj3, g3, s0, tables, nb, t, bb):
    mst, hmy, xq, pm, vbd = tables
    nc = t // GLA_C
    nl = len(GLA_LEVELS)

    def chunk(d, c):
        return jnp.where(d == 0, c, nc - 1 - c)

    return pl.pallas_call(
        functools.partial(_gla_kernel, bb),
        grid=(nb // bb, 2, nc),
        in_specs=[
            pl.BlockSpec((bb, GLA_C, GLA_K), lambda b, d, c: (b, chunk(d, c), COL_QG // GLA_K)),
            pl.BlockSpec((bb, GLA_C, GLA_K), lambda b, d, c: (b, chunk(d, c), COL_KG // GLA_K)),
            pl.BlockSpec((bb, GLA_C, GLA_V), lambda b, d, c: (b, chunk(d, c), COL_VG // GLA_V)),
            pl.BlockSpec((bb, GLA_C, GLA_K), lambda b, d, c: (b, chunk(d, c), d)),
            pl.BlockSpec((1, GLA_MROWS, GLA_C), lambda b, d, c: (d, 0, 0)),
            pl.BlockSpec((1, nl + 1, GLA_HEADS * GLA_C, GLA_K), lambda b, d, c: (d, 0, 0, 0)),
            pl.BlockSpec((1, nl, GLA_C, GLA_K), lambda b, d, c: (d, 0, 0, 0)),
            pl.BlockSpec((1, nl + 1, GLA_C, GLA_HEADS * GLA_C), lambda b, d, c: (d, 0, 0, 0)),
            pl.BlockSpec((GLA_HEADS * GLA_C, GLA_V), lambda b, d, c: (0, 0)),
            pl.BlockSpec((bb, 1, GLA_DV, GLA_K), lambda b, d, c: (b, d, 0, 0)),
        ],
        out_specs=[
            pl.BlockSpec((1, bb, GLA_C, GLA_V), lambda b, d, c: (d, b, chunk(d, c), 0)),
            pl.BlockSpec((bb, 1, GLA_DV, GLA_K), lambda b, d, c: (b, d, 0, 0)),
        ],
        out_shape=[
            jax.ShapeDtypeStruct((2, nb, t, GLA_V), BF16),
            jax.ShapeDtypeStruct((nb, 2, GLA_DV, GLA_K), F32),
        ],
        scratch_shapes=[pltpu.VMEM((bb, GLA_DV, GLA_K), F32)],
        compiler_params=_params(("arbitrary", "arbitrary", "arbitrary")),
        name="gla",
    )(proj3, proj3, proj3, g3, mst, hmy, xq, pm, vbd, s0)


def _fourier_kernel(t, u_ref, bcs_ref, ct_ref, o_ref, y_scr):
    i = pl.program_id(1)

    @pl.when(i == 0)
    def _():
        rows = min(t, ROW_TILE)
        for r in range(t // rows):
            y = _dot(u_ref[0, r * rows:(r + 1) * rows, :], bcs_ref[...])
            y_scr[r * rows:(r + 1) * rows, :] = y[:, :FOU_W].astype(BF16)
            y_scr[t + r * rows:t + (r + 1) * rows, :] = y[:, FOU_W:].astype(BF16)

    o_ref[0] = _dot(ct_ref[...], y_scr[...]).astype(BF16)


def _fourier(proj3, bcs, ct, nb, t, tm):
    return pl.pallas_call(
        functools.partial(_fourier_kernel, t),
        grid=(nb, t // tm),
        in_specs=[
            pl.BlockSpec((1, t, FOU_W), lambda b, i: (b, 0, COL_UF // FOU_W)),
            pl.BlockSpec((FOU_W, 2 * FOU_W), lambda b, i: (0, 0)),
            pl.BlockSpec((tm, 2 * t), lambda b, i: (i, 0)),
        ],
        out_specs=pl.BlockSpec((1, tm, FOU_W), lambda b, i: (b, i, 0)),
        out_shape=jax.ShapeDtypeStruct((nb, t, FOU_W), BF16),
        scratch_shapes=[pltpu.VMEM((2 * t, FOU_W), BF16)],
        compiler_params=_params(("arbitrary", "arbitrary")),
        name="fourier_mix",
    )(proj3, bcs, ct)


FFT_N = 64
FFT_G = 8
FFT_CW = 256


def _fft_kernel(u_ref, bcs_ref, m1_ref, q2_ref, o_ref, z_scr, a_scr):
    n, gsz, cw = FFT_N, FFT_G, FFT_CW
    rows = n * gsz
    for r in range(n // gsz):
        y = _dot(u_ref[0, r * rows:(r + 1) * rows, :], bcs_ref[...])
        z_scr[gsz * r:gsz * (r + 1)] = y.reshape(gsz, n, 2 * cw)
    for g in range(n // gsz):
        zg = z_scr[:, gsz * g:gsz * (g + 1), :].reshape(rows, 2 * cw)
        rhs = jnp.concatenate([zg[:, :cw], zg[:, cw:]], axis=0).astype(BF16)
        a_scr[g] = _dot(m1_ref[...], rhs).astype(BF16)
    for f in range(0, n // gsz, 2):
        res = []
        for ff in (f, f + 1):
            slabs = [a_scr[g, part * rows + n * ff:part * rows + n * (ff + 1), :]
                     for part in range(2) for g in range(n // gsz)]
            res.append(_dot(q2_ref[ff], jnp.concatenate(slabs, axis=0)).reshape(n, gsz, cw))
        o_ref[0, :, gsz * f:gsz * (f + 2), :] = jnp.concatenate(res, axis=1).astype(BF16)


def _fft_tables():
    n, gsz, t = FFT_N, FFT_G, FFT_N * FFT_N
    k = np.arange(FOU_GROUP_W)
    ang = 2.0 * np.pi * ((k[:, None] * k[None, :]) % FOU_GROUP_W) / FOU_GROUP_W
    eye2 = np.eye(FFT_CW // FOU_GROUP_W)
    bcs = np.concatenate([np.kron(eye2, np.cos(ang)), -np.kron(eye2, np.sin(ang))], axis=1) * FOU_GROUP_W ** -0.5
    i = np.arange(n)
    a1 = 2.0 * np.pi * ((i[:, None] * i[None, :]) % n) / n
    c1 = np.kron(np.cos(a1), np.eye(gsz))
    s1 = np.kron(np.sin(a1), np.eye(gsz))
    m1 = np.block([[c1, s1], [-s1, c1]])
    ngrp = n // gsz
    shape = (ngrp, n * gsz, n * gsz)
    f = lax.broadcasted_iota(jnp.int32, shape, 0)
    row = lax.broadcasted_iota(jnp.int32, shape, 1)
    col = lax.broadcasted_iota(jnp.int32, shape, 2)
    f2, a = row // gsz, row % gsz
    g, ap, j = col // (gsz * gsz), (col // gsz) % gsz, col % gsz
    freq = gsz * f + a + n * f2
    theta = (((gsz * g + j) * freq) % t).astype(F32) * (2.0 * math.pi / t)
    keep = (a == ap)
    scale = t ** -0.5
    qc = jnp.where(keep, jnp.cos(theta) * scale, 0.0)
    qs = jnp.where(keep, jnp.sin(theta) * scale, 0.0)
    q2 = jnp.concatenate([qc, qs], axis=2).astype(BF16)
    return jnp.asarray(bcs, BF16), jnp.asarray(m1, BF16), q2


def _fourier_fft(proj3, tables, nb):
    bcs, m1, q2 = tables
    n, gsz, cw = FFT_N, FFT_G, FFT_CW
    t = n * n
    once = pl.Buffered(1)
    out = pl.pallas_call(
        _fft_kernel,
        grid=(nb, FOU_W // cw),
        in_specs=[
            pl.BlockSpec((1, t, cw), lambda b, h: (b, 0, COL_UF // cw + h)),
            pl.BlockSpec((cw, 2 * cw), lambda b, h: (0, 0), pipeline_mode=once),
            pl.BlockSpec((2 * n * gsz, 2 * n * gsz), lambda b, h: (0, 0), pipeline_mode=once),
            pl.BlockSpec((n // gsz, n * gsz, 2 * n * gsz), lambda b, h: (0, 0, 0), pipeline_mode=once),
        ],
        out_specs=pl.BlockSpec((1, n, n, cw), lambda b, h: (b, 0, 0, h)),
        out_shape=jax.ShapeDtypeStruct((nb, n, n, FOU_W), BF16),
        scratch_shapes=[pltpu.VMEM((n, n, 2 * cw), F32), pltpu.VMEM((n // gsz, 2 * n * gsz, cw), BF16)],
        compiler_params=_params(("arbitrary", "arbitrary")),
        name="fourier_fft",
    )(proj3, bcs, m1, q2)
    return out.reshape(nb, t, FOU_W)


def _dft_tables(t):
    lo = min(t, 64)
    hi = t // lo
    f = jnp.arange(t, dtype=jnp.int32)[None, :]
    unit = 2.0 * math.pi / t
    a = ((lo * jnp.arange(hi, dtype=jnp.int32)[:, None] * f) % t).astype(F32) * unit
    b = ((jnp.arange(lo, dtype=jnp.int32)[:, None] * f) % t).astype(F32) * unit
    ca, sa = jnp.cos(a)[:, None, :], jnp.sin(a)[:, None, :]
    cb, sb = jnp.cos(b)[None, :, :], jnp.sin(b)[None, :, :]
    scale = t ** -0.5
    cos = ((ca * cb - sa * sb) * scale).reshape(t, t)
    msin = ((sa * cb + ca * sb) * -scale).reshape(t, t)
    return jnp.concatenate([cos, msin], axis=1).astype(BF16)


def _channel_dft_table():
    n = np.arange(FOU_GROUP_W)
    ang = 2.0 * np.pi * ((n[:, None] * n[None, :]) % FOU_GROUP_W) / FOU_GROUP_W
    eye = np.eye(FOU_GROUPS)
    bc = np.kron(eye, np.cos(ang)) * FOU_GROUP_W ** -0.5
    bs = np.kron(eye, np.sin(ang)) * FOU_GROUP_W ** -0.5
    return np.concatenate([bc, bs], axis=1).astype(np.float32)


def _merge_kernel(x_ref, g1_ref, fm_ref, of_ref, ob_ref, r_ref, od_ref, gtf_ref, gtg_ref, gtd_ref,
                  wf_ref, wg_ref, wd_ref, wo_ref, gn_ref, grp_ref, o_ref):
    y_f = _dot(fm_ref[...].astype(BF16), wf_ref[...])
    og = of_ref[0].astype(F32) + ob_ref[0].astype(F32)
    ms = _dot((og * og).astype(BF16), grp_ref[...])
    r = r_ref[...].astype(F32)
    og = og * lax.rsqrt(ms + EPS) * gn_ref[...] * (r * jax.nn.sigmoid(r))
    y_g = _dot(og.astype(BF16), wg_ref[...])
    y_d = _dot(od_ref[...], wd_ref[...])
    merged = (jax.nn.sigmoid(gtf_ref[...].astype(F32)) * y_f + jax.nn.sigmoid(gtg_ref[...].astype(F32)) * y_g
              + jax.nn.sigmoid(gtd_ref[...].astype(F32)) * y_d)
    o_ref[...] = x_ref[...] + g1_ref[0] * _dot(merged.astype(BF16), wo_ref[...])


def _merge(x, g1, proj, fm, o_gla, od, w_fou, w_gla_o, w_diff_o, w_out, gn, grp, tokens_per_cond, tm):
    m = x.shape[0]
    per = tokens_per_cond // tm
    ncond = g1.shape[0]
    cidx = (lambda i: (i // per, 0, 0)) if ncond > 1 else (lambda i: (0, 0, 0))
    once = pl.Buffered(1)
    half = pl.BlockSpec((FOU_W, D_MODEL), lambda i: (0, 0), pipeline_mode=once)
    return pl.pallas_call(
        _merge_kernel,
        grid=(m // tm,),
        in_specs=[
            pl.BlockSpec((tm, D_MODEL), lambda i: (i, 0)),
            pl.BlockSpec((1, 1, D_MODEL), cidx),
            pl.BlockSpec((tm, FOU_W), lambda i: (i, 0)),
            pl.BlockSpec((1, tm, GLA_V), lambda i: (0, i, 0)),
            pl.BlockSpec((1, tm, GLA_V), lambda i: (1, i, 0)),
            pl.BlockSpec((tm, GLA_V), lambda i: (i, COL_RG // GLA_V)),
            pl.BlockSpec((tm, DIFF_V), lambda i: (i, 0)),
            pl.BlockSpec((tm, D_MODEL), lambda i: (i, COL_GATES // D_MODEL)),
            pl.BlockSpec((tm, D_MODEL), lambda i: (i, COL_GATES // D_MODEL + 1)),
            pl.BlockSpec((tm, D_MODEL), lambda i: (i, COL_GATES // D_MODEL + 2)),
            half, half, half,
            pl.BlockSpec((D_MODEL, D_MODEL), lambda i: (0, 0), pipeline_mode=once),
            pl.BlockSpec((1, GLA_V), lambda i: (0, 0)),
            pl.BlockSpec((GLA_V, GLA_V), lambda i: (0, 0)),
        ],
        out_specs=pl.BlockSpec((tm, D_MODEL), lambda i: (i, 0)),
        out_shape=jax.ShapeDtypeStruct((m, D_MODEL), F32),
        compiler_params=_params(("arbitrary",)),
        name="merge",
    )(x, g1, fm, o_gla, o_gla, proj, od, proj, proj, proj, w_fou, w_gla_o, w_diff_o, w_out, gn, grp)


def _ffn_kernel(tf, x_ref, sc_ref, sh_ref, g2_ref, gain_ref, wg_ref, wu_ref, wd_ref, o_ref, h_scr):
    x = x_ref[...]
    ms = jnp.mean(x * x, axis=-1, keepdims=True)
    scale = gain_ref[...] * (1.0 + sc_ref[0])
    h_scr[...] = (x * lax.rsqrt(ms + EPS) * scale + sh_ref[0]).astype(BF16)
    acc = None
    for f in range(D_FF // tf):
        cols = slice(f * tf, (f + 1) * tf)
        gate = _dot(h_scr[...], wg_ref[:, cols])
        up = _dot(h_scr[...], wu_ref[:, cols])
        a = (gate * jax.nn.sigmoid(gate) * up).astype(BF16)
        part = _dot(a, wd_ref[cols, :])
        acc = part if acc is None else acc + part
    o_ref[...] = x_ref[...] + g2_ref[0] * acc


def _ffn(x, sc, sh, g2, gain, w_gate, w_up, w_down, tokens_per_cond, tm):
    m = x.shape[0]
    tf = FFN_TILE_F
    per = tokens_per_cond // tm
    ncond = sc.shape[0]
    cidx = (lambda i: (i // per, 0, 0)) if ncond > 1 else (lambda i: (0, 0, 0))
    const = lambda i: (0, 0)
    once = pl.Buffered(1)
    return pl.pallas_call(
        functools.partial(_ffn_kernel, tf),
        grid=(m // tm,),
        in_specs=[
            pl.BlockSpec((tm, D_MODEL), lambda i: (i, 0)),
            pl.BlockSpec((1, 1, D_MODEL), cidx),
            pl.BlockSpec((1, 1, D_MODEL), cidx),
            pl.BlockSpec((1, 1, D_MODEL), cidx),
            pl.BlockSpec((1, D_MODEL), const),
            pl.BlockSpec((D_MODEL, D_FF), const, pipeline_mode=once),
            pl.BlockSpec((D_MODEL, D_FF), const, pipeline_mode=once),
            pl.BlockSpec((D_FF, D_MODEL), const, pipeline_mode=once),
        ],
        out_specs=pl.BlockSpec((tm, D_MODEL), lambda i: (i, 0)),
        out_shape=jax.ShapeDtypeStruct((m, D_MODEL), F32),
        scratch_shapes=[pltpu.VMEM((tm, D_MODEL), BF16)],
        compiler_params=_params(("arbitrary",)),
        name="ffn",
    )(x, sc, sh, g2, gain, w_gate, w_up, w_down)


def _group_mean_matrix(width, group):
    idx = np.arange(width) // group
    return (idx[:, None] == idx[None, :]).astype(np.float32) / group


def _rotate_half_matrix():
    r = np.zeros((DIFF_HEAD_DIM, DIFF_HEAD_DIM), np.float32)
    for axis in range(2):
        base = axis * ROPE_AXIS_DIM
        for f in range(ROPE_FREQS):
            r[base + ROPE_FREQS + f, base + f] = -1.0
            r[base + f, base + ROPE_FREQS + f] = 1.0
    return np.kron(np.eye(DIFF_QK // DIFF_HEAD_DIM), r).astype(np.float32)


def _rotate_half_perm():
    p = np.zeros((DIFF_HEAD_DIM,), np.int32)
    for axis in range(2):
        base = axis * ROPE_AXIS_DIM
        for f in range(ROPE_FREQS):
            p[base + f] = base + ROPE_FREQS + f
            p[base + ROPE_FREQS + f] = base + f
    return p


def _rope_tables(n_tokens):
    rows = n_tokens // GRID_W
    row = jnp.repeat(jnp.arange(rows), GRID_W).astype(F32)
    col = jnp.tile(jnp.arange(GRID_W), rows).astype(F32)
    inv = ROPE_BASE ** (-jnp.arange(ROPE_FREQS, dtype=F32) * 2.0 / ROPE_AXIS_DIM)
    ang_r = row[:, None] * inv
    ang_c = col[:, None] * inv
    ang = jnp.concatenate([ang_r, ang_r, ang_c, ang_c], axis=-1)
    reps = DIFF_QK // DIFF_HEAD_DIM
    return jnp.tile(jnp.cos(ang), (1, reps)), jnp.tile(jnp.sin(ang), (1, reps))


def _pack_state(s):
    return jnp.transpose(s, (0, 1, 4, 2, 3)).reshape(s.shape[0], 2, GLA_DV, GLA_K)


def _unpack_state(sp):
    s = sp.reshape(sp.shape[0], 2, GLA_DV, GLA_HEADS, GLA_DK)
    return jnp.transpose(s, (0, 1, 3, 4, 2))


def _run_path(x, mods, layers, consts, nb, t, ctx):
    m = nb * t
    tm, tq, gla_bb, dft_rows = _choose_tiles(nb, t)
    x = x.reshape(m, D_MODEL)
    is_latent = ctx is not None
    cos, sin = consts['rope'] if is_latent else (consts['ones'], consts['ones'])
    ct = consts['ct'][t]
    k_out, v_out, s_out = [], [], []
    for l, p in enumerate(layers):
        sh1, sc1, g1, sh2, sc2, g2 = [mods[l][:, None, j * D_MODEL:(j + 1) * D_MODEL] for j in range(6)]
        outs = _in_projection(x, sc1, sh1, p['norm1'], p['w_main'], p['w_a'], p['w2'], p['b2'],
                              p['qk_gains'], consts['grp64'], consts['rot'], cos, sin,
                              nb, t, is_latent, not is_latent, tm)
        proj, g, qn, kn = outs[:4]
        proj3 = proj.reshape(nb, t, PROJ_W)
        lam_init = 0.8 - 0.6 * math.exp(-0.3 * l)
        segments = [(kn, 0, proj3, COL_VD // DIFF_V_HEAD, t)]
        if is_latent:
            ck = ctx[1][:, l].reshape(nb, -1, DIFF_QK).astype(BF16)
            cv = ctx[2][:, l].reshape(nb, -1, DIFF_V).astype(BF16)
            segments.append((ck, 0, cv, 0, ck.shape[1]))
        od = _diff_attention(qn.reshape(nb, t, DIFF_QK), segments, p['diff_lambda'], p['diff_norm'],
                             lam_init, nb, t, tq)

        if is_latent:
            s0 = _pack_state(ctx[0][:, l])
        else:
            s0 = jnp.zeros((nb, 2, GLA_DV, GLA_K), F32)
        o_gla, s_fin = _gla(proj3, g.reshape(nb, t, 2 * GLA_K), s0, consts['gla'], nb, t, gla_bb)

        if t == FFT_N * FFT_N:
            fm = _fourier_fft(proj3, consts['fft'], nb)
        else:
            fm = _fourier(proj3, consts['bcs'], ct, nb, t, dft_rows)

        x = _merge(x, g1, proj, fm.reshape(m, FOU_W), o_gla.reshape(2, m, GLA_V), od.reshape(m, DIFF_V),
                   p['w_fou'], p['w_gla_o'], p['w_diff_o'], p['w_out'], p['gla_norm'], consts['grp128'],
                   t, tm)
        x = _ffn(x, sc2, sh2, g2, p['norm2'], p['w_ff_gate'], p['w_ff_up'], p['w_ff_down'], t, tm)

        if not is_latent:
            k_out.append(outs[4].reshape(nb, t, DIFF_HEADS, 2, DIFF_HEAD_DIM))
            v_out.append(proj3[:, :, COL_VD:COL_VD + DIFF_V].astype(F32).reshape(nb, t, DIFF_HEADS, DIFF_V_HEAD))
            s_out.append(_unpack_state(s_fin))
    return x.reshape(nb, t, D_MODEL), k_out, v_out, s_out


def kernel(x_prompt, x_sample, c, cache_diff_k, cache_diff_v, state_gla, c_ctx, w_mod, b_mod, norm1, norm2, w_in, w_gla_a2, b_gla_a, gla_norm, diff_qk_norm, diff_lambda, diff_norm, w_fou, w_gla_o, w_diff_o, w_out, w_ff_gate, w_ff_up, w_ff_down):
    nb_ctx, t_ctx, _ = x_prompt.shape
    nb_lat, t_lat, _ = x_sample.shape

    n_cond = 1 + nb_lat
    r_pad = -(-n_cond // 8) * 8
    cond = jnp.zeros((r_pad, D_MODEL), F32).at[0].set(c_ctx).at[1:n_cond].set(c)
    mods = _modulation(cond, w_mod, b_mod)
    mods_ctx = [mods[l, 0:1] for l in range(DEPTH)]
    mods_lat = [mods[l, 1:n_cond] for l in range(DEPTH)]

    a0 = 2048
    a1 = a0 + 2 * GATE_RANK
    pieces = {
        'u_f': (0, 512), 'q_g': (512, 768), 'k_g': (768, 1024), 'v_g': (1024, 1536), 'r_g': (1536, 2048),
        'q_d': (a1, a1 + 512), 'k_d': (a1 + 512, a1 + 1024), 'v_d': (a1 + 1024, a1 + 1536),
        'gates': (a1 + 1536, a1 + 1536 + 3072),
    }
    order = ['gates', 'u_f', 'v_g', 'r_g', 'v_d', 'q_g', 'k_g', 'q_d', 'k_d']
    w_main = jnp.concatenate([w_in[:, :, pieces[n][0]:pieces[n][1]] for n in order], axis=-1).astype(BF16)
    w_a = jnp.pad(w_in[:, :, a0:a1], ((0, 0), (0, 0), (0, A_PAD - 2 * GATE_RANK))).astype(BF16)
    w2 = jnp.zeros((DEPTH, A_PAD, 2 * GLA_K), F32)
    w2 = w2.at[:, 0:GATE_RANK, 0:GLA_K].set(w_gla_a2[:, 0])
    w2 = w2.at[:, GATE_RANK:2 * GATE_RANK, GLA_K:].set(w_gla_a2[:, 1])
    b2 = b_gla_a.reshape(DEPTH, 1, 2 * GLA_K)

    perm = _rotate_half_perm()
    reps = DIFF_QK // DIFF_HEAD_DIM
    layers = []
    for l in range(DEPTH):
        gq = diff_qk_norm[l, 0]
        gk = diff_qk_norm[l, 1]
        layers.append({
            'norm1': norm1[l][None, :], 'norm2': norm2[l][None, :],
            'w_main': w_main[l], 'w_a': w_a[l], 'w2': w2[l], 'b2': b2[l],
            'qk_gains': tuple(jnp.tile(v, reps)[None, :] for v in (gq, gk, gq[perm], gk[perm])),
            'diff_lambda': diff_lambda[l], 'diff_norm': diff_norm[l][None, :],
            'gla_norm': jnp.tile(gla_norm[l], GLA_HEADS)[None, :],
            'w_fou': w_fou[l].astype(BF16), 'w_gla_o': w_gla_o[l].astype(BF16),
            'w_diff_o': w_diff_o[l].astype(BF16), 'w_out': w_out[l].astype(BF16),
            'w_ff_gate': w_ff_gate[l].astype(BF16), 'w_ff_up': w_ff_up[l].astype(BF16),
            'w_ff_down': w_ff_down[l].astype(BF16),
        })

    mst, hmy, xq, pm, vbd = _gla_tables()
    consts = {
        'grp64': jnp.asarray(_group_mean_matrix(DIFF_QK, DIFF_HEAD_DIM), BF16),
        'grp128': jnp.asarray(_group_mean_matrix(GLA_V, GLA_DV), BF16),
        'rot': jnp.asarray(_rotate_half_matrix(), BF16),
        'rope': _rope_tables(t_lat),
        'ones': jnp.ones((t_ctx, DIFF_QK), F32),
        'bcs': jnp.asarray(_channel_dft_table(), BF16),
        'ct': {tt: (None if tt == FFT_N * FFT_N else _dft_tables(tt)) for tt in {t_ctx, t_lat}},
        'fft': _fft_tables(),
        'gla': (jnp.asarray(mst, BF16), jnp.asarray(hmy, BF16), jnp.asarray(xq, BF16), jnp.asarray(pm),
                jnp.asarray(vbd, BF16)),
    }

    y_prompt, k_list, v_list, s_list = _run_path(x_prompt, mods_ctx, layers, consts, nb_ctx, t_ctx, None)
    y_sample, _, _, _ = _run_path(x_sample, mods_lat, layers, consts, nb_lat, t_lat,
                                  (state_gla, cache_diff_k, cache_diff_v))
    return (y_prompt, y_sample, jnp.stack(k_list, axis=1), jnp.stack(v_list, axis=1),
            jnp.stack(s_list, axis=1))
```
